```python
import math
import jax, jax.numpy as jnp
from jax import lax
import numpy as np

D_MODEL = 2048
BATCH = 4
SEQ = 2048
DEPTH = 2
DEC_BATCH = 128
DEC_SEQ = 4
PAST_LEN = 16384
PAGE_SIZE = 128

D_RET = D_MODEL // 2
RET_HEADS = 4
RET_HEAD_DIM = D_RET // RET_HEADS
D_POOL = D_MODEL - D_RET
POOL_WINDOWS = (2, 4, 8, 16)
N_POOL_GROUPS = len(POOL_WINDOWS)
POOL_GROUP_DIM = D_POOL // N_POOL_GROUPS
POOL_BUF = max(POOL_WINDOWS) - 1
D_IN = 4 * D_RET + D_POOL
N_MEM = 256
MEM_HEADS = 4
MEM_HEAD_DIM = D_MODEL // MEM_HEADS
D_FF = 4 * D_MODEL
RET_CHUNK = 128
ROPE_BASE = 10000.0
EPS = 1e-6

kernel_name = 'hymba_retention_pool_memory_decode_step'


def rmsnorm(x, w):
    xf = x.astype(jnp.float32)
    y = xf * lax.rsqrt(jnp.mean(xf * xf, axis=-1, keepdims=True) + EPS)
    return (y * w.astype(jnp.float32)).astype(x.dtype)


def rotary(x, pos):
    half = x.shape[-1] // 2
    inv = ROPE_BASE ** (-jnp.arange(half, dtype=jnp.float32) / half)
    ang = pos.astype(jnp.float32)[:, None] * inv[None, :]
    cos = jnp.cos(ang)[None, :, None, :]
    sin = jnp.sin(ang)[None, :, None, :]
    xf = x.astype(jnp.float32)
    x1, x2 = xf[..., :half], xf[..., half:]
    return jnp.concatenate([x1 * cos - x2 * sin, x2 * cos + x1 * sin], axis=-1)


def log_gamma():
    return jnp.log1p(-jnp.exp2(-5.0 - jnp.arange(RET_HEADS, dtype=jnp.float32)))


def retention(q, k, v, s0, chunk):
    B, T, H, Dk = q.shape
    Dv = v.shape[-1]
    nc = T // chunk
    lg = log_gamma()
    idx = jnp.arange(chunk, dtype=jnp.float32)
    diff = idx[:, None] - idx[None, :]
    decay_in = jnp.where(diff[None] >= 0.0,
                         jnp.exp(lg[:, None, None] * jnp.maximum(diff, 0.0)[None]), 0.0)
    decay_q = jnp.exp(lg[None, :] * (idx[:, None] + 1.0))
    decay_k = jnp.exp(lg[None, :] * (chunk - 1.0 - idx[:, None]))
    decay_chunk = jnp.exp(lg * chunk)

    def split(a):
        return a.reshape(B, nc, chunk, H, a.shape[-1]).swapaxes(0, 1)

    def step(s, qkv):
        qc, kc, vc = qkv
        scores = jnp.einsum('bihd,bjhd->bhij', qc, kc) * decay_in[None]
        o = jnp.einsum('bhij,bjhe->bihe', scores, vc)
        o = o + jnp.einsum('bihd,bhde->bihe', qc * decay_q[None, :, :, None], s)
        s = s * decay_chunk[None, :, None, None] + jnp.einsum(
            'bjhd,bjhe->bhde', kc * decay_k[None, :, :, None], vc)
        return s, o

    s, o = lax.scan(step, s0, (split(q), split(k), split(v)))
    return o.swapaxes(0, 1).reshape(B, T, H, Dv), s


def pool_mix(u, buf, n_prev):
    T = u.shape[1]
    uf = u.astype(jnp.float32)
    ext = jnp.concatenate([buf.astype(jnp.float32), uf], axis=1)
    cs = jnp.pad(jnp.cumsum(ext, axis=1), ((0, 0), (1, 0), (0, 0)))
    end = cs[:, POOL_BUF + 1:]
    outs = []
    for g, w in enumerate(POOL_WINDOWS):
        sl = slice(g * POOL_GROUP_DIM, (g + 1) * POOL_GROUP_DIM)
        start = cs[:, POOL_BUF + 1 - w: POOL_BUF + 1 - w + T, sl]
        cnt = jnp.minimum(jnp.arange(T) + n_prev + 1, w).astype(jnp.float32)
        outs.append((end[..., sl] - start) / cnt[None, :, None] - uf[..., sl])
    pooled = jnp.stack(outs, axis=2)
    return pooled, ext[:, -POOL_BUF:].astype(u.dtype)


def mem_kv(mem, norm_w, w_k, w_v):
    B = mem.shape[0]
    m = rmsnorm(mem, norm_w)
    k = (m @ w_k).reshape(B, N_MEM, MEM_HEADS, MEM_HEAD_DIM)
    v = (m @ w_v).reshape(B, N_MEM, MEM_HEADS, MEM_HEAD_DIM)
    return k, v


def cross_attend(h, mk, mv, w_q):
    B, T, _ = h.shape
    q = (h @ w_q).reshape(B, T, MEM_HEADS, MEM_HEAD_DIM)
    s = jnp.einsum('bthd,bmhd->bhtm', q, mk).astype(jnp.float32) * (MEM_HEAD_DIM ** -0.5)
    p = jax.nn.softmax(s, axis=-1).astype(h.dtype)
    return jnp.einsum('bhtm,bmhd->bthd', p, mv).reshape(B, T, D_MODEL)


def layer(x, pos, s_ret, p_buf, n_prev, chunk, mk, mv, lw):
    B, T, _ = x.shape
    h = rmsnorm(x, lw['attn_norm_w'])
    z = h @ lw['w_in']
    q, k, v, g, u = jnp.split(z, [D_RET, 2 * D_RET, 3 * D_RET, 4 * D_RET], axis=-1)
    heads = lambda a: a.reshape(B, T, RET_HEADS, RET_HEAD_DIM)
    qr = rotary(heads(q), pos)
    kr = rotary(heads(k), pos) * (RET_HEAD_DIM ** -0.5)
    o, s_new = retention(qr, kr, heads(v).astype(jnp.float32), s_ret.astype(jnp.float32), chunk)
    o = o * lax.rsqrt(jnp.mean(o * o, axis=-1, keepdims=True) + EPS)
    o = o.reshape(B, T, D_RET) * lw['ret_norm_w'].astype(jnp.float32)
    o = (jax.nn.silu(g.astype(jnp.float32)) * o).astype(x.dtype)
    pooled, buf_new = pool_mix(u, p_buf, n_prev)
    pm = jnp.einsum('btgc,gcd->btgd', pooled.astype(x.dtype), lw['pool_w']).reshape(B, T, D_POOL)
    pm = pm * lw['pool_scale']
    x = x + jnp.concatenate([o, pm], axis=-1) @ lw['w_out']
    h = rmsnorm(x, lw['xattn_norm_w'])
    x = x + cross_attend(h, mk, mv, lw['w_xq']) @ lw['w_xo']
    h = rmsnorm(x, lw['mlp_norm_w'])
    x = x + jnp.square(jax.nn.relu(h @ lw['w_up'])) @ lw['w_down']
    return x, s_new.astype(x.dtype), buf_new


def setup_inputs(seed: int = 0) -> dict:
    key = jax.random.key(seed)
    ks = jax.random.split(key, 24)
    f32 = jnp.float32
    nrm = lambda k, shape, s: jax.random.normal(k, shape, f32) * s
    gain = lambda k, shape: 1.0 + 0.02 * jax.random.normal(k, shape, f32)
    return {
        'x_prompt': nrm(ks[0], (BATCH, SEQ, D_MODEL), 1.0),
        'x_sample': nrm(ks[1], (DEC_BATCH, DEC_SEQ, D_MODEL), 1.0),
        'mem_prompt': nrm(ks[2], (BATCH, N_MEM, D_MODEL), 1.0),
        'state_ret': nrm(ks[3], (DEPTH, DEC_BATCH, RET_HEADS, RET_HEAD_DIM, RET_HEAD_DIM), 0.5),
        'state_pool': nrm(ks[4], (DEPTH, DEC_BATCH, POOL_BUF, D_POOL), 1.0),
        'cache_mem_k': nrm(ks[5], (DEPTH, DEC_BATCH, N_MEM, MEM_HEADS, MEM_HEAD_DIM), 1.0),
        'cache_mem_v': nrm(ks[6], (DEPTH, DEC_BATCH, N_MEM, MEM_HEADS, MEM_HEAD_DIM), 1.0),
        'attn_norm_w': gain(ks[7], (DEPTH, D_MODEL)),
        'w_in': nrm(ks[8], (DEPTH, D_MODEL, D_IN), D_MODEL ** -0.5),
        'ret_norm_w': gain(ks[9], (DEPTH, D_RET)),
        'pool_w': nrm(ks[10], (DEPTH, N_POOL_GROUPS, POOL_GROUP_DIM, POOL_GROUP_DIM), POOL_GROUP_DIM ** -0.5),
        'pool_scale': gain(ks[11], (DEPTH, D_POOL)),
        'w_out': nrm(ks[12], (DEPTH, D_MODEL, D_MODEL), D_MODEL ** -0.5),
        'xattn_norm_w': gain(ks[13], (DEPTH, D_MODEL)),
        'mem_norm_w': gain(ks[14], (DEPTH, D_MODEL)),
        'w_xq': nrm(ks[15], (DEPTH, D_MODEL, D_MODEL), D_MODEL ** -0.5),
        'w_mk': nrm(ks[16], (DEPTH, D_MODEL, D_MODEL), D_MODEL ** -0.5),
        'w_mv': nrm(ks[17], (DEPTH, D_MODEL, D_MODEL), D_MODEL ** -0.5),
        'w_xo': nrm(ks[18], (DEPTH, D_MODEL, D_MODEL), D_MODEL ** -0.5),
        'mlp_norm_w': gain(ks[19], (DEPTH, D_MODEL)),
        'w_up': nrm(ks[20], (DEPTH, D_MODEL, D_FF), D_MODEL ** -0.5),
        'w_down': nrm(ks[21], (DEPTH, D_FF, D_MODEL), D_FF ** -0.5),
        'final_norm_w': gain(ks[22], (D_MODEL,)),
    }


def reference(x_prompt, x_sample, mem_prompt, state_ret, state_pool, cache_mem_k, cache_mem_v,
              attn_norm_w, w_in, ret_norm_w, pool_w, pool_scale, w_out, xattn_norm_w, mem_norm_w,
              w_xq, w_mk, w_mv, w_xo, mlp_norm_w, w_up, w_down, final_norm_w):
    pos_p = jnp.arange(SEQ)
    pos_s = jnp.arange(DEC_SEQ) + PAST_LEN
    xp, xs = x_prompt, x_sample
    ret_p, buf_p, mk_p, mv_p, ret_s, buf_s = [], [], [], [], [], []
    for l in range(DEPTH):
        lw = {'attn_norm_w': attn_norm_w[l], 'w_in': w_in[l], 'ret_norm_w': ret_norm_w[l],
              'pool_w': pool_w[l], 'pool_scale': pool_scale[l], 'w_out': w_out[l],
              'xattn_norm_w': xattn_norm_w[l], 'w_xq': w_xq[l], 'w_xo': w_xo[l],
              'mlp_norm_w': mlp_norm_w[l], 'w_up': w_up[l], 'w_down': w_down[l]}
        mk, mv = mem_kv(mem_prompt, mem_norm_w[l], w_mk[l], w_mv[l])
        s0 = jnp.zeros((BATCH, RET_HEADS, RET_HEAD_DIM, RET_HEAD_DIM), jnp.float32)
        b0 = jnp.zeros((BATCH, POOL_BUF, D_POOL), xp.dtype)
        xp, sp, bp = layer(xp, pos_p, s0, b0, 0, RET_CHUNK, mk, mv, lw)
        ret_p.append(sp)
        buf_p.append(bp)
        mk_p.append(mk)
        mv_p.append(mv)
        xs, ss, bs = layer(xs, pos_s, state_ret[l], state_pool[l], PAST_LEN, DEC_SEQ,
                           cache_mem_k[l], cache_mem_v[l], lw)
        ret_s.append(ss)
        buf_s.append(bs)
    y_prompt = rmsnorm(xp, final_norm_w)
    y_sample = rmsnorm(xs, final_norm_w)
    return (y_prompt, y_sample, jnp.stack(ret_p), jnp.stack(buf_p), jnp.stack(mk_p), jnp.stack(mv_p),
            jnp.stack(ret_s), jnp.stack(buf_s))
```

```python
import functools
import math

import numpy as np
import jax
import jax.numpy as jnp
from jax import lax
from jax.experimental import pallas as pl
from jax.experimental.pallas import tpu as pltpu

D_MODEL = 2048
BATCH = 4
SEQ = 2048
DEPTH = 2
DEC_BATCH = 128
DEC_SEQ = 4
PAST_LEN = 16384
D_RET = 1024
RET_HEADS = 4
RET_HEAD_DIM = 256
D_POOL = 1024
POOL_WINDOWS = (2, 4, 8, 16)
POOL_GROUP_DIM = 256
POOL_BUF = 15
D_IN = 5120
N_MEM = 256
MEM_HEADS = 4
MEM_HEAD_DIM = 512
D_FF = 8192
RET_CHUNK = 128
ROPE_BASE = 10000.0
EPS = 1e-6

F32 = jnp.float32
BF16 = jnp.bfloat16

P_ROWS = BATCH * SEQ
S_ROWS = DEC_BATCH * DEC_SEQ
M_ROWS = P_ROWS + S_ROWS
S_TILE_B = 8
S_TILE_ROWS = S_TILE_B * DEC_SEQ
N_S_TILES = DEC_BATCH // S_TILE_B
S_BLK0 = P_ROWS // S_TILE_ROWS

TM = 1088
NORM_CHUNK = 64
VMEM_LIMIT = 58 * 1024 * 1024


def _cparams(sem):
    return pltpu.CompilerParams(dimension_semantics=sem, vmem_limit_bytes=VMEM_LIMIT)


def _rms_rows(x, nw):
    ms = jnp.mean(x * x, axis=-1, keepdims=True)
    return x * lax.rsqrt(ms + EPS) * nw


def _for_row_chunks(n_rows, chunk, body):
    def step(c, carry):
        body(pl.ds(pl.multiple_of(c * chunk, chunk), chunk))
        return carry
    lax.fori_loop(0, n_rows // chunk, step, 0)


def _norm_matmul_kernel(x_ref, nw_ref, w_ref, o_ref, h_ref, *, tm):
    @pl.when(pl.program_id(1) == 0)
    def _():
        def body(rows):
            h_ref[rows, :] = _rms_rows(x_ref[rows, :], nw_ref[...]).astype(BF16)
        _for_row_chunks(tm, NORM_CHUNK, body)

    o_ref[...] = jnp.dot(h_ref[...], w_ref[...].astype(BF16),
                         preferred_element_type=F32).astype(o_ref.dtype)


def norm_matmul(x, nw, w, layer, *, tm, tn, out_dtype):
    m, k = x.shape
    n = w.shape[-1]
    return pl.pallas_call(
        functools.partial(_norm_matmul_kernel, tm=tm),
        out_shape=jax.ShapeDtypeStruct((m, n), out_dtype),
        grid=(m // tm, n // tn),
        in_specs=[
            pl.BlockSpec((tm, k), lambda i, j: (i, 0)),
            pl.BlockSpec((None, 1, k), lambda i, j: (layer, 0, 0)),
            pl.BlockSpec((None, k, tn), lambda i, j: (layer, 0, j)),
        ],
        out_specs=pl.BlockSpec((tm, tn), lambda i, j: (i, j)),
        scratch_shapes=[pltpu.VMEM((tm, k), BF16)],
        compiler_params=_cparams(("parallel", "arbitrary")),
        name="norm_matmul",
    )(x, nw, w)


def _matmul2_res_kernel(a1_ref, a2_ref, w_ref, r_ref, o_ref, *, k1):
    w = w_ref[...].astype(BF16)
    acc = jnp.dot(a1_ref[...], w[:k1], preferred_element_type=F32)
    acc = acc + jnp.dot(a2_ref[...], w[k1:], preferred_element_type=F32)
    o_ref[...] = r_ref[...] + acc


def matmul2_residual(a1, a2, w, res, layer, *, tm, tn):
    m, k1 = a1.shape
    k2 = a2.shape[1]
    n = w.shape[-1]
    return pl.pallas_call(
        functools.partial(_matmul2_res_kernel, k1=k1),
        out_shape=jax.ShapeDtypeStruct((m, n), F32),
        grid=(m // tm, n // tn),
        in_specs=[
            pl.BlockSpec((tm, k1), lambda i, j: (i, 0)),
            pl.BlockSpec((tm, k2), lambda i, j: (i, 0)),
            pl.BlockSpec((None, k1 + k2, tn), lambda i, j: (layer, 0, j)),
            pl.BlockSpec((tm, tn), lambda i, j: (i, j)),
        ],
        out_specs=pl.BlockSpec((tm, tn), lambda i, j: (i, j)),
        compiler_params=_cparams(("parallel", "arbitrary")),
        name="matmul2_residual",
    )(a1, a2, w, res)


def _matmul_res_kernel(a_ref, w_ref, r_ref, o_ref):
    o_ref[...] = r_ref[...] + jnp.dot(a_ref[...], w_ref[...].astype(BF16),
                                      preferred_element_type=F32)


def matmul_residual(a, w, res, layer, *, tm, tn):
    m, k = a.shape
    n = w.shape[-1]
    return pl.pallas_call(
        _matmul_res_kernel,
        out_shape=jax.ShapeDtypeStruct((m, n), F32),
        grid=(m // tm, n // tn),
        in_specs=[
            pl.BlockSpec((tm, k), lambda i, j: (i, 0)),
            pl.BlockSpec((None, k, tn), lambda i, j: (layer, 0, j)),
            pl.BlockSpec((tm, tn), lambda i, j: (i, j)),
        ],
        out_specs=pl.BlockSpec((tm, tn), lambda i, j: (i, j)),
        compiler_params=_cparams(("parallel", "arbitrary")),
        name="matmul_residual",
    )(a, w, res)


def _mlp_kernel(x_ref, nw_ref, wu_ref, wd_ref, o_ref, h_ref, *, tm):
    @pl.when(pl.program_id(1) == 0)
    def _():
        def body(rows):
            x = x_ref[rows, :]
            h_ref[rows, :] = _rms_rows(x, nw_ref[...]).astype(BF16)
            o_ref[rows, :] = x
        _for_row_chunks(tm, NORM_CHUNK, body)

    hid = jnp.dot(h_ref[...], wu_ref[...].astype(BF16), preferred_element_type=F32)
    hid = jnp.square(jnp.maximum(hid, 0.0)).astype(BF16)
    o_ref[...] += jnp.dot(hid, wd_ref[...].astype(BF16), preferred_element_type=F32)


def mlp_residual(x, nw, w_up, w_down, layer, *, tm, tf):
    m, k = x.shape
    ff = w_up.shape[-1]
    return pl.pallas_call(
        functools.partial(_mlp_kernel, tm=tm),
        out_shape=jax.ShapeDtypeStruct((m, k), F32),
        grid=(m // tm, ff // tf),
        in_specs=[
            pl.BlockSpec((tm, k), lambda i, f: (i, 0)),
            pl.BlockSpec((None, 1, k), lambda i, f: (layer, 0, 0)),
            pl.BlockSpec((None, k, tf), lambda i, f: (layer, 0, f)),
            pl.BlockSpec((None, tf, k), lambda i, f: (layer, f, 0)),
        ],
        out_specs=pl.BlockSpec((tm, k), lambda i, f: (i, 0)),
        scratch_shapes=[pltpu.VMEM((tm, k), BF16)],
        compiler_params=_cparams(("parallel", "arbitrary")),
        name="mlp_residual",
    )(x, nw, w_up, w_down)


def _rotate(x, cos, sin):
    half = x.shape[-1] // 2
    x1, x2 = x[:, :half], x[:, half:]
    return jnp.concatenate([x1 * cos - x2 * sin, x2 * cos + x1 * sin], axis=-1)


def _head_norm_gate(o, g, rnw):
    o = o * lax.rsqrt(jnp.mean(o * o, axis=-1, keepdims=True) + EPS)
    return (g * jax.nn.sigmoid(g)) * (o * rnw)


def _dot_t_lhs(a, b):
    return lax.dot_general(a, b, (((0,), (0,)), ((), ())), preferred_element_type=F32)


def _dot_t_rhs(a, b):
    return lax.dot_general(a, b, (((1,), (1,)), ((), ())), preferred_element_type=F32)


RET_P_ROWS = 512


def _ret_prompt_kernel(q_ref, k_ref, v_ref, g_ref, cos_ref, sin_ref, din_ref, dq_ref, dk_ref,
                       dc_ref, rnw_ref, o_ref, s_out_ref, s_ref):
    @pl.when(pl.program_id(2) == 0)
    def _():
        s_ref[...] = jnp.zeros_like(s_ref)

    din = din_ref[...]
    dq = jnp.concatenate([dq_ref[...], dq_ref[...]], axis=-1)
    dk = jnp.concatenate([dk_ref[...], dk_ref[...]], axis=-1)
    for c in range(RET_P_ROWS // RET_CHUNK):
        rows = slice(c * RET_CHUNK, (c + 1) * RET_CHUNK)
        cos, sin = cos_ref[rows, :], sin_ref[rows, :]
        q = _rotate(q_ref[rows, :], cos, sin)
        k = _rotate(k_ref[rows, :], cos, sin) * (RET_HEAD_DIM ** -0.5)
        vb = v_ref[rows, :].astype(BF16)
        s = s_ref[...]
        scores = _dot_t_rhs(q.astype(BF16), k.astype(BF16)) * din
        o = jnp.dot(scores.astype(BF16), vb, preferred_element_type=F32)
        o = o + jnp.dot((q * dq).astype(BF16), s.astype(BF16), preferred_element_type=F32)
        s_ref[...] = s * dc_ref[...] + _dot_t_lhs((k * dk).astype(BF16), vb)
        o_ref[rows, :] = _head_norm_gate(o, g_ref[rows, :], rnw_ref[...]).astype(o_ref.dtype)

    @pl.when(pl.program_id(2) == pl.num_programs(2) - 1)
    def _():
        s_out_ref[...] = s_ref[...]


def retention_prompt(z, cos, sin, din, dq, dk, dc, rnw, layer):
    nt = SEQ // RET_P_ROWS
    zspec = lambda off: pl.BlockSpec((RET_P_ROWS, RET_HEAD_DIM),
                                     lambda b, h, t: (b * nt + t, off + h))
    tab = lambda w: pl.BlockSpec((None, RET_CHUNK, w), lambda b, h, t: (h, 0, 0))
    return pl.pallas_call(
        _ret_prompt_kernel,
        out_shape=(jax.ShapeDtypeStruct((M_ROWS, D_RET), BF16),
                   jax.ShapeDtypeStruct((BATCH, RET_HEADS, RET_HEAD_DIM, RET_HEAD_DIM), F32)),
        grid=(BATCH, RET_HEADS, nt),
        in_specs=[
            zspec(0), zspec(RET_HEADS), zspec(2 * RET_HEADS), zspec(3 * RET_HEADS),
            pl.BlockSpec((RET_P_ROWS, RET_HEAD_DIM // 2), lambda b, h, t: (t, 0)),
            pl.BlockSpec((RET_P_ROWS, RET_HEAD_DIM // 2), lambda b, h, t: (t, 0)),
            tab(RET_CHUNK), tab(RET_HEAD_DIM // 2), tab(RET_HEAD_DIM // 2),
            pl.BlockSpec((None, 1, RET_HEAD_DIM), lambda b, h, t: (h, 0, 0)),
            pl.BlockSpec((None, 1, RET_HEAD_DIM), lambda b, h, t: (layer, 0, h)),
        ],
        out_specs=(
            pl.BlockSpec((RET_P_ROWS, RET_HEAD_DIM), lambda b, h, t: (b * nt + t, h)),
            pl.BlockSpec((None, None, RET_HEAD_DIM, RET_HEAD_DIM), lambda b, h, t: (b, h, 0, 0)),
        ),
        scratch_shapes=[pltpu.VMEM((RET_HEAD_DIM, RET_HEAD_DIM), F32)],
        compiler_params=_cparams(("parallel", "parallel", "arbitrary")),
        name="retention_prompt",
    )(z, z, z, z, cos, sin, din, dq, dk, dc, rnw)


def _ret_sample_kernel(dec_ref, z_ref, cos_ref, sin_ref, st_ref, rnw_ref, mix_in_ref,
                       o_ref, st_out_ref):
    del mix_in_ref
    h = pl.program_id(1)
    c0 = pl.multiple_of(h * RET_HEAD_DIM, RET_HEAD_DIM)
    col = lambda part: pl.ds(pl.multiple_of(part * D_RET + c0, RET_HEAD_DIM), RET_HEAD_DIM)
    cos, sin = cos_ref[...], sin_ref[...]
    q = _rotate(z_ref[:, col(0)], cos, sin)
    k = _rotate(z_ref[:, col(1)], cos, sin) * (RET_HEAD_DIM ** -0.5)
    v = z_ref[:, col(2)]
    g = z_ref[:, col(3)]
    slab = lambda a, t: a[t * S_TILE_B:(t + 1) * S_TILE_B, :]

    intra = []
    for t in range(DEC_SEQ):
        acc = None
        for j in range(t + 1):
            w = jnp.sum(slab(q, t) * slab(k, j), axis=-1, keepdims=True) * dec_ref[h, t * DEC_SEQ + j]
            acc = w * slab(v, j) if acc is None else acc + w * slab(v, j)
        intra.append(acc)
    o = jnp.concatenate(intra, axis=0)

    dq_rows = jnp.concatenate(
        [jnp.full((S_TILE_B, 1), 1.0, F32) * dec_ref[h, 16 + t] for t in range(DEC_SEQ)], axis=0)
    dk_rows = jnp.concatenate(
        [jnp.full((S_TILE_B, 1), 1.0, F32) * dec_ref[h, 20 + t] for t in range(DEC_SEQ)], axis=0)
    qd = (q * dq_rows).astype(BF16)
    kd = k * dk_rows
    vb = v.astype(BF16)
    dchunk = dec_ref[h, 24]
    row_b = lax.broadcasted_iota(jnp.int32, (S_TILE_ROWS, 1), 0) % S_TILE_B
    for b in range(S_TILE_B):
        mine = row_b == b
        s = st_ref[b]
        o = o + jnp.where(mine, jnp.dot(qd, s.astype(BF16), preferred_element_type=F32), 0.0)
        st_out_ref[b] = s * dchunk + _dot_t_lhs(jnp.where(mine, kd, 0.0).astype(BF16), vb)
    o_ref[...] = _head_norm_gate(o, g, rnw_ref[...]).astype(o_ref.dtype)


def retention_sample(dec, z, cos_s, sin_s, state_ret, rnw, mix, layer):
    return pl.pallas_call(
        _ret_sample_kernel,
        out_shape=(jax.ShapeDtypeStruct(mix.shape, mix.dtype),
                   jax.ShapeDtypeStruct((DEC_BATCH, RET_HEADS, RET_HEAD_DIM, RET_HEAD_DIM), F32)),
        grid=(N_S_TILES, RET_HEADS),
        in_specs=[
            pl.BlockSpec(memory_space=pltpu.SMEM),
            pl.BlockSpec((S_TILE_ROWS, D_IN), lambda i, h: (S_BLK0 + i, 0)),
            pl.BlockSpec((S_TILE_ROWS, RET_HEAD_DIM // 2), lambda i, h: (0, 0)),
            pl.BlockSpec((S_TILE_ROWS, RET_HEAD_DIM // 2), lambda i, h: (0, 0)),
            pl.BlockSpec((None, S_TILE_B, None, RET_HEAD_DIM, RET_HEAD_DIM),
                         lambda i, h: (layer, i, h, 0, 0)),
            pl.BlockSpec((None, 1, RET_HEAD_DIM), lambda i, h: (layer, 0, h)),
            pl.BlockSpec(memory_space=pl.ANY),
        ],
        out_specs=(
            pl.BlockSpec((S_TILE_ROWS, RET_HEAD_DIM), lambda i, h: (S_BLK0 + i, h)),
            pl.BlockSpec((S_TILE_B, None, RET_HEAD_DIM, RET_HEAD_DIM), lambda i, h: (i, h, 0, 0)),
        ),
        input_output_aliases={6: 0},
        compiler_params=_cparams(("parallel", "arbitrary")),
        name="retention_sample",
    )(dec, z, cos_s, sin_s, state_ret, rnw, mix)


POOL_P_ROWS = 256
POOL_HALO = 16


def _shift_rows(a, s):
    return pltpu.roll(a, s, axis=0)


def _pool_prompt_kernel(u_ref, pw_ref, ps_ref, o_ref, buf_ref, halo_ref):
    t = pl.program_id(1)

    @pl.when(t == 0)
    def _():
        halo_ref[...] = jnp.zeros_like(halo_ref)

    pos = t * POOL_P_ROWS + lax.broadcasted_iota(jnp.int32, (POOL_P_ROWS, 1), 0)
    for g, w in enumerate(POOL_WINDOWS):
        cols = slice(g * POOL_GROUP_DIM, (g + 1) * POOL_GROUP_DIM)
        u = u_ref[:, cols]
        a = jnp.concatenate([halo_ref[:, cols], u], axis=0)
        s = 1
        while s < w:
            a = a + _shift_rows(a, s)
            s *= 2
        cnt = jnp.minimum(pos + 1, w).astype(F32)
        pooled = a[POOL_HALO:, :] / cnt - u
        pm = jnp.dot(pooled.astype(BF16), pw_ref[g].astype(BF16), preferred_element_type=F32)
        o_ref[:, cols] = (pm * ps_ref[:, cols]).astype(o_ref.dtype)

    halo_ref[...] = u_ref[POOL_P_ROWS - POOL_HALO:, :]

    @pl.when(t == pl.num_programs(1) - 1)
    def _():
        buf_ref[...] = u_ref[POOL_P_ROWS - POOL_BUF:, :]


def pool_prompt(z, pool_w, pool_scale, layer):
    nt = SEQ // POOL_P_ROWS
    return pl.pallas_call(
        _pool_prompt_kernel,
        out_shape=(jax.ShapeDtypeStruct((M_ROWS, D_POOL), BF16),
                   jax.ShapeDtypeStruct((BATCH, POOL_BUF, D_POOL), F32)),
        grid=(BATCH, nt),
        in_specs=[
            pl.BlockSpec((POOL_P_ROWS, D_POOL), lambda b, t: (b * nt + t, 4)),
            pl.BlockSpec((None, 4, POOL_GROUP_DIM, POOL_GROUP_DIM), lambda b, t: (layer, 0, 0, 0)),
            pl.BlockSpec((None, 1, D_POOL), lambda b, t: (layer, 0, 0)),
        ],
        out_specs=(
            pl.BlockSpec((POOL_P_ROWS, D_POOL), lambda b, t: (b * nt + t, 0)),
            pl.BlockSpec((None, POOL_BUF, D_POOL), lambda b, t: (b, 0, 0)),
        ),
        scratch_shapes=[pltpu.VMEM((POOL_HALO, D_POOL), F32)],
        compiler_params=_cparams(("parallel", "arbitrary")),
        name="pool_prompt",
    )(z, pool_w, pool_scale)


def _pool_sample_kernel(u_ref, buf_ref, pw_ref, ps_ref, mix_in_ref, o_ref, nbuf_ref):
    del mix_in_ref

    def ext(r, cols):
        if r < POOL_BUF:
            return buf_ref[:, r, cols]
        return u_ref[(r - POOL_BUF) * S_TILE_B:(r - POOL_BUF + 1) * S_TILE_B, cols]

    for g, w in enumerate(POOL_WINDOWS):
        cols = slice(g * POOL_GROUP_DIM, (g + 1) * POOL_GROUP_DIM)
        pooled = []
        for t in range(DEC_SEQ):
            win = ext(POOL_BUF + t, cols)
            for r in range(POOL_BUF + t - w + 1, POOL_BUF + t):
                win = win + ext(r, cols)
            cnt = float(min(t + PAST_LEN + 1, w))
            pooled.append(win / cnt - ext(POOL_BUF + t, cols))
        pooled = jnp.concatenate(pooled, axis=0)
        pm = jnp.dot(pooled.astype(BF16), pw_ref[g].astype(BF16), preferred_element_type=F32)
        o_ref[:, cols] = (pm * ps_ref[:, cols]).astype(o_ref.dtype)

    full = slice(0, D_POOL)
    for r in range(POOL_BUF):
        nbuf_ref[:, r, :] = ext(r + DEC_SEQ, full)


def pool_sample(z, state_pool, pool_w, pool_scale, mix, layer):
    return pl.pallas_call(
        _pool_sample_kernel,
        out_shape=(jax.ShapeDtypeStruct(mix.shape, mix.dtype),
                   jax.ShapeDtypeStruct((DEC_BATCH, POOL_BUF, D_POOL), F32)),
        grid=(N_S_TILES,),
        in_specs=[
            pl.BlockSpec((S_TILE_ROWS, D_POOL), lambda i: (S_BLK0 + i, 4)),
            pl.BlockSpec((None, S_TILE_B, POOL_BUF, D_POOL), lambda i: (layer, i, 0, 0)),
            pl.BlockSpec((None, 4, POOL_GROUP_DIM, POOL_GROUP_DIM), lambda i: (layer, 0, 0, 0)),
            pl.BlockSpec((None, 1, D_POOL), lambda i: (layer, 0, 0)),
            pl.BlockSpec(memory_space=pl.ANY),
        ],
        out_specs=(
            pl.BlockSpec((S_TILE_ROWS, D_POOL), lambda i: (S_BLK0 + i, 0)),
            pl.BlockSpec((S_TILE_B, POOL_BUF, D_POOL), lambda i: (i, 0, 0)),
        ),
        input_output_aliases={4: 0},
        compiler_params=_cparams(("arbitrary",)),
        name="pool_sample",
    )(z, state_pool, pool_w, pool_scale, mix)


def _softmax_rows(s):
    m = jnp.max(s, axis=-1, keepdims=True)
    e = jnp.exp(s - m)
    return e / jnp.sum(e, axis=-1, keepdims=True)


XATTN_P_ROWS = 512


def _xattn_prompt_kernel(q_ref, mk_ref, mv_ref, o_ref):
    s = _dot_t_rhs(q_ref[...], mk_ref[...].astype(BF16)) * (MEM_HEAD_DIM ** -0.5)
    p = _softmax_rows(s)
    o_ref[...] = jnp.dot(p.astype(BF16), mv_ref[...].astype(BF16),
                         preferred_element_type=F32).astype(o_ref.dtype)


def xattn_prompt(qx, mk, mv):
    nt = SEQ // XATTN_P_ROWS
    return pl.pallas_call(
        _xattn_prompt_kernel,
        out_shape=jax.ShapeDtypeStruct((M_ROWS, D_MODEL), BF16),
        grid=(BATCH, MEM_HEADS, nt),
        in_specs=[
            pl.BlockSpec((XATTN_P_ROWS, MEM_HEAD_DIM), lambda b, h, t: (b * nt + t, h)),
            pl.BlockSpec((N_MEM, MEM_HEAD_DIM), lambda b, h, t: (b, h)),
            pl.BlockSpec((N_MEM, MEM_HEAD_DIM), lambda b, h, t: (b, h)),
        ],
        out_specs=pl.BlockSpec((XATTN_P_ROWS, MEM_HEAD_DIM), lambda b, h, t: (b * nt + t, h)),
        compiler_params=_cparams(("parallel", "parallel", "arbitrary")),
        name="xattn_prompt",
    )(qx, mk, mv)


XATTN_S_B = 2


def _xattn_sample_kernel(q_ref, k_ref, v_ref, att_in_ref, o_ref, acc_ref):
    del att_in_ref
    j = pl.program_id(1)

    @pl.when(j == 0)
    def _():
        acc_ref[...] = jnp.zeros_like(acc_ref)

    row_b = lax.broadcasted_iota(jnp.int32, (S_TILE_ROWS, 1), 0) % S_TILE_B
    for bb in range(XATTN_S_B):
        mine = row_b == j * XATTN_S_B + bb
        for h in range(MEM_HEADS):
            cols = slice(h * MEM_HEAD_DIM, (h + 1) * MEM_HEAD_DIM)
            kh = k_ref[bb, :, h, :].astype(BF16)
            vh = v_ref[bb, :, h, :].astype(BF16)
            s = _dot_t_rhs(q_ref[:, cols], kh) * (MEM_HEAD_DIM ** -0.5)
            p = _softmax_rows(s)
            oh = jnp.dot(p.astype(BF16), vh, preferred_element_type=F32)
            acc_ref[:, cols] += jnp.where(mine, oh, 0.0)

    @pl.when(j == pl.num_programs(1) - 1)
    def _():
        o_ref[...] = acc_ref[...].astype(o_ref.dtype)


def xattn_sample(qx, cache_k, cache_v, att, layer):
    nj = S_TILE_B // XATTN_S_B
    kv_spec = pl.BlockSpec((None, XATTN_S_B, N_MEM, MEM_HEADS, MEM_HEAD_DIM),
                           lambda i, j: (layer, i * nj + j, 0, 0, 0))
    return pl.pallas_call(
        _xattn_sample_kernel,
        out_shape=jax.ShapeDtypeStruct(att.shape, att.dtype),
        grid=(N_S_TILES, nj),
        in_specs=[
            pl.BlockSpec((S_TILE_ROWS, D_MODEL), lambda i, j: (S_BLK0 + i, 0)),
            kv_spec, kv_spec,
            pl.BlockSpec(memory_space=pl.ANY),
        ],
        out_specs=pl.BlockSpec((S_TILE_ROWS, D_MODEL), lambda i, j: (S_BLK0 + i, 0)),
        scratch_shapes=[pltpu.VMEM((S_TILE_ROWS, D_MODEL), F32)],
        input_output_aliases={3: 0},
        compiler_params=_cparams(("parallel", "arbitrary")),
        name="xattn_sample",
    )(qx, cache_k, cache_v, att)


FINAL_ROWS = 512


def _final_norm_kernel(x_ref, nw_ref, yp_ref, ys_ref):
    def run(dst_ref):
        def body(rows):
            dst_ref[rows, :] = _rms_rows(x_ref[rows, :], nw_ref[...])
        _for_row_chunks(FINAL_ROWS, NORM_CHUNK, body)

    is_prompt = pl.program_id(0) < P_ROWS // FINAL_ROWS
    pl.when(is_prompt)(lambda: run(yp_ref))
    pl.when(jnp.logical_not(is_prompt))(lambda: run(ys_ref))


def final_norm(x, nw):
    n_p = P_ROWS // FINAL_ROWS
    return pl.pallas_call(
        _final_norm_kernel,
        out_shape=(jax.ShapeDtypeStruct((P_ROWS, D_MODEL), F32),
                   jax.ShapeDtypeStruct((S_ROWS, D_MODEL), F32)),
        grid=(M_ROWS // FINAL_ROWS,),
        in_specs=[
            pl.BlockSpec((FINAL_ROWS, D_MODEL), lambda i: (i, 0)),
            pl.BlockSpec((1, D_MODEL), lambda i: (0, 0)),
        ],
        out_specs=(
            pl.BlockSpec((FINAL_ROWS, D_MODEL), lambda i: (jnp.minimum(i, n_p - 1), 0)),
            pl.BlockSpec((S_ROWS, D_MODEL), lambda i: (0, 0)),
        ),
        compiler_params=_cparams(("arbitrary",)),
        name="final_norm",
    )(x, nw)


def _rope_tables(pos):
    half = RET_HEAD_DIM // 2
    inv = ROPE_BASE ** (-jnp.arange(half, dtype=F32) / half)
    ang = pos.astype(F32)[:, None] * inv[None, :]
    return jnp.cos(ang), jnp.sin(ang)


def _log_gamma():
    return jnp.log1p(-jnp.exp2(-5.0 - jnp.arange(RET_HEADS, dtype=F32)))


def _decay_tables(chunk):
    lg = _log_gamma()
    idx = jnp.arange(chunk, dtype=F32)
    diff = idx[:, None] - idx[None, :]
    decay_in = jnp.where(diff[None] >= 0.0,
                         jnp.exp(lg[:, None, None] * jnp.maximum(diff, 0.0)[None]), 0.0)
    decay_q = jnp.exp(lg[:, None] * (idx[None, :] + 1.0))
    decay_k = jnp.exp(lg[:, None] * (chunk - 1.0 - idx[None, :]))
    decay_chunk = jnp.exp(lg * chunk)
    return decay_in, decay_q, decay_k, decay_chunk


def _to_sample_rows(a):
    d = a.shape[-1]
    return a.reshape(N_S_TILES, S_TILE_B, DEC_SEQ, d).transpose(0, 2, 1, 3).reshape(S_ROWS, d)


def _from_sample_rows(a):
    d = a.shape[-1]
    return a.reshape(N_S_TILES, DEC_SEQ, S_TILE_B, d).transpose(0, 2, 1, 3).reshape(DEC_BATCH, DEC_SEQ, d)


def kernel(x_prompt, x_sample, mem_prompt, state_ret, state_pool, cache_mem_k, cache_mem_v,
           attn_norm_w, w_in, ret_norm_w, pool_w, pool_scale, w_out, xattn_norm_w, mem_norm_w,
           w_xq, w_mk, w_mv, w_xo, mlp_norm_w, w_up, w_down, final_norm_w):
    cos_p, sin_p = _rope_tables(jnp.arange(SEQ))
    cos_s, sin_s = _rope_tables(jnp.arange(DEC_SEQ) + PAST_LEN)
    cos_s = jnp.repeat(cos_s, S_TILE_B, axis=0)
    sin_s = jnp.repeat(sin_s, S_TILE_B, axis=0)
    din_p, dq_p, dk_p, dc_p = _decay_tables(RET_CHUNK)
    half = RET_HEAD_DIM // 2
    dq_p = jnp.broadcast_to(dq_p[:, :, None], (RET_HEADS, RET_CHUNK, half))
    dk_p = jnp.broadcast_to(dk_p[:, :, None], (RET_HEADS, RET_CHUNK, half))
    dc_p = jnp.broadcast_to(dc_p[:, None, None], (RET_HEADS, 1, RET_HEAD_DIM))
    din_s, dq_s, dk_s, dc_s = _decay_tables(DEC_SEQ)
    dec_s = jnp.concatenate([din_s.reshape(RET_HEADS, DEC_SEQ * DEC_SEQ), dq_s, dk_s,
                             dc_s[:, None], jnp.zeros((RET_HEADS, 7), F32)], axis=1)

    row3 = lambda a: a.reshape(DEPTH, 1, a.shape[-1])
    attn_nw, xattn_nw, mem_nw, mlp_nw = map(row3, (attn_norm_w, xattn_norm_w, mem_norm_w, mlp_norm_w))
    ret_nw, pool_sc = row3(ret_norm_w), row3(pool_scale)

    x = jnp.concatenate([x_prompt.reshape(P_ROWS, D_MODEL), _to_sample_rows(x_sample)], axis=0)
    mem = mem_prompt.reshape(BATCH * N_MEM, D_MODEL)

    ret_p, buf_p, mk_p, mv_p, ret_s, buf_s = [], [], [], [], [], []
    for l in range(DEPTH):
        z = norm_matmul(x, attn_nw, w_in, l, tm=TM, tn=512, out_dtype=F32)
        mix_r, sp = retention_prompt(z, cos_p, sin_p, din_p, dq_p, dk_p, dc_p, ret_nw, l)
        mix_r, ss = retention_sample(dec_s, z, cos_s, sin_s, state_ret, ret_nw, mix_r, l)
        mix_p, bp = pool_prompt(z, pool_w, pool_sc, l)
        mix_p, bs = pool_sample(z, state_pool, pool_w, pool_sc, mix_p, l)
        x = matmul2_residual(mix_r, mix_p, w_out, x, l, tm=TM, tn=512)

        mk = norm_matmul(mem, mem_nw, w_mk, l, tm=1024, tn=512, out_dtype=F32)
        mv = norm_matmul(mem, mem_nw, w_mv, l, tm=1024, tn=512, out_dtype=F32)
        qx = norm_matmul(x, xattn_nw, w_xq, l, tm=TM, tn=512, out_dtype=BF16)
        att = xattn_prompt(qx, mk, mv)
        att = xattn_sample(qx, cache_mem_k, cache_mem_v, att, l)
        x = matmul_residual(att, w_xo, x, l, tm=TM, tn=512)

        x = mlp_residual(x, mlp_nw, w_up, w_down, l, tm=TM, tf=256)

        ret_p.append(sp)
        buf_p.append(bp)
        mk_p.append(mk.reshape(BATCH, N_MEM, MEM_HEADS, MEM_HEAD_DIM))
        mv_p.append(mv.reshape(BATCH, N_MEM, MEM_HEADS, MEM_HEAD_DIM))
        ret_s.append(ss)
        buf_s.append(bs)

    y_p, y_s = final_norm(x, final_norm_w.reshape(1, D_MODEL))
    return (y_p.reshape(BATCH, SEQ, D_MODEL), _from_sample_rows(y_s),
            jnp.stack(ret_p), jnp.stack(buf_p), jnp.stack(mk_p), jnp.stack(mv_p),
            jnp.stack(ret_s), jnp.stack(buf_s))
```

```python
import functools

import jax
import jax.numpy as jnp
from jax import lax
from jax.experimental import pallas as pl
from jax.experimental.pallas import tpu as pltpu

D_MODEL = 2048
BATCH = 4
SEQ = 2048
DEPTH = 2
DEC_BATCH = 128
DEC_SEQ = 4
PAST_LEN = 16384
D_RET = 1024
RET_HEADS = 4
RET_HEAD_DIM = 256
D_POOL = 1024
POOL_WINDOWS = (2, 4, 8, 16)
POOL_GROUP_DIM = 256
POOL_BUF = 15
D_IN = 5120
N_MEM = 256
MEM_HEADS = 4
MEM_HEAD_DIM = 512
D_FF = 8192
RET_CHUNK = 128
ROPE_BASE = 10000.0
EPS = 1e-6

F32 = jnp.float32
BF16 = jnp.bfloat16

P_ROWS = BATCH * SEQ
S_ROWS = DEC_BATCH * DEC_SEQ
M_ROWS = P_ROWS + S_ROWS
S_TILE_B = 8
S_TILE_ROWS = S_TILE_B * DEC_SEQ
N_S_TILES = DEC_BATCH // S_TILE_B
S_BLK0 = P_ROWS // S_TILE_ROWS

TM = 1088
NORM_CHUNK = 64
VMEM_LIMIT = 58 * 1024 * 1024


def _cparams(sem):
    return pltpu.CompilerParams(dimension_semantics=sem, vmem_limit_bytes=VMEM_LIMIT)


_ANY = pl.BlockSpec(memory_space=pl.ANY)


def _skip_refs(kernel_fn, start, count):
    def wrapped(*refs):
        return kernel_fn(*refs[:start], *refs[start + count:])
    return wrapped


def _carried(prev):
    return [] if prev is None else [prev]


def _rms_rows(x, nw):
    ms = jnp.mean(x * x, axis=-1, keepdims=True)
    return x * lax.rsqrt(ms + EPS) * nw


def _for_row_chunks(n_rows, chunk, body):
    def step(c, carry):
        body(pl.ds(pl.multiple_of(c * chunk, chunk), chunk))
        return carry
    lax.fori_loop(0, n_rows // chunk, step, 0)


def _norm_matmul_kernel(x_ref, nw_ref, w_ref, o_ref, h_ref, *, tm):
    @pl.when(pl.program_id(1) == 0)
    def _():
        def body(rows):
            h_ref[rows, :] = _rms_rows(x_ref[rows, :], nw_ref[...]).astype(BF16)
        _for_row_chunks(tm, NORM_CHUNK, body)

    o_ref[...] = jnp.dot(h_ref[...], w_ref[...].astype(BF16),
                         preferred_element_type=F32).astype(o_ref.dtype)


def norm_matmul(x, nw, w, layer, *, tm, tn, out_dtype):
    m, k = x.shape
    n = w.shape[-1]
    return pl.pallas_call(
        functools.partial(_norm_matmul_kernel, tm=tm),
        out_shape=jax.ShapeDtypeStruct((m, n), out_dtype),
        grid=(m // tm, n // tn),
        in_specs=[
            pl.BlockSpec((tm, k), lambda i, j: (i, 0)),
            pl.BlockSpec((None, 1, k), lambda i, j: (layer, 0, 0)),
            pl.BlockSpec((None, k, tn), lambda i, j: (layer, 0, j)),
        ],
        out_specs=pl.BlockSpec((tm, tn), lambda i, j: (i, j)),
        scratch_shapes=[pltpu.VMEM((tm, k), BF16)],
        compiler_params=_cparams(("parallel", "arbitrary")),
        name="norm_matmul",
    )(x, nw, w)


def norm_matmul_stacked(x, nw, w, layer, prev, *, tm, tn):
    m, k = x.shape
    n = w.shape[-1]
    extra = _carried(prev)
    return pl.pallas_call(
        _skip_refs(functools.partial(_norm_matmul_kernel, tm=tm), 3, len(extra)),
        out_shape=jax.ShapeDtypeStruct((DEPTH, m, n), F32),
        grid=(m // tm, n // tn),
        in_specs=[
            pl.BlockSpec((tm, k), lambda i, j: (i, 0)),
            pl.BlockSpec((None, 1, k), lambda i, j: (layer, 0, 0)),
            pl.BlockSpec((None, k, tn), lambda i, j: (layer, 0, j)),
        ] + [_ANY] * len(extra),
        out_specs=pl.BlockSpec((None, tm, tn), lambda i, j: (layer, i, j)),
        scratch_shapes=[pltpu.VMEM((tm, k), BF16)],
        input_output_aliases={3: 0} if extra else {},
        compiler_params=_cparams(("parallel", "arbitrary")),
        name="norm_matmul_stacked",
    )(x, nw, w, *extra)


def _matmul2_res_kernel(a1_ref, a2_ref, w_ref, r_ref, o_ref, *, k1):
    w = w_ref[...].astype(BF16)
    acc = jnp.dot(a1_ref[...], w[:k1], preferred_element_type=F32)
    acc = acc + jnp.dot(a2_ref[...], w[k1:], preferred_element_type=F32)
    o_ref[...] = r_ref[...] + acc


def matmul2_residual(a1, a2, w, res, layer, *, tm, tn):
    m, k1 = a1.shape
    k2 = a2.shape[1]
    n = w.shape[-1]
    return pl.pallas_call(
        functools.partial(_matmul2_res_kernel, k1=k1),
        out_shape=jax.ShapeDtypeStruct((m, n), F32),
        grid=(m // tm, n // tn),
        in_specs=[
            pl.BlockSpec((tm, k1), lambda i, j: (i, 0)),
            pl.BlockSpec((tm, k2), lambda i, j: (i, 0)),
            pl.BlockSpec((None, k1 + k2, tn), lambda i, j: (layer, 0, j)),
            pl.BlockSpec((tm, tn), lambda i, j: (i, j)),
        ],
        out_specs=pl.BlockSpec((tm, tn), lambda i, j: (i, j)),
        compiler_params=_cparams(("parallel", "arbitrary")),
        name="matmul2_residual",
    )(a1, a2, w, res)


def _matmul_res_kernel(a_ref, w_ref, r_ref, o_ref):
    o_ref[...] = r_ref[...] + jnp.dot(a_ref[...], w_ref[...].astype(BF16),
                                      preferred_element_type=F32)


def matmul_residual(a, w, res, layer, *, tm, tn):
    m, k = a.shape
    n = w.shape[-1]
    return pl.pallas_call(
        _matmul_res_kernel,
        out_shape=jax.ShapeDtypeStruct((m, n), F32),
        grid=(m // tm, n // tn),
        in_specs=[
            pl.BlockSpec((tm, k), lambda i, j: (i, 0)),
            pl.BlockSpec((None, k, tn), lambda i, j: (layer, 0, j)),
            pl.BlockSpec((tm, tn), lambda i, j: (i, j)),
        ],
        out_specs=pl.BlockSpec((tm, tn), lambda i, j: (i, j)),
        compiler_params=_cparams(("parallel", "arbitrary")),
        name="matmul_residual",
    )(a, w, res)


MLP_TF = 512
MLP_TN = 512
MLP_KC = 2048
MLP_UP_STEPS = D_FF // MLP_TF
MLP_K_STEPS = D_FF // MLP_KC
MLP_DOWN_STEPS = (D_MODEL // MLP_TN) * MLP_K_STEPS


def _mlp_kernel(x_ref, nw_ref, wu_ref, wd_ref, o_ref, h_ref, hid_ref, *, tm):
    s = pl.program_id(1)

    @pl.when(s == 0)
    def _():
        def body(rows):
            h_ref[rows, :] = _rms_rows(x_ref[rows, :], nw_ref[...]).astype(BF16)
        _for_row_chunks(tm, NORM_CHUNK, body)

    @pl.when(s < MLP_UP_STEPS)
    def _():
        hid = jnp.dot(h_ref[...], wu_ref[...], preferred_element_type=F32)
        per_chunk = MLP_KC // MLP_TF
        c0 = pl.multiple_of((s % per_chunk) * MLP_TF, MLP_TF)
        hid_ref[s // per_chunk, :, pl.ds(c0, MLP_TF)] = jnp.square(jnp.maximum(hid, 0.0)).astype(BF16)

    @pl.when(s >= MLP_UP_STEPS)
    def _():
        d = s - MLP_UP_STEPS
        kc = d % MLP_K_STEPS
        part = jnp.dot(hid_ref[kc], wd_ref[...], preferred_element_type=F32)

        @pl.when(kc == 0)
        def _():
            n0 = pl.multiple_of((d // MLP_K_STEPS) * MLP_TN, MLP_TN)
            o_ref[...] = x_ref[:, pl.ds(n0, MLP_TN)] + part

        @pl.when(kc != 0)
        def _():
            o_ref[...] += part


def mlp_residual(x, nw, w_up, w_down, layer, *, tm):
    m, k = x.shape
    down = lambda s: jnp.maximum(s - MLP_UP_STEPS, 0)
    return pl.pallas_call(
        functools.partial(_mlp_kernel, tm=tm),
        out_shape=jax.ShapeDtypeStruct((m, k), F32),
        grid=(m // tm, MLP_UP_STEPS + MLP_DOWN_STEPS),
        in_specs=[
            pl.BlockSpec((tm, k), lambda i, s: (i, 0), pipeline_mode=pl.Buffered(1)),
            pl.BlockSpec((None, 1, k), lambda i, s: (layer, 0, 0)),
            pl.BlockSpec((None, k, MLP_TF), lambda i, s: (layer, 0, jnp.minimum(s, MLP_UP_STEPS - 1))),
            pl.BlockSpec((None, MLP_KC, MLP_TN),
                         lambda i, s: (layer, down(s) % MLP_K_STEPS, down(s) // MLP_K_STEPS)),
        ],
        out_specs=pl.BlockSpec((tm, MLP_TN), lambda i, s: (i, down(s) // MLP_K_STEPS)),
        scratch_shapes=[pltpu.VMEM((tm, k), BF16),
                        pltpu.VMEM((MLP_K_STEPS, tm, MLP_KC), BF16)],
        compiler_params=_cparams(("parallel", "arbitrary")),
        name="mlp_residual",
    )(x, nw, w_up, w_down)


def _rotate(x, cos, sin):
    half = x.shape[-1] // 2
    x1, x2 = x[:, :half], x[:, half:]
    return jnp.concatenate([x1 * cos - x2 * sin, x2 * cos + x1 * sin], axis=-1)


def _head_norm_gate(o, g, rnw):
    o = o * lax.rsqrt(jnp.mean(o * o, axis=-1, keepdims=True) + EPS)
    return (g * jax.nn.sigmoid(g)) * (o * rnw)


def _dot_t_lhs(a, b):
    return lax.dot_general(a, b, (((0,), (0,)), ((), ())), preferred_element_type=F32)


def _dot_t_rhs(a, b):
    return lax.dot_general(a, b, (((1,), (1,)), ((), ())), preferred_element_type=F32)


RET_P_ROWS = 512


def _ret_prompt_kernel(q_ref, k_ref, v_ref, g_ref, cos_ref, sin_ref, din_ref, dq_ref, dk_ref,
                       dc_ref, rnw_ref, o_ref, s_out_ref, s_ref):
    @pl.when(pl.program_id(2) == 0)
    def _():
        s_ref[...] = jnp.zeros_like(s_ref)

    din = din_ref[...]
    dq = jnp.concatenate([dq_ref[...], dq_ref[...]], axis=-1)
    dk = jnp.concatenate([dk_ref[...], dk_ref[...]], axis=-1)
    for c in range(RET_P_ROWS // RET_CHUNK):
        rows = slice(c * RET_CHUNK, (c + 1) * RET_CHUNK)
        cos, sin = cos_ref[rows, :], sin_ref[rows, :]
        q = _rotate(q_ref[rows, :], cos, sin)
        k = _rotate(k_ref[rows, :], cos, sin) * (RET_HEAD_DIM ** -0.5)
        vb = v_ref[rows, :].astype(BF16)
        s = s_ref[...]
        scores = _dot_t_rhs(q.astype(BF16), k.astype(BF16)) * din
        o = jnp.dot(scores.astype(BF16), vb, preferred_element_type=F32)
        o = o + jnp.dot((q * dq).astype(BF16), s.astype(BF16), preferred_element_type=F32)
        s_ref[...] = s * dc_ref[...] + _dot_t_lhs((k * dk).astype(BF16), vb)
        o_ref[rows, :] = _head_norm_gate(o, g_ref[rows, :], rnw_ref[...]).astype(o_ref.dtype)

    @pl.when(pl.program_id(2) == pl.num_programs(2) - 1)
    def _():
        s_out_ref[...] = s_ref[...]


def retention_prompt(z, cos, sin, din, dq, dk, dc, rnw, layer, state_prev):
    nt = SEQ // RET_P_ROWS
    extra = _carried(state_prev)
    zspec = lambda off: pl.BlockSpec((RET_P_ROWS, RET_HEAD_DIM),
                                     lambda b, h, t: (b * nt + t, off + h))
    tab = lambda w: pl.BlockSpec((None, RET_CHUNK, w), lambda b, h, t: (h, 0, 0))
    return pl.pallas_call(
        _skip_refs(_ret_prompt_kernel, 11, len(extra)),
        out_shape=(jax.ShapeDtypeStruct((M_ROWS, D_RET), BF16),
                   jax.ShapeDtypeStruct((DEPTH, BATCH, RET_HEADS, RET_HEAD_DIM, RET_HEAD_DIM), F32)),
        grid=(BATCH, RET_HEADS, nt),
        in_specs=[
            zspec(0), zspec(RET_HEADS), zspec(2 * RET_HEADS), zspec(3 * RET_HEADS),
            pl.BlockSpec((RET_P_ROWS, RET_HEAD_DIM // 2), lambda b, h, t: (t, 0)),
            pl.BlockSpec((RET_P_ROWS, RET_HEAD_DIM // 2), lambda b, h, t: (t, 0)),
            tab(RET_CHUNK), tab(RET_HEAD_DIM // 2), tab(RET_HEAD_DIM // 2),
            pl.BlockSpec((None, 1, RET_HEAD_DIM), lambda b, h, t: (h, 0, 0)),
            pl.BlockSpec((None, 1, RET_HEAD_DIM), lambda b, h, t: (layer, 0, h)),
        ] + [_ANY] * len(extra),
        out_specs=(
            pl.BlockSpec((RET_P_ROWS, RET_HEAD_DIM), lambda b, h, t: (b * nt + t, h)),
            pl.BlockSpec((None, None, None, RET_HEAD_DIM, RET_HEAD_DIM),
                         lambda b, h, t: (layer, b, h, 0, 0)),
        ),
        scratch_shapes=[pltpu.VMEM((RET_HEAD_DIM, RET_HEAD_DIM), F32)],
        input_output_aliases={11: 1} if extra else {},
        compiler_params=_cparams(("parallel", "parallel", "arbitrary")),
        name="retention_prompt",
    )(z, z, z, z, cos, sin, din, dq, dk, dc, rnw, *extra)


def _ret_sample_kernel(dec_ref, z_ref, cos_ref, sin_ref, st_ref, rnw_ref, o_ref, st_out_ref):
    h = pl.program_id(1)
    c0 = pl.multiple_of(h * RET_HEAD_DIM, RET_HEAD_DIM)
    col = lambda part: pl.ds(pl.multiple_of(part * D_RET + c0, RET_HEAD_DIM), RET_HEAD_DIM)
    cos, sin = cos_ref[...], sin_ref[...]
    q = _rotate(z_ref[:, col(0)], cos, sin)
    k = _rotate(z_ref[:, col(1)], cos, sin) * (RET_HEAD_DIM ** -0.5)
    v = z_ref[:, col(2)]
    g = z_ref[:, col(3)]
    slab = lambda a, t: a[t * S_TILE_B:(t + 1) * S_TILE_B, :]

    intra = []
    for t in range(DEC_SEQ):
        acc = None
        for j in range(t + 1):
            w = jnp.sum(slab(q, t) * slab(k, j), axis=-1, keepdims=True) * dec_ref[h, t * DEC_SEQ + j]
            acc = w * slab(v, j) if acc is None else acc + w * slab(v, j)
        intra.append(acc)
    o = jnp.concatenate(intra, axis=0)

    dq_rows = jnp.concatenate(
        [jnp.full((S_TILE_B, 1), 1.0, F32) * dec_ref[h, 16 + t] for t in range(DEC_SEQ)], axis=0)
    dk_rows = jnp.concatenate(
        [jnp.full((S_TILE_B, 1), 1.0, F32) * dec_ref[h, 20 + t] for t in range(DEC_SEQ)], axis=0)
    qd = (q * dq_rows).astype(BF16)
    kd = k * dk_rows
    vb = v.astype(BF16)
    dchunk = dec_ref[h, 24]
    row_b = lax.broadcasted_iota(jnp.int32, (S_TILE_ROWS, 1), 0) % S_TILE_B
    for b in range(S_TILE_B):
        mine = row_b == b
        s = st_ref[b]
        o = o + jnp.where(mine, jnp.dot(qd, s.astype(BF16), preferred_element_type=F32), 0.0)
        st_out_ref[b] = s * dchunk + _dot_t_lhs(jnp.where(mine, kd, 0.0).astype(BF16), vb)
    o_ref[...] = _head_norm_gate(o, g, rnw_ref[...]).astype(o_ref.dtype)


def retention_sample(dec, z, cos_s, sin_s, state_ret, rnw, mix, layer, state_prev):
    extra = [mix] + _carried(state_prev)
    return pl.pallas_call(
        _skip_refs(_ret_sample_kernel, 6, len(extra)),
        out_shape=(jax.ShapeDtypeStruct(mix.shape, mix.dtype),
                   jax.ShapeDtypeStruct(state_ret.shape, F32)),
        grid=(N_S_TILES, RET_HEADS),
        in_specs=[
            pl.BlockSpec(memory_space=pltpu.SMEM),
            pl.BlockSpec((S_TILE_ROWS, D_IN), lambda i, h: (S_BLK0 + i, 0)),
            pl.BlockSpec((S_TILE_ROWS, RET_HEAD_DIM // 2), lambda i, h: (0, 0)),
            pl.BlockSpec((S_TILE_ROWS, RET_HEAD_DIM // 2), lambda i, h: (0, 0)),
            pl.BlockSpec((None, S_TILE_B, None, RET_HEAD_DIM, RET_HEAD_DIM),
                         lambda i, h: (layer, i, h, 0, 0)),
            pl.BlockSpec((None, 1, RET_HEAD_DIM), lambda i, h: (layer, 0, h)),
        ] + [_ANY] * len(extra),
        out_specs=(
            pl.BlockSpec((S_TILE_ROWS, RET_HEAD_DIM), lambda i, h: (S_BLK0 + i, h)),
            pl.BlockSpec((None, S_TILE_B, None, RET_HEAD_DIM, RET_HEAD_DIM),
                         lambda i, h: (layer, i, h, 0, 0)),
        ),
        input_output_aliases={6 + n: n for n in range(len(extra))},
        compiler_params=_cparams(("parallel", "arbitrary")),
        name="retention_sample",
    )(dec, z, cos_s, sin_s, state_ret, rnw, *extra)


POOL_P_ROWS = 256
POOL_HALO = 16


def _shift_rows(a, s):
    return pltpu.roll(a, s, axis=0)


def _pool_prompt_kernel(u_ref, pw_ref, ps_ref, o_ref, buf_ref, halo_ref):
    t = pl.program_id(1)

    @pl.when(t == 0)
    def _():
        halo_ref[...] = jnp.zeros_like(halo_ref)

    pos = t * POOL_P_ROWS + lax.broadcasted_iota(jnp.int32, (POOL_P_ROWS, 1), 0)
    for g, w in enumerate(POOL_WINDOWS):
        cols = slice(g * POOL_GROUP_DIM, (g + 1) * POOL_GROUP_DIM)
        u = u_ref[:, cols]
        a = jnp.concatenate([halo_ref[:, cols], u], axis=0)
        s = 1
        while s < w:
            a = a + _shift_rows(a, s)
            s *= 2
        cnt = jnp.minimum(pos + 1, w).astype(F32)
        pooled = a[POOL_HALO:, :] / cnt - u
        pm = jnp.dot(pooled.astype(BF16), pw_ref[g].astype(BF16), preferred_element_type=F32)
        o_ref[:, cols] = (pm * ps_ref[:, cols]).astype(o_ref.dtype)

    halo_ref[...] = u_ref[POOL_P_ROWS - POOL_HALO:, :]

    @pl.when(t == pl.num_programs(1) - 1)
    def _():
        buf_ref[...] = u_ref[POOL_P_ROWS - POOL_BUF:, :]


def pool_prompt(z, pool_w, pool_scale, layer, buf_prev):
    nt = SEQ // POOL_P_ROWS
    extra = _carried(buf_prev)
    return pl.pallas_call(
        _skip_refs(_pool_prompt_kernel, 3, len(extra)),
        out_shape=(jax.ShapeDtypeStruct((M_ROWS, D_POOL), BF16),
                   jax.ShapeDtypeStruct((DEPTH, BATCH, POOL_BUF, D_POOL), F32)),
        grid=(BATCH, nt),
        in_specs=[
            pl.BlockSpec((POOL_P_ROWS, D_POOL), lambda b, t: (b * nt + t, 4)),
            pl.BlockSpec((None, 4, POOL_GROUP_DIM, POOL_GROUP_DIM), lambda b, t: (layer, 0, 0, 0)),
            pl.BlockSpec((None, 1, D_POOL), lambda b, t: (layer, 0, 0)),
        ] + [_ANY] * len(extra),
        out_specs=(
            pl.BlockSpec((POOL_P_ROWS, D_POOL), lambda b, t: (b * nt + t, 0)),
            pl.BlockSpec((None, None, POOL_BUF, D_POOL), lambda b, t: (layer, b, 0, 0)),
        ),
        scratch_shapes=[pltpu.VMEM((POOL_HALO, D_POOL), F32)],
        input_output_aliases={3: 1} if extra else {},
        compiler_params=_cparams(("parallel", "arbitrary")),
        name="pool_prompt",
    )(z, pool_w, pool_scale, *extra)


def _pool_sample_kernel(u_ref, buf_ref, pw_ref, ps_ref, o_ref, nbuf_ref):

    def ext(r, cols):
        if r < POOL_BUF:
            return buf_ref[:, r, cols]
        return u_ref[(r - POOL_BUF) * S_TILE_B:(r - POOL_BUF + 1) * S_TILE_B, cols]

    for g, w in enumerate(POOL_WINDOWS):
        cols = slice(g * POOL_GROUP_DIM, (g + 1) * POOL_GROUP_DIM)
        pooled = []
        for t in range(DEC_SEQ):
            win = ext(POOL_BUF + t, cols)
            for r in range(POOL_BUF + t - w + 1, POOL_BUF + t):
                win = win + ext(r, cols)
            cnt = float(min(t + PAST_LEN + 1, w))
            pooled.append(win / cnt - ext(POOL_BUF + t, cols))
        pooled = jnp.concatenate(pooled, axis=0)
        pm = jnp.dot(pooled.astype(BF16), pw_ref[g].astype(BF16), preferred_element_type=F32)
        o_ref[:, cols] = (pm * ps_ref[:, cols]).astype(o_ref.dtype)

    full = slice(0, D_POOL)
    for r in range(POOL_BUF):
        nbuf_ref[:, r, :] = ext(r + DEC_SEQ, full)


def pool_sample(z, state_pool, pool_w, pool_scale, mix, layer, buf_prev):
    extra = [mix] + _carried(buf_prev)
    return pl.pallas_call(
        _skip_refs(_pool_sample_kernel, 4, len(extra)),
        out_shape=(jax.ShapeDtypeStruct(mix.shape, mix.dtype),
                   jax.ShapeDtypeStruct(state_pool.shape, F32)),
        grid=(N_S_TILES,),
        in_specs=[
            pl.BlockSpec((S_TILE_ROWS, D_POOL), lambda i: (S_BLK0 + i, 4)),
            pl.BlockSpec((None, S_TILE_B, POOL_BUF, D_POOL), lambda i: (layer, i, 0, 0)),
            pl.BlockSpec((None, 4, POOL_GROUP_DIM, POOL_GROUP_DIM), lambda i: (layer, 0, 0, 0)),
            pl.BlockSpec((None, 1, D_POOL), lambda i: (layer, 0, 0)),
        ] + [_ANY] * len(extra),
        out_specs=(
            pl.BlockSpec((S_TILE_ROWS, D_POOL), lambda i: (S_BLK0 + i, 0)),
            pl.BlockSpec((None, S_TILE_B, POOL_BUF, D_POOL), lambda i: (layer, i, 0, 0)),
        ),
        input_output_aliases={4 + n: n for n in range(len(extra))},
        compiler_params=_cparams(("arbitrary",)),
        name="pool_sample",
    )(z, state_pool, pool_w, pool_scale, *extra)


def _softmax_rows(s):
    m = jnp.max(s, axis=-1, keepdims=True)
    e = jnp.exp(s - m)
    return e / jnp.sum(e, axis=-1, keepdims=True)


XATTN_P_ROWS = 512


def _xattn_prompt_kernel(q_ref, mk_ref, mv_ref, o_ref):
    s = _dot_t_rhs(q_ref[...], mk_ref[...].astype(BF16)) * (MEM_HEAD_DIM ** -0.5)
    p = _softmax_rows(s)
    o_ref[...] = jnp.dot(p.astype(BF16), mv_ref[...].astype(BF16),
                         preferred_element_type=F32).astype(o_ref.dtype)


def xattn_prompt(qx, mk, mv, layer):
    nt = SEQ // XATTN_P_ROWS
    return pl.pallas_call(
        _xattn_prompt_kernel,
        out_shape=jax.ShapeDtypeStruct((M_ROWS, D_MODEL), BF16),
        grid=(BATCH, MEM_HEADS, nt),
        in_specs=[
            pl.BlockSpec((XATTN_P_ROWS, MEM_HEAD_DIM), lambda b, h, t: (b * nt + t, h)),
            pl.BlockSpec((None, N_MEM, MEM_HEAD_DIM), lambda b, h, t: (layer, b, h)),
            pl.BlockSpec((None, N_MEM, MEM_HEAD_DIM), lambda b, h, t: (layer, b, h)),
        ],
        out_specs=pl.BlockSpec((XATTN_P_ROWS, MEM_HEAD_DIM), lambda b, h, t: (b * nt + t, h)),
        compiler_params=_cparams(("parallel", "parallel", "arbitrary")),
        name="xattn_prompt",
    )(qx, mk, mv)


XATTN_S_B = 4
XATTN_S_ROWS = MEM_HEADS * DEC_SEQ
XATTN_S_KEYS = N_MEM * MEM_HEADS


def _xattn_sample_kernel(q_ref, k_ref, v_ref, o_ref):
    row_h = lax.broadcasted_iota(jnp.int32, (XATTN_S_ROWS, XATTN_S_KEYS), 0) // DEC_SEQ
    col_h = lax.broadcasted_iota(jnp.int32, (XATTN_S_ROWS, XATTN_S_KEYS), 1) % MEM_HEADS
    same_head = row_h == col_h
    for bb in range(XATTN_S_B):
        k2 = k_ref[bb].reshape(XATTN_S_KEYS, MEM_HEAD_DIM).astype(BF16)
        v2 = v_ref[bb].reshape(XATTN_S_KEYS, MEM_HEAD_DIM).astype(BF16)
        s = _dot_t_rhs(q_ref[bb], k2) * (MEM_HEAD_DIM ** -0.5)
        s = jnp.where(same_head, s, -jnp.inf)
        p = _softmax_rows(s)
        o_ref[bb] = jnp.dot(p.astype(BF16), v2, preferred_element_type=F32).astype(o_ref.dtype)


def xattn_sample(qs, cache_k, cache_v, layer):
    kv_spec = pl.BlockSpec((None, XATTN_S_B, N_MEM, MEM_HEADS, MEM_HEAD_DIM),
                           lambda i: (layer, i, 0, 0, 0))
    qo_spec = pl.BlockSpec((XATTN_S_B, XATTN_S_ROWS, MEM_HEAD_DIM), lambda i: (i, 0, 0))
    return pl.pallas_call(
        _xattn_sample_kernel,
        out_shape=jax.ShapeDtypeStruct(qs.shape, BF16),
        grid=(DEC_BATCH // XATTN_S_B,),
        in_specs=[qo_spec, kv_spec, kv_spec],
        out_specs=qo_spec,
        compiler_params=_cparams(("parallel",)),
        name="xattn_sample",
    )(qs, cache_k, cache_v)


def _sample_rows_to_heads(a):
    a = a.reshape(N_S_TILES, DEC_SEQ, S_TILE_B, MEM_HEADS, MEM_HEAD_DIM)
    return a.transpose(0, 2, 3, 1, 4).reshape(DEC_BATCH, XATTN_S_ROWS, MEM_HEAD_DIM)


def _heads_to_sample_rows(a):
    a = a.reshape(N_S_TILES, S_TILE_B, MEM_HEADS, DEC_SEQ, MEM_HEAD_DIM)
    return a.transpose(0, 3, 1, 2, 4).reshape(S_ROWS, D_MODEL)


FINAL_ROWS = 512


def _final_norm_kernel(x_ref, nw_ref, yp_ref, ys_ref):
    def run(dst_ref):
        def body(rows):
            dst_ref[rows, :] = _rms_rows(x_ref[rows, :], nw_ref[...])
        _for_row_chunks(FINAL_ROWS, NORM_CHUNK, body)

    is_prompt = pl.program_id(0) < P_ROWS // FINAL_ROWS
    pl.when(is_prompt)(lambda: run(yp_ref))
    pl.when(jnp.logical_not(is_prompt))(lambda: run(ys_ref))


def final_norm(x, nw):
    n_p = P_ROWS // FINAL_ROWS
    return pl.pallas_call(
        _final_norm_kernel,
        out_shape=(jax.ShapeDtypeStruct((P_ROWS, D_MODEL), F32),
                   jax.ShapeDtypeStruct((S_ROWS, D_MODEL), F32)),
        grid=(M_ROWS // FINAL_ROWS,),
        in_specs=[
            pl.BlockSpec((FINAL_ROWS, D_MODEL), lambda i: (i, 0)),
            pl.BlockSpec((1, D_MODEL), lambda i: (0, 0)),
        ],
        out_specs=(
            pl.BlockSpec((FINAL_ROWS, D_MODEL), lambda i: (jnp.minimum(i, n_p - 1), 0)),
            pl.BlockSpec((S_ROWS, D_MODEL), lambda i: (0, 0)),
        ),
        compiler_params=_cparams(("arbitrary",)),
        name="final_norm",
    )(x, nw)


def _rope_tables(pos):
    half = RET_HEAD_DIM // 2
    inv = ROPE_BASE ** (-jnp.arange(half, dtype=F32) / half)
    ang = pos.astype(F32)[:, None] * inv[None, :]
    return jnp.cos(ang), jnp.sin(ang)


def _log_gamma():
    return jnp.log1p(-jnp.exp2(-5.0 - jnp.arange(RET_HEADS, dtype=F32)))


def _decay_tables(chunk):
    lg = _log_gamma()
    idx = jnp.arange(chunk, dtype=F32)
    diff = idx[:, None] - idx[None, :]
    decay_in = jnp.where(diff[None] >= 0.0,
                         jnp.exp(lg[:, None, None] * jnp.maximum(diff, 0.0)[None]), 0.0)
    decay_q = jnp.exp(lg[:, None] * (idx[None, :] + 1.0))
    decay_k = jnp.exp(lg[:, None] * (chunk - 1.0 - idx[None, :]))
    decay_chunk = jnp.exp(lg * chunk)
    return decay_in, decay_q, decay_k, decay_chunk


def _to_sample_rows(a):
    d = a.shape[-1]
    return a.reshape(N_S_TILES, S_TILE_B, DEC_SEQ, d).transpose(0, 2, 1, 3).reshape(S_ROWS, d)


def _from_sample_rows(a):
    d = a.shape[-1]
    return a.reshape(N_S_TILES, DEC_SEQ, S_TILE_B, d).transpose(0, 2, 1, 3).reshape(DEC_BATCH, DEC_SEQ, d)


def kernel(x_prompt, x_sample, mem_prompt, state_ret, state_pool, cache_mem_k, cache_mem_v,
           attn_norm_w, w_in, ret_norm_w, pool_w, pool_scale, w_out, xattn_norm_w, mem_norm_w,
           w_xq, w_mk, w_mv, w_xo, mlp_norm_w, w_up, w_down, final_norm_w):
    cos_p, sin_p = _rope_tables(jnp.arange(SEQ))
    cos_s, sin_s = _rope_tables(jnp.arange(DEC_SEQ) + PAST_LEN)
    cos_s = jnp.repeat(cos_s, S_TILE_B, axis=0)
    sin_s = jnp.repeat(sin_s, S_TILE_B, axis=0)
    din_p, dq_p, dk_p, dc_p = _decay_tables(RET_CHUNK)
    half = RET_HEAD_DIM // 2
    dq_p = jnp.broadcast_to(dq_p[:, :, None], (RET_HEADS, RET_CHUNK, half))
    dk_p = jnp.broadcast_to(dk_p[:, :, None], (RET_HEADS, RET_CHUNK, half))
    dc_p = jnp.broadcast_to(dc_p[:, None, None], (RET_HEADS, 1, RET_HEAD_DIM))
    din_s, dq_s, dk_s, dc_s = _decay_tables(DEC_SEQ)
    dec_s = jnp.concatenate([din_s.reshape(RET_HEADS, DEC_SEQ * DEC_SEQ), dq_s, dk_s,
                             dc_s[:, None], jnp.zeros((RET_HEADS, 7), F32)], axis=1)

    row3 = lambda a: a.reshape(DEPTH, 1, a.shape[-1])
    attn_nw, xattn_nw, mem_nw, mlp_nw = map(row3, (attn_norm_w, xattn_norm_w, mem_norm_w, mlp_norm_w))
    ret_nw, pool_sc = row3(ret_norm_w), row3(pool_scale)
    w_up_b, w_down_b = w_up.astype(BF16), w_down.astype(BF16)

    x = jnp.concatenate([x_prompt.reshape(P_ROWS, D_MODEL), _to_sample_rows(x_sample)], axis=0)
    mem = mem_prompt.reshape(BATCH * N_MEM, D_MODEL)

    ret_p = buf_p = mk_p = mv_p = ret_s = buf_s = None
    for l in range(DEPTH):
        z = norm_matmul(x, attn_nw, w_in, l, tm=TM, tn=512, out_dtype=F32)
        mix_r, ret_p = retention_prompt(z, cos_p, sin_p, din_p, dq_p, dk_p, dc_p, ret_nw, l, ret_p)
        mix_r, ret_s = retention_sample(dec_s, z, cos_s, sin_s, state_ret, ret_nw, mix_r, l, ret_s)
        mix_p, buf_p = pool_prompt(z, pool_w, pool_sc, l, buf_p)
        mix_p, buf_s = pool_sample(z, state_pool, pool_w, pool_sc, mix_p, l, buf_s)
        x = matmul2_residual(mix_r, mix_p, w_out, x, l, tm=TM, tn=512)

        mk_p = norm_matmul_stacked(mem, mem_nw, w_mk, l, mk_p, tm=1024, tn=512)
        mv_p = norm_matmul_stacked(mem, mem_nw, w_mv, l, mv_p, tm=1024, tn=512)
        qx = norm_matmul(x, xattn_nw, w_xq, l, tm=TM, tn=512, out_dtype=BF16)
        att = xattn_prompt(qx, mk_p, mv_p, l)
        att_s = xattn_sample(_sample_rows_to_heads(qx[P_ROWS:]), cache_mem_k, cache_mem_v, l)
        att = lax.dynamic_update_slice(att, _heads_to_sample_rows(att_s), (P_ROWS, 0))
        x = matmul_residual(att, w_xo, x, l, tm=TM, tn=512)

        x = mlp_residual(x, mlp_nw, w_up_b, w_down_b, l, tm=TM)

    y_p, y_s = final_norm(x, final_norm_w.reshape(1, D_MODEL))
    mem_shape = (DEPTH, BATCH, N_MEM, MEM_HEADS, MEM_HEAD_DIM)
    return (y_p.reshape(BATCH, SEQ, D_MODEL), _from_sample_rows(y_s),
            ret_p, buf_p, mk_p.reshape(mem_shape), mv_p.reshape(mem_shape), ret_s, buf_s)
```

```python
import functools

import jax
import jax.numpy as jnp
from jax import lax
from jax.experimental import pallas as pl
from jax.experimental.pallas import tpu as pltpu

D_MODEL = 2048
BATCH = 4
SEQ = 2048
DEPTH = 2
DEC_BATCH = 128
DEC_SEQ = 4
PAST_LEN = 16384
D_RET = 1024
RET_HEADS = 4
RET_HEAD_DIM = 256
D_POOL = 1024
POOL_WINDOWS = (2, 4, 8, 16)
POOL_GROUP_DIM = 256
POOL_BUF = 15
D_IN = 5120
N_MEM = 256
MEM_HEADS = 4
MEM_HEAD_DIM = 512
D_FF = 8192
RET_CHUNK = 128
ROPE_BASE = 10000.0
EPS = 1e-6

F32 = jnp.float32
BF16 = jnp.bfloat16

P_ROWS = BATCH * SEQ
S_ROWS = DEC_BATCH * DEC_SEQ
M_ROWS = P_ROWS + S_ROWS
S_TILE_B = 8
S_TILE_ROWS = S_TILE_B * DEC_SEQ
N_S_TILES = DEC_BATCH // S_TILE_B
S_BLK0 = P_ROWS // S_TILE_ROWS

TM = 1088
NORM_CHUNK = 64
VMEM_LIMIT = 58 * 1024 * 1024


def _cparams(sem):
    return pltpu.CompilerParams(dimension_semantics=sem, vmem_limit_bytes=VMEM_LIMIT)


_ANY = pl.BlockSpec(memory_space=pl.ANY)


def _skip_refs(kernel_fn, start, count):
    def wrapped(*refs):
        return kernel_fn(*refs[:start], *refs[start + count:])
    return wrapped


def _carried(prev):
    return [] if prev is None else [prev]


def _rms_rows(x, nw):
    ms = jnp.mean(x * x, axis=-1, keepdims=True)
    return x * lax.rsqrt(ms + EPS) * nw


def _for_row_chunks(n_rows, chunk, body):
    def step(c, carry):
        body(pl.ds(pl.multiple_of(c * chunk, chunk), chunk))
        return carry
    lax.fori_loop(0, n_rows // chunk, step, 0)


def _norm_matmul_kernel(x_ref, nw_ref, w_ref, o_ref, h_ref, *, tm):
    @pl.when(pl.program_id(1) == 0)
    def _():
        def body(rows):
            h_ref[rows, :] = _rms_rows(x_ref[rows, :], nw_ref[...]).astype(BF16)
        _for_row_chunks(tm, NORM_CHUNK, body)

    o_ref[...] = jnp.dot(h_ref[...], w_ref[...], preferred_element_type=F32).astype(o_ref.dtype)


def norm_matmul(x, nw, w, layer, *, tm, tn, out_dtype):
    m, k = x.shape
    n = w.shape[-1]
    return pl.pallas_call(
        functools.partial(_norm_matmul_kernel, tm=tm),
        out_shape=jax.ShapeDtypeStruct((m, n), out_dtype),
        grid=(m // tm, n // tn),
        in_specs=[
            pl.BlockSpec((tm, k), lambda i, j: (i, 0)),
            pl.BlockSpec((None, 1, k), lambda i, j: (layer, 0, 0)),
            pl.BlockSpec((None, k, tn), lambda i, j: (layer, 0, j)),
        ],
        out_specs=pl.BlockSpec((tm, tn), lambda i, j: (i, j)),
        scratch_shapes=[pltpu.VMEM((tm, k), BF16)],
        compiler_params=_cparams(("parallel", "arbitrary")),
        name="norm_matmul",
    )(x, nw, w)


def norm_matmul_stacked(x, nw, w, layer, prev, *, tm, tn):
    m, k = x.shape
    n = w.shape[-1]
    extra = _carried(prev)
    return pl.pallas_call(
        _skip_refs(functools.partial(_norm_matmul_kernel, tm=tm), 3, len(extra)),
        out_shape=jax.ShapeDtypeStruct((DEPTH, m, n), F32),
        grid=(m // tm, n // tn),
        in_specs=[
            pl.BlockSpec((tm, k), lambda i, j: (i, 0)),
            pl.BlockSpec((None, 1, k), lambda i, j: (layer, 0, 0)),
            pl.BlockSpec((None, k, tn), lambda i, j: (layer, 0, j)),
        ] + [_ANY] * len(extra),
        out_specs=pl.BlockSpec((None, tm, tn), lambda i, j: (layer, i, j)),
        scratch_shapes=[pltpu.VMEM((tm, k), BF16)],
        input_output_aliases={3: 0} if extra else {},
        compiler_params=_cparams(("parallel", "arbitrary")),
        name="norm_matmul_stacked",
    )(x, nw, w, *extra)


def _matmul2_res_kernel(a1_ref, a2_ref, w_ref, r_ref, o_ref, *, k1):
    acc = jnp.dot(a1_ref[...], w_ref[:k1, :], preferred_element_type=F32)
    acc = acc + jnp.dot(a2_ref[...], w_ref[k1:, :], preferred_element_type=F32)
    o_ref[...] = r_ref[...] + acc


def matmul2_residual(a1, a2, w, res, layer, *, tm, tn):
    m, k1 = a1.shape
    k2 = a2.shape[1]
    n = w.shape[-1]
    return pl.pallas_call(
        functools.partial(_matmul2_res_kernel, k1=k1),
        out_shape=jax.ShapeDtypeStruct((m, n), F32),
        grid=(m // tm, n // tn),
        in_specs=[
            pl.BlockSpec((tm, k1), lambda i, j: (i, 0)),
            pl.BlockSpec((tm, k2), lambda i, j: (i, 0)),
            pl.BlockSpec((None, k1 + k2, tn), lambda i, j: (layer, 0, j)),
            pl.BlockSpec((tm, tn), lambda i, j: (i, j)),
        ],
        out_specs=pl.BlockSpec((tm, tn), lambda i, j: (i, j)),
        compiler_params=_cparams(("parallel", "arbitrary")),
        name="matmul2_residual",
    )(a1, a2, w, res)


def _matmul_res_kernel(a_ref, w_ref, r_ref, o_ref):
    o_ref[...] = r_ref[...] + jnp.dot(a_ref[...], w_ref[...], preferred_element_type=F32)


def matmul_residual(a, w, res, layer, *, tm, tn):
    m, k = a.shape
    n = w.shape[-1]
    return pl.pallas_call(
        _matmul_res_kernel,
        out_shape=jax.ShapeDtypeStruct((m, n), F32),
        grid=(m // tm, n // tn),
        in_specs=[
            pl.BlockSpec((tm, k), lambda i, j: (i, 0)),
            pl.BlockSpec((None, k, tn), lambda i, j: (layer, 0, j)),
            pl.BlockSpec((tm, tn), lambda i, j: (i, j)),
        ],
        out_specs=pl.BlockSpec((tm, tn), lambda i, j: (i, j)),
        compiler_params=_cparams(("parallel", "arbitrary")),
        name="matmul_residual",
    )(a, w, res)


MLP_TF = 512
MLP_TN = 512
MLP_KC = 4096
MLP_UP_STEPS = D_FF // MLP_TF
MLP_K_STEPS = D_FF // MLP_KC
MLP_DOWN_STEPS = (D_MODEL // MLP_TN) * MLP_K_STEPS


def _mlp_kernel(x_ref, nw_ref, wu_ref, wd_ref, o_ref, h_ref, hid_ref, *, tm):
    s = pl.program_id(1)

    @pl.when(s == 0)
    def _():
        def body(rows):
            h_ref[rows, :] = _rms_rows(x_ref[rows, :], nw_ref[...]).astype(BF16)
        _for_row_chunks(tm, NORM_CHUNK, body)

    @pl.when(s < MLP_UP_STEPS)
    def _():
        hid = jnp.dot(h_ref[...], wu_ref[...], preferred_element_type=F32)
        per_chunk = MLP_KC // MLP_TF
        c0 = pl.multiple_of((s % per_chunk) * MLP_TF, MLP_TF)
        hid_ref[s // per_chunk, :, pl.ds(c0, MLP_TF)] = jnp.square(jnp.maximum(hid, 0.0)).astype(BF16)

    @pl.when(s >= MLP_UP_STEPS)
    def _():
        d = s - MLP_UP_STEPS
        kc = d % MLP_K_STEPS
        part = jnp.dot(hid_ref[kc], wd_ref[...], preferred_element_type=F32)

        @pl.when(kc == 0)
        def _():
            n0 = pl.multiple_of((d // MLP_K_STEPS) * MLP_TN, MLP_TN)
            o_ref[...] = x_ref[:, pl.ds(n0, MLP_TN)] + part

        @pl.when(kc != 0)
        def _():
            o_ref[...] += part


def mlp_residual(x, nw, w_up, w_down, layer, *, tm):
    m, k = x.shape
    down = lambda s: jnp.maximum(s - MLP_UP_STEPS, 0)
    return pl.pallas_call(
        functools.partial(_mlp_kernel, tm=tm),
        out_shape=jax.ShapeDtypeStruct((m, k), F32),
        grid=(m // tm, MLP_UP_STEPS + MLP_DOWN_STEPS),
        in_specs=[
            pl.BlockSpec((tm, k), lambda i, s: (i, 0), pipeline_mode=pl.Buffered(1)),
            pl.BlockSpec((None, 1, k), lambda i, s: (layer, 0, 0)),
            pl.BlockSpec((None, k, MLP_TF), lambda i, s: (layer, 0, jnp.minimum(s, MLP_UP_STEPS - 1))),
            pl.BlockSpec((None, MLP_KC, MLP_TN),
                         lambda i, s: (layer, down(s) % MLP_K_STEPS, down(s) // MLP_K_STEPS)),
        ],
        out_specs=pl.BlockSpec((tm, MLP_TN), lambda i, s: (i, down(s) // MLP_K_STEPS)),
        scratch_shapes=[pltpu.VMEM((tm, k), BF16),
                        pltpu.VMEM((MLP_K_STEPS, tm, MLP_KC), BF16)],
        compiler_params=_cparams(("parallel", "arbitrary")),
        name="mlp_residual",
    )(x, nw, w_up, w_down)


def _rotate(x, cos, sin):
    half = x.shape[-1] // 2
    x1, x2 = x[:, :half], x[:, half:]
    return jnp.concatenate([x1 * cos - x2 * sin, x2 * cos + x1 * sin], axis=-1)


def _head_norm_gate(o, g, rnw):
    o = o * lax.rsqrt(jnp.mean(o * o, axis=-1, keepdims=True) + EPS)
    return (g * jax.nn.sigmoid(g)) * (o * rnw)


def _dot_t_lhs(a, b):
    return lax.dot_general(a, b, (((0,), (0,)), ((), ())), preferred_element_type=F32)


def _dot_t_rhs(a, b):
    return lax.dot_general(a, b, (((1,), (1,)), ((), ())), preferred_element_type=F32)


RET_P_ROWS = 512


def _ret_prompt_kernel(q_ref, k_ref, v_ref, g_ref, cos_ref, sin_ref, din_ref, dq_ref, dk_ref,
                       dc_ref, rnw_ref, o_ref, s_out_ref, s_ref):
    @pl.when(pl.program_id(2) == 0)
    def _():
        s_ref[...] = jnp.zeros_like(s_ref)

    din = din_ref[...]
    dq = jnp.concatenate([dq_ref[...], dq_ref[...]], axis=-1)
    dk = jnp.concatenate([dk_ref[...], dk_ref[...]], axis=-1)
    for c in range(RET_P_ROWS // RET_CHUNK):
        rows = slice(c * RET_CHUNK, (c + 1) * RET_CHUNK)
        cos, sin = cos_ref[rows, :], sin_ref[rows, :]
        q = _rotate(q_ref[rows, :], cos, sin)
        k = _rotate(k_ref[rows, :], cos, sin) * (RET_HEAD_DIM ** -0.5)
        vb = v_ref[rows, :].astype(BF16)
        s = s_ref[...]
        scores = _dot_t_rhs(q.astype(BF16), k.astype(BF16)) * din
        o = jnp.dot(scores.astype(BF16), vb, preferred_element_type=F32)
        o = o + jnp.dot((q * dq).astype(BF16), s.astype(BF16), preferred_element_type=F32)
        s_ref[...] = s * dc_ref[...] + _dot_t_lhs((k * dk).astype(BF16), vb)
        o_ref[rows, :] = _head_norm_gate(o, g_ref[rows, :], rnw_ref[...]).astype(o_ref.dtype)

    @pl.when(pl.program_id(2) == pl.num_programs(2) - 1)
    def _():
        s_out_ref[...] = s_ref[...]


def retention_prompt(z, cos, sin, din, dq, dk, dc, rnw, layer, state_prev):
    nt = SEQ // RET_P_ROWS
    extra = _carried(state_prev)
    zspec = lambda off: pl.BlockSpec((RET_P_ROWS, RET_HEAD_DIM),
                                     lambda b, h, t: (b * nt + t, off + h))
    tab = lambda w: pl.BlockSpec((None, RET_CHUNK, w), lambda b, h, t: (h, 0, 0))
    return pl.pallas_call(
        _skip_refs(_ret_prompt_kernel, 11, len(extra)),
        out_shape=(jax.ShapeDtypeStruct((M_ROWS, D_RET), BF16),
                   jax.ShapeDtypeStruct((DEPTH, BATCH, RET_HEADS, RET_HEAD_DIM, RET_HEAD_DIM), F32)),
        grid=(BATCH, RET_HEADS, nt),
        in_specs=[
            zspec(0), zspec(RET_HEADS), zspec(2 * RET_HEADS), zspec(3 * RET_HEADS),
            pl.BlockSpec((RET_P_ROWS, RET_HEAD_DIM // 2), lambda b, h, t: (t, 0)),
            pl.BlockSpec((RET_P_ROWS, RET_HEAD_DIM // 2), lambda b, h, t: (t, 0)),
            tab(RET_CHUNK), tab(RET_HEAD_DIM // 2), tab(RET_HEAD_DIM // 2),
            pl.BlockSpec((None, 1, RET_HEAD_DIM), lambda b, h, t: (h, 0, 0)),
            pl.BlockSpec((None, 1, RET_HEAD_DIM), lambda b, h, t: (layer, 0, h)),
        ] + [_ANY] * len(extra),
        out_specs=(
            pl.BlockSpec((RET_P_ROWS, RET_HEAD_DIM), lambda b, h, t: (b * nt + t, h)),
            pl.BlockSpec((None, None, None, RET_HEAD_DIM, RET_HEAD_DIM),
                         lambda b, h, t: (layer, b, h, 0, 0)),
        ),
        scratch_shapes=[pltpu.VMEM((RET_HEAD_DIM, RET_HEAD_DIM), F32)],
        input_output_aliases={11: 1} if extra else {},
        compiler_params=_cparams(("parallel", "parallel", "arbitrary")),
        name="retention_prompt",
    )(z, z, z, z, cos, sin, din, dq, dk, dc, rnw, *extra)


def _ret_sample_kernel(dec_ref, z_ref, cos_ref, sin_ref, st_ref, rnw_ref, o_ref, st_out_ref):
    h = pl.program_id(1)
    c0 = pl.multiple_of(h * RET_HEAD_DIM, RET_HEAD_DIM)
    col = lambda part: pl.ds(pl.multiple_of(part * D_RET + c0, RET_HEAD_DIM), RET_HEAD_DIM)
    cos, sin = cos_ref[...], sin_ref[...]
    q = _rotate(z_ref[:, col(0)], cos, sin)
    k = _rotate(z_ref[:, col(1)], cos, sin) * (RET_HEAD_DIM ** -0.5)
    v = z_ref[:, col(2)]
    g = z_ref[:, col(3)]
    slab = lambda a, t: a[t * S_TILE_B:(t + 1) * S_TILE_B, :]

    intra = []
    for t in range(DEC_SEQ):
        acc = None
        for j in range(t + 1):
            w = jnp.sum(slab(q, t) * slab(k, j), axis=-1, keepdims=True) * dec_ref[h, t * DEC_SEQ + j]
            acc = w * slab(v, j) if acc is None else acc + w * slab(v, j)
        intra.append(acc)
    o = jnp.concatenate(intra, axis=0)

    dq_rows = jnp.concatenate(
        [jnp.full((S_TILE_B, 1), 1.0, F32) * dec_ref[h, 16 + t] for t in range(DEC_SEQ)], axis=0)
    dk_rows = jnp.concatenate(
        [jnp.full((S_TILE_B, 1), 1.0, F32) * dec_ref[h, 20 + t] for t in range(DEC_SEQ)], axis=0)
    qd = (q * dq_rows).astype(BF16)
    kd = k * dk_rows
    vb = v.astype(BF16)
    dchunk = dec_ref[h, 24]
    row_b = lax.broadcasted_iota(jnp.int32, (S_TILE_ROWS, 1), 0) % S_TILE_B
    for b in range(S_TILE_B):
        mine = row_b == b
        s = st_ref[b]
        o = o + jnp.where(mine, jnp.dot(qd, s.astype(BF16), preferred_element_type=F32), 0.0)
        st_out_ref[b] = s * dchunk + _dot_t_lhs(jnp.where(mine, kd, 0.0).astype(BF16), vb)
    o_ref[...] = _head_norm_gate(o, g, rnw_ref[...]).astype(o_ref.dtype)


def retention_sample(dec, z, cos_s, sin_s, state_ret, rnw, mix, layer, state_prev):
    extra = [mix] + _carried(state_prev)
    return pl.pallas_call(
        _skip_refs(_ret_sample_kernel, 6, len(extra)),
        out_shape=(jax.ShapeDtypeStruct(mix.shape, mix.dtype),
                   jax.ShapeDtypeStruct(state_ret.shape, F32)),
        grid=(N_S_TILES, RET_HEADS),
        in_specs=[
            pl.BlockSpec(memory_space=pltpu.SMEM),
            pl.BlockSpec((S_TILE_ROWS, D_IN), lambda i, h: (S_BLK0 + i, 0)),
            pl.BlockSpec((S_TILE_ROWS, RET_HEAD_DIM // 2), lambda i, h: (0, 0)),
            pl.BlockSpec((S_TILE_ROWS, RET_HEAD_DIM // 2), lambda i, h: (0, 0)),
            pl.BlockSpec((None, S_TILE_B, None, RET_HEAD_DIM, RET_HEAD_DIM),
                         lambda i, h: (layer, i, h, 0, 0)),
            pl.BlockSpec((None, 1, RET_HEAD_DIM), lambda i, h: (layer, 0, h)),
        ] + [_ANY] * len(extra),
        out_specs=(
            pl.BlockSpec((S_TILE_ROWS, RET_HEAD_DIM), lambda i, h: (S_BLK0 + i, h)),
            pl.BlockSpec((None, S_TILE_B, None, RET_HEAD_DIM, RET_HEAD_DIM),
                         lambda i, h: (layer, i, h, 0, 0)),
        ),
        input_output_aliases={6 + n: n for n in range(len(extra))},
        compiler_params=_cparams(("parallel", "arbitrary")),
        name="retention_sample",
    )(dec, z, cos_s, sin_s, state_ret, rnw, *extra)


POOL_P_ROWS = 256
POOL_HALO = 16


def _shift_rows(a, s):
    return pltpu.roll(a, s, axis=0)


def _pool_prompt_kernel(u_ref, pw_ref, ps_ref, o_ref, buf_ref, halo_ref):
    t = pl.program_id(1)

    @pl.when(t == 0)
    def _():
        halo_ref[...] = jnp.zeros_like(halo_ref)

    pos = t * POOL_P_ROWS + lax.broadcasted_iota(jnp.int32, (POOL_P_ROWS, 1), 0)
    for g, w in enumerate(POOL_WINDOWS):
        cols = slice(g * POOL_GROUP_DIM, (g + 1) * POOL_GROUP_DIM)
        u = u_ref[:, cols]
        a = jnp.concatenate([halo_ref[:, cols], u], axis=0)
        s = 1
        while s < w:
            a = a + _shift_rows(a, s)
            s *= 2
        cnt = jnp.minimum(pos + 1, w).astype(F32)
        pooled = a[POOL_HALO:, :] / cnt - u
        pm = jnp.dot(pooled.astype(BF16), pw_ref[g].astype(BF16), preferred_element_type=F32)
        o_ref[:, cols] = (pm * ps_ref[:, cols]).astype(o_ref.dtype)

    halo_ref[...] = u_ref[POOL_P_ROWS - POOL_HALO:, :]

    @pl.when(t == pl.num_programs(1) - 1)
    def _():
        buf_ref[...] = u_ref[POOL_P_ROWS - POOL_BUF:, :]


def pool_prompt(z, pool_w, pool_scale, layer, buf_prev):
    nt = SEQ // POOL_P_ROWS
    extra = _carried(buf_prev)
    return pl.pallas_call(
        _skip_refs(_pool_prompt_kernel, 3, len(extra)),
        out_shape=(jax.ShapeDtypeStruct((M_ROWS, D_POOL), BF16),
                   jax.ShapeDtypeStruct((DEPTH, BATCH, POOL_BUF, D_POOL), F32)),
        grid=(BATCH, nt),
        in_specs=[
            pl.BlockSpec((POOL_P_ROWS, D_POOL), lambda b, t: (b * nt + t, 4)),
            pl.BlockSpec((None, 4, POOL_GROUP_DIM, POOL_GROUP_DIM), lambda b, t: (layer, 0, 0, 0)),
            pl.BlockSpec((None, 1, D_POOL), lambda b, t: (layer, 0, 0)),
        ] + [_ANY] * len(extra),
        out_specs=(
            pl.BlockSpec((POOL_P_ROWS, D_POOL), lambda b, t: (b * nt + t, 0)),
            pl.BlockSpec((None, None, POOL_BUF, D_POOL), lambda b, t: (layer, b, 0, 0)),
        ),
        scratch_shapes=[pltpu.VMEM((POOL_HALO, D_POOL), F32)],
        input_output_aliases={3: 1} if extra else {},
        compiler_params=_cparams(("parallel", "arbitrary")),
        name="pool_prompt",
    )(z, pool_w, pool_scale, *extra)


def _pool_sample_kernel(u_ref, buf_ref, pw_ref, ps_ref, o_ref, nbuf_ref):

    def ext(r, cols):
        if r < POOL_BUF:
            return buf_ref[:, r, cols]
        return u_ref[(r - POOL_BUF) * S_TILE_B:(r - POOL_BUF + 1) * S_TILE_B, cols]

    for g, w in enumerate(POOL_WINDOWS):
        cols = slice(g * POOL_GROUP_DIM, (g + 1) * POOL_GROUP_DIM)
        pooled = []
        for t in range(DEC_SEQ):
            win = ext(POOL_BUF + t, cols)
            for r in range(POOL_BUF + t - w + 1, POOL_BUF + t):
                win = win + ext(r, cols)
            cnt = float(min(t + PAST_LEN + 1, w))
            pooled.append(win / cnt - ext(POOL_BUF + t, cols))
        pooled = jnp.concatenate(pooled, axis=0)
        pm = jnp.dot(pooled.astype(BF16), pw_ref[g].astype(BF16), preferred_element_type=F32)
        o_ref[:, cols] = (pm * ps_ref[:, cols]).astype(o_ref.dtype)

    full = slice(0, D_POOL)
    for r in range(POOL_BUF):
        nbuf_ref[:, r, :] = ext(r + DEC_SEQ, full)


def pool_sample(z, state_pool, pool_w, pool_scale, mix, layer, buf_prev):
    extra = [mix] + _carried(buf_prev)
    return pl.pallas_call(
        _skip_refs(_pool_sample_kernel, 4, len(extra)),
        out_shape=(jax.ShapeDtypeStruct(mix.shape, mix.dtype),
                   jax.ShapeDtypeStruct(state_pool.shape, F32)),
        grid=(N_S_TILES,),
        in_specs=[
            pl.BlockSpec((S_TILE_ROWS, D_POOL), lambda i: (S_BLK0 + i, 4)),
            pl.BlockSpec((None, S_TILE_B, POOL_BUF, D_POOL), lambda i: (layer, i, 0, 0)),
            pl.BlockSpec((None, 4, POOL_GROUP_DIM, POOL_GROUP_DIM), lambda i: (layer, 0, 0, 0)),
            pl.BlockSpec((None, 1, D_POOL), lambda i: (layer, 0, 0)),
        ] + [_ANY] * len(extra),
        out_specs=(
            pl.BlockSpec((S_TILE_ROWS, D_POOL), lambda i: (S_BLK0 + i, 0)),
            pl.BlockSpec((None, S_TILE_B, POOL_BUF, D_POOL), lambda i: (layer, i, 0, 0)),
        ),
        input_output_aliases={4 + n: n for n in range(len(extra))},
        compiler_params=_cparams(("arbitrary",)),
        name="pool_sample",
    )(z, state_pool, pool_w, pool_scale, *extra)


def _softmax_rows(s):
    m = jnp.max(s, axis=-1, keepdims=True)
    e = jnp.exp(s - m)
    return e / jnp.sum(e, axis=-1, keepdims=True)


XATTN_P_ROWS = 512


def _xattn_prompt_kernel(q_ref, mk_ref, mv_ref, o_ref):
    for h in range(MEM_HEADS):
        cols = slice(h * MEM_HEAD_DIM, (h + 1) * MEM_HEAD_DIM)
        s = _dot_t_rhs(q_ref[:, cols], mk_ref[:, cols].astype(BF16)) * (MEM_HEAD_DIM ** -0.5)
        p = _softmax_rows(s)
        o_ref[:, cols] = jnp.dot(p.astype(BF16), mv_ref[:, cols].astype(BF16),
                                 preferred_element_type=F32).astype(o_ref.dtype)


def xattn_prompt(qx, mk, mv, layer):
    nt = SEQ // XATTN_P_ROWS
    return pl.pallas_call(
        _xattn_prompt_kernel,
        out_shape=jax.ShapeDtypeStruct((M_ROWS, D_MODEL), BF16),
        grid=(BATCH, nt),
        in_specs=[
            pl.BlockSpec((XATTN_P_ROWS, D_MODEL), lambda b, t: (b * nt + t, 0)),
            pl.BlockSpec((None, N_MEM, D_MODEL), lambda b, t: (layer, b, 0)),
            pl.BlockSpec((None, N_MEM, D_MODEL), lambda b, t: (layer, b, 0)),
        ],
        out_specs=pl.BlockSpec((XATTN_P_ROWS, D_MODEL), lambda b, t: (b * nt + t, 0)),
        compiler_params=_cparams(("parallel", "arbitrary")),
        name="xattn_prompt",
    )(qx, mk, mv)


XATTN_S_B = 4
XATTN_S_ROWS = MEM_HEADS * DEC_SEQ
XATTN_S_KEYS = N_MEM * MEM_HEADS


def _xattn_sample_kernel(q_ref, k_ref, v_ref, o_ref):
    row_h = lax.broadcasted_iota(jnp.int32, (XATTN_S_ROWS, XATTN_S_KEYS), 0) // DEC_SEQ
    col_h = lax.broadcasted_iota(jnp.int32, (XATTN_S_ROWS, XATTN_S_KEYS), 1) % MEM_HEADS
    same_head = row_h == col_h
    for bb in range(XATTN_S_B):
        k2 = k_ref[bb].reshape(XATTN_S_KEYS, MEM_HEAD_DIM).astype(BF16)
        v2 = v_ref[bb].reshape(XATTN_S_KEYS, MEM_HEAD_DIM).astype(BF16)
        s = _dot_t_rhs(q_ref[bb], k2) * (MEM_HEAD_DIM ** -0.5)
        s = jnp.where(same_head, s, -jnp.inf)
        p = _softmax_rows(s)
        o_ref[bb] = jnp.dot(p.astype(BF16), v2, preferred_element_type=F32).astype(o_ref.dtype)


def xattn_sample(qs, cache_k, cache_v, layer):
    kv_spec = pl.BlockSpec((None, XATTN_S_B, N_MEM, MEM_HEADS, MEM_HEAD_DIM),
                           lambda i: (layer, i, 0, 0, 0))
    qo_spec = pl.BlockSpec((XATTN_S_B, XATTN_S_ROWS, MEM_HEAD_DIM), lambda i: (i, 0, 0))
    return pl.pallas_call(
        _xattn_sample_kernel,
        out_shape=jax.ShapeDtypeStruct(qs.shape, BF16),
        grid=(DEC_BATCH // XATTN_S_B,),
        in_specs=[qo_spec, kv_spec, kv_spec],
        out_specs=qo_spec,
        compiler_params=_cparams(("parallel",)),
        name="xattn_sample",
    )(qs, cache_k, cache_v)


def _sample_rows_to_heads(a):
    a = a.reshape(N_S_TILES, DEC_SEQ, S_TILE_B, MEM_HEADS, MEM_HEAD_DIM)
    return a.transpose(0, 2, 3, 1, 4).reshape(DEC_BATCH, XATTN_S_ROWS, MEM_HEAD_DIM)


def _heads_to_sample_rows(a):
    a = a.reshape(N_S_TILES, S_TILE_B, MEM_HEADS, DEC_SEQ, MEM_HEAD_DIM)
    return a.transpose(0, 3, 1, 2, 4).reshape(S_ROWS, D_MODEL)


FINAL_ROWS = 512


def _final_norm_kernel(x_ref, nw_ref, yp_ref, ys_ref):
    def run(dst_ref):
        def body(rows):
            dst_ref[rows, :] = _rms_rows(x_ref[rows, :], nw_ref[...])
        _for_row_chunks(FINAL_ROWS, NORM_CHUNK, body)

    is_prompt = pl.program_id(0) < P_ROWS // FINAL_ROWS
    pl.when(is_prompt)(lambda: run(yp_ref))
    pl.when(jnp.logical_not(is_prompt))(lambda: run(ys_ref))


def final_norm(x, nw):
    n_p = P_ROWS // FINAL_ROWS
    return pl.pallas_call(
        _final_norm_kernel,
        out_shape=(jax.ShapeDtypeStruct((P_ROWS, D_MODEL), F32),
                   jax.ShapeDtypeStruct((S_ROWS, D_MODEL), F32)),
        grid=(M_ROWS // FINAL_ROWS,),
        in_specs=[
            pl.BlockSpec((FINAL_ROWS, D_MODEL), lambda i: (i, 0)),
            pl.BlockSpec((1, D_MODEL), lambda i: (0, 0)),
        ],
        out_specs=(
            pl.BlockSpec((FINAL_ROWS, D_MODEL), lambda i: (jnp.minimum(i, n_p - 1), 0)),
            pl.BlockSpec((S_ROWS, D_MODEL), lambda i: (0, 0)),
        ),
        compiler_params=_cparams(("arbitrary",)),
        name="final_norm",
    )(x, nw)


def _rope_tables(pos):
    half = RET_HEAD_DIM // 2
    inv = ROPE_BASE ** (-jnp.arange(half, dtype=F32) / half)
    ang = pos.astype(F32)[:, None] * inv[None, :]
    return jnp.cos(ang), jnp.sin(ang)


def _log_gamma():
    return jnp.log1p(-jnp.exp2(-5.0 - jnp.arange(RET_HEADS, dtype=F32)))


def _decay_tables(chunk):
    lg = _log_gamma()
    idx = jnp.arange(chunk, dtype=F32)
    diff = idx[:, None] - idx[None, :]
    decay_in = jnp.where(diff[None] >= 0.0,
                         jnp.exp(lg[:, None, None] * jnp.maximum(diff, 0.0)[None]), 0.0)
    decay_q = jnp.exp(lg[:, None] * (idx[None, :] + 1.0))
    decay_k = jnp.exp(lg[:, None] * (chunk - 1.0 - idx[None, :]))
    decay_chunk = jnp.exp(lg * chunk)
    return decay_in, decay_q, decay_k, decay_chunk


def _to_sample_rows(a):
    d = a.shape[-1]
    return a.reshape(N_S_TILES, S_TILE_B, DEC_SEQ, d).transpose(0, 2, 1, 3).reshape(S_ROWS, d)


def _from_sample_rows(a):
    d = a.shape[-1]
    return a.reshape(N_S_TILES, DEC_SEQ, S_TILE_B, d).transpose(0, 2, 1, 3).reshape(DEC_BATCH, DEC_SEQ, d)


def kernel(x_prompt, x_sample, mem_prompt, state_ret, state_pool, cache_mem_k, cache_mem_v,
           attn_norm_w, w_in, ret_norm_w, pool_w, pool_scale, w_out, xattn_norm_w, mem_norm_w,
           w_xq, w_mk, w_mv, w_xo, mlp_norm_w, w_up, w_down, final_norm_w):
    cos_p, sin_p = _rope_tables(jnp.arange(SEQ))
    cos_s, sin_s = _rope_tables(jnp.arange(DEC_SEQ) + PAST_LEN)
    cos_s = jnp.repeat(cos_s, S_TILE_B, axis=0)
    sin_s = jnp.repeat(sin_s, S_TILE_B, axis=0)
    din_p, dq_p, dk_p, dc_p = _decay_tables(RET_CHUNK)
    half = RET_HEAD_DIM // 2
    dq_p = jnp.broadcast_to(dq_p[:, :, None], (RET_HEADS, RET_CHUNK, half))
    dk_p = jnp.broadcast_to(dk_p[:, :, None], (RET_HEADS, RET_CHUNK, half))
    dc_p = jnp.broadcast_to(dc_p[:, None, None], (RET_HEADS, 1, RET_HEAD_DIM))
    din_s, dq_s, dk_s, dc_s = _decay_tables(DEC_SEQ)
    dec_s = jnp.concatenate([din_s.reshape(RET_HEADS, DEC_SEQ * DEC_SEQ), dq_s, dk_s,
                             dc_s[:, None], jnp.zeros((RET_HEADS, 7), F32)], axis=1)

    row3 = lambda a: a.reshape(DEPTH, 1, a.shape[-1])
    attn_nw, xattn_nw, mem_nw, mlp_nw = map(row3, (attn_norm_w, xattn_norm_w, mem_norm_w, mlp_norm_w))
    ret_nw, pool_sc = row3(ret_norm_w), row3(pool_scale)
    w_in_b, w_out_b, w_xq_b, w_mk_b, w_mv_b, w_xo_b, w_up_b, w_down_b = (
        w.astype(BF16) for w in (w_in, w_out, w_xq, w_mk, w_mv, w_xo, w_up, w_down))

    x = jnp.concatenate([x_prompt.reshape(P_ROWS, D_MODEL), _to_sample_rows(x_sample)], axis=0)
    mem = mem_prompt.reshape(BATCH * N_MEM, D_MODEL)

    ret_p = buf_p = mk_p = mv_p = ret_s = buf_s = None
    for l in range(DEPTH):
        z = norm_matmul(x, attn_nw, w_in_b, l, tm=TM, tn=1024, out_dtype=F32)
        mix_r, ret_p = retention_prompt(z, cos_p, sin_p, din_p, dq_p, dk_p, dc_p, ret_nw, l, ret_p)
        mix_r, ret_s = retention_sample(dec_s, z, cos_s, sin_s, state_ret, ret_nw, mix_r, l, ret_s)
        mix_p, buf_p = pool_prompt(z, pool_w, pool_sc, l, buf_p)
        mix_p, buf_s = pool_sample(z, state_pool, pool_w, pool_sc, mix_p, l, buf_s)
        x = matmul2_residual(mix_r, mix_p, w_out_b, x, l, tm=TM, tn=1024)

        mk_p = norm_matmul_stacked(mem, mem_nw, w_mk_b, l, mk_p, tm=1024, tn=1024)
        mv_p = norm_matmul_stacked(mem, mem_nw, w_mv_b, l, mv_p, tm=1024, tn=1024)
        qx = norm_matmul(x, xattn_nw, w_xq_b, l, tm=TM, tn=1024, out_dtype=BF16)
        att = xattn_prompt(qx, mk_p, mv_p, l)
        att_s = xattn_sample(_sample_rows_to_heads(qx[P_ROWS:]), cache_mem_k, cache_mem_v, l)
        att = lax.dynamic_update_slice(att, _heads_to_sample_rows(att_s), (P_ROWS, 0))
        x = matmul_residual(att, w_xo_b, x, l, tm=TM, tn=1024)

        x = mlp_residual(x, mlp_nw, w_up_b, w_down_b, l, tm=TM)

    y_p, y_s = final_norm(x, final_norm_w.reshape(1, D_MODEL))
    mem_shape = (DEPTH, BATCH, N_MEM, MEM_HEADS, MEM_HEAD_DIM)
    return (y_p.reshape(BATCH, SEQ, D_MODEL), _from_sample_rows(y_s),
            ret_p, buf_p, mk_p.reshape(mem_shape), mv_p.reshape(mem_shape), ret_s, buf_s)
```

```python
import functools

import jax
import jax.numpy as jnp
from jax import lax
from jax.experimental import pallas as pl
from jax.experimental.pallas import tpu as pltpu

D_MODEL = 2048
BATCH = 4
SEQ = 2048
DEPTH = 2
DEC_BATCH = 128
DEC_SEQ = 4
PAST_LEN = 16384
D_RET = 1024
RET_HEADS = 4
RET_HEAD_DIM = 256
D_POOL = 1024
POOL_WINDOWS = (2, 4, 8, 16)
POOL_GROUP_DIM = 256
POOL_BUF = 15
D_IN = 5120
N_MEM = 256
MEM_HEADS = 4
MEM_HEAD_DIM = 512
D_FF = 8192
RET_CHUNK = 128
ROPE_BASE = 10000.0
EPS = 1e-6

F32 = jnp.float32
BF16 = jnp.bfloat16

P_ROWS = BATCH * SEQ
S_ROWS = DEC_BATCH * DEC_SEQ
M_ROWS = P_ROWS + S_ROWS
S_TILE_B = 8
S_TILE_ROWS = S_TILE_B * DEC_SEQ
N_S_TILES = DEC_BATCH // S_TILE_B
S_BLK0 = P_ROWS // S_TILE_ROWS

TM = 1088
TM_RESIDENT = 544
VMEM_LIMIT = 58 * 1024 * 1024


def _cparams(sem):
    return pltpu.CompilerParams(dimension_semantics=sem, vmem_limit_bytes=VMEM_LIMIT)


_ANY = pl.BlockSpec(memory_space=pl.ANY)


def _skip_refs(kernel_fn, start, count):
    def wrapped(*refs):
        return kernel_fn(*refs[:start], *refs[start + count:])
    return wrapped


def _carried(prev):
    return [] if prev is None else [prev]


def _rms_rows(x, nw):
    ms = jnp.mean(x * x, axis=-1, keepdims=True)
    return x * lax.rsqrt(ms + EPS) * nw


def _for_row_chunks(n_rows, chunk, body):
    def step(c, carry):
        body(pl.ds(pl.multiple_of(c * chunk, chunk), chunk))
        return carry
    lax.fori_loop(0, n_rows // chunk, step, 0)


def _norm_chunk(n_rows):
    return 272 if n_rows % 272 == 0 else 128


def _cast_weight_once(w_ref, wb_ref):
    @pl.when(pl.program_id(0) == 0)
    def _():
        def body(rows):
            wb_ref[rows, :] = w_ref[rows, :].astype(BF16)
        _for_row_chunks(w_ref.shape[0], 256, body)


def _norm_matmul_kernel(x_ref, nw_ref, w_ref, o_ref, h_ref, *, tm):
    @pl.when(pl.program_id(1) == 0)
    def _():
        def body(rows):
            h_ref[rows, :] = _rms_rows(x_ref[rows, :], nw_ref[...]).astype(BF16)
        _for_row_chunks(tm, _norm_chunk(tm), body)

    o_ref[...] = jnp.dot(h_ref[...], w_ref[...].astype(BF16),
                         preferred_element_type=F32).astype(o_ref.dtype)


def norm_matmul(x, nw, w, layer, *, tm, tn, out_dtype):
    m, k = x.shape
    n = w.shape[-1]
    return pl.pallas_call(
        functools.partial(_norm_matmul_kernel, tm=tm),
        out_shape=jax.ShapeDtypeStruct((m, n), out_dtype),
        grid=(m // tm, n // tn),
        in_specs=[
            pl.BlockSpec((tm, k), lambda i, j: (i, 0)),
            pl.BlockSpec((None, 1, k), lambda i, j: (layer, 0, 0)),
            pl.BlockSpec((None, k, tn), lambda i, j: (layer, 0, j)),
        ],
        out_specs=pl.BlockSpec((tm, tn), lambda i, j: (i, j)),
        scratch_shapes=[pltpu.VMEM((tm, k), BF16)],
        compiler_params=_cparams(("parallel", "arbitrary")),
        name="norm_matmul",
    )(x, nw, w)


def norm_matmul_stacked(x, nw, w, layer, prev, *, tm, tn):
    m, k = x.shape
    n = w.shape[-1]
    extra = _carried(prev)
    return pl.pallas_call(
        _skip_refs(functools.partial(_norm_matmul_kernel, tm=tm), 3, len(extra)),
        out_shape=jax.ShapeDtypeStruct((DEPTH, m, n), F32),
        grid=(m // tm, n // tn),
        in_specs=[
            pl.BlockSpec((tm, k), lambda i, j: (i, 0)),
            pl.BlockSpec((None, 1, k), lambda i, j: (layer, 0, 0)),
            pl.BlockSpec((None, k, tn), lambda i, j: (layer, 0, j)),
        ] + [_ANY] * len(extra),
        out_specs=pl.BlockSpec((None, tm, tn), lambda i, j: (layer, i, j)),
        scratch_shapes=[pltpu.VMEM((tm, k), BF16)],
        input_output_aliases={3: 0} if extra else {},
        compiler_params=_cparams(("parallel", "arbitrary")),
        name="norm_matmul_stacked",
    )(x, nw, w, *extra)


def _matmul2_res_kernel(a1_ref, a2_ref, w_ref, r_ref, o_ref, *, k1):
    acc = jnp.dot(a1_ref[...], w_ref[:k1, :], preferred_element_type=F32)
    acc = acc + jnp.dot(a2_ref[...], w_ref[k1:, :], preferred_element_type=F32)
    o_ref[...] = r_ref[...] + acc


def matmul2_residual(a1, a2, w, res, layer, *, tm, tn):
    m, k1 = a1.shape
    k2 = a2.shape[1]
    n = w.shape[-1]
    return pl.pallas_call(
        functools.partial(_matmul2_res_kernel, k1=k1),
        out_shape=jax.ShapeDtypeStruct((m, n), F32),
        grid=(m // tm, n // tn),
        in_specs=[
            pl.BlockSpec((tm, k1), lambda i, j: (i, 0)),
            pl.BlockSpec((tm, k2), lambda i, j: (i, 0)),
            pl.BlockSpec((None, k1 + k2, tn), lambda i, j: (layer, 0, j)),
            pl.BlockSpec((tm, tn), lambda i, j: (i, j)),
        ],
        out_specs=pl.BlockSpec((tm, tn), lambda i, j: (i, j)),
        compiler_params=_cparams(("parallel", "arbitrary")),
        name="matmul2_residual",
    )(a1, a2, w, res)


def _matmul_res_kernel(a_ref, w_ref, r_ref, o_ref):
    o_ref[...] = r_ref[...] + jnp.dot(a_ref[...], w_ref[...], preferred_element_type=F32)


def matmul_residual(a, w, res, layer, *, tm, tn):
    m, k = a.shape
    n = w.shape[-1]
    return pl.pallas_call(
        _matmul_res_kernel,
        out_shape=jax.ShapeDtypeStruct((m, n), F32),
        grid=(m // tm, n // tn),
        in_specs=[
            pl.BlockSpec((tm, k), lambda i, j: (i, 0)),
            pl.BlockSpec((None, k, tn), lambda i, j: (layer, 0, j)),
            pl.BlockSpec((tm, tn), lambda i, j: (i, j)),
        ],
        out_specs=pl.BlockSpec((tm, tn), lambda i, j: (i, j)),
        compiler_params=_cparams(("parallel", "arbitrary")),
        name="matmul_residual",
    )(a, w, res)


def _resident_weight_spec(w, layer):
    return pl.BlockSpec((None,) + w.shape[1:], lambda i: (layer, 0, 0), pipeline_mode=pl.Buffered(1))


def _norm_matmul_resident_kernel(x_ref, nw_ref, w_ref, o_ref, wb_ref):
    _cast_weight_once(w_ref, wb_ref)
    h = _rms_rows(x_ref[...], nw_ref[...]).astype(BF16)
    o_ref[...] = jnp.dot(h, wb_ref[...], preferred_element_type=F32).astype(o_ref.dtype)


def norm_matmul_resident(x, nw, w, layer, *, tm, out_dtype):
    m, k = x.shape
    n = w.shape[-1]
    return pl.pallas_call(
        _norm_matmul_resident_kernel,
        out_shape=jax.ShapeDtypeStruct((m, n), out_dtype),
        grid=(m // tm,),
        in_specs=[
            pl.BlockSpec((tm, k), lambda i: (i, 0)),
            pl.BlockSpec((None, 1, k), lambda i: (layer, 0, 0)),
            _resident_weight_spec(w, layer),
        ],
        out_specs=pl.BlockSpec((tm, n), lambda i: (i, 0)),
        scratch_shapes=[pltpu.VMEM((k, n), BF16)],
        compiler_params=_cparams(("arbitrary",)),
        name="norm_matmul_resident",
    )(x, nw, w)


def _matmul2_res_resident_kernel(a1_ref, a2_ref, w_ref, r_ref, o_ref, wb_ref, *, k1):
    _cast_weight_once(w_ref, wb_ref)
    acc = jnp.dot(a1_ref[...], wb_ref[:k1, :], preferred_element_type=F32)
    acc = acc + jnp.dot(a2_ref[...], wb_ref[k1:, :], preferred_element_type=F32)
    o_ref[...] = r_ref[...] + acc


def matmul2_residual_resident(a1, a2, w, res, layer, *, tm):
    m, k1 = a1.shape
    k2 = a2.shape[1]
    n = w.shape[-1]
    return pl.pallas_call(
        functools.partial(_matmul2_res_resident_kernel, k1=k1),
        out_shape=jax.ShapeDtypeStruct((m, n), F32),
        grid=(m // tm,),
        in_specs=[
            pl.BlockSpec((tm, k1), lambda i: (i, 0)),
            pl.BlockSpec((tm, k2), lambda i: (i, 0)),
            _resident_weight_spec(w, layer),
            pl.BlockSpec((tm, n), lambda i: (i, 0)),
        ],
        out_specs=pl.BlockSpec((tm, n), lambda i: (i, 0)),
        scratch_shapes=[pltpu.VMEM((k1 + k2, n), BF16)],
        compiler_params=_cparams(("arbitrary",)),
        name="matmul2_residual_resident",
    )(a1, a2, w, res)


def _matmul_res_resident_kernel(a_ref, w_ref, r_ref, o_ref, wb_ref):
    _cast_weight_once(w_ref, wb_ref)
    o_ref[...] = r_ref[...] + jnp.dot(a_ref[...], wb_ref[...], preferred_element_type=F32)


def matmul_residual_resident(a, w, res, layer, *, tm):
    m, k = a.shape
    n = w.shape[-1]
    return pl.pallas_call(
        _matmul_res_resident_kernel,
        out_shape=jax.ShapeDtypeStruct((m, n), F32),
        grid=(m // tm,),
        in_specs=[
            pl.BlockSpec((tm, k), lambda i: (i, 0)),
            _resident_weight_spec(w, layer),
            pl.BlockSpec((tm, n), lambda i: (i, 0)),
        ],
        out_specs=pl.BlockSpec((tm, n), lambda i: (i, 0)),
        scratch_shapes=[pltpu.VMEM((k, n), BF16)],
        compiler_params=_cparams(("arbitrary",)),
        name="matmul_residual_resident",
    )(a, w, res)


MLP_TF = 512
MLP_TN = 512
MLP_KC = 4096
MLP_UP_STEPS = D_FF // MLP_TF
MLP_K_STEPS = D_FF // MLP_KC
MLP_DOWN_STEPS = (D_MODEL // MLP_TN) * MLP_K_STEPS


def _mlp_kernel(x_ref, nw_ref, wu_ref, wd_ref, o_ref, h_ref, hid_ref, *, tm):
    s = pl.program_id(1)

    @pl.when(s == 0)
    def _():
        def body(rows):
            h_ref[rows, :] = _rms_rows(x_ref[rows, :], nw_ref[...]).astype(BF16)
        _for_row_chunks(tm, _norm_chunk(tm), body)

    @pl.when(s < MLP_UP_STEPS)
    def _():
        hid = jnp.dot(h_ref[...], wu_ref[...], preferred_element_type=F32)
        per_chunk = MLP_KC // MLP_TF
        c0 = pl.multiple_of((s % per_chunk) * MLP_TF, MLP_TF)
        hid_ref[s // per_chunk, :, pl.ds(c0, MLP_TF)] = jnp.square(jnp.maximum(hid, 0.0)).astype(BF16)

    @pl.when(s >= MLP_UP_STEPS)
    def _():
        d = s - MLP_UP_STEPS
        kc = d % MLP_K_STEPS
        part = jnp.dot(hid_ref[kc], wd_ref[...], preferred_element_type=F32)

        @pl.when(kc == 0)
        def _():
            n0 = pl.multiple_of((d // MLP_K_STEPS) * MLP_TN, MLP_TN)
            o_ref[...] = x_ref[:, pl.ds(n0, MLP_TN)] + part

        @pl.when(kc != 0)
        def _():
            o_ref[...] += part


def mlp_residual(x, nw, w_up, w_down, layer, *, tm):
    m, k = x.shape
    down = lambda s: jnp.maximum(s - MLP_UP_STEPS, 0)
    return pl.pallas_call(
        functools.partial(_mlp_kernel, tm=tm),
        out_shape=jax.ShapeDtypeStruct((m, k), F32),
        grid=(m // tm, MLP_UP_STEPS + MLP_DOWN_STEPS),
        in_specs=[
            pl.BlockSpec((tm, k), lambda i, s: (i, 0), pipeline_mode=pl.Buffered(1)),
            pl.BlockSpec((None, 1, k), lambda i, s: (layer, 0, 0)),
            pl.BlockSpec((None, k, MLP_TF), lambda i, s: (layer, 0, jnp.minimum(s, MLP_UP_STEPS - 1))),
            pl.BlockSpec((None, MLP_KC, MLP_TN),
                         lambda i, s: (layer, down(s) % MLP_K_STEPS, down(s) // MLP_K_STEPS)),
        ],
        out_specs=pl.BlockSpec((tm, MLP_TN), lambda i, s: (i, down(s) // MLP_K_STEPS)),
        scratch_shapes=[pltpu.VMEM((tm, k), BF16),
                        pltpu.VMEM((MLP_K_STEPS, tm, MLP_KC), BF16)],
        compiler_params=_cparams(("parallel", "arbitrary")),
        name="mlp_residual",
    )(x, nw, w_up, w_down)


def _rotate(x, cos, sin):
    half = x.shape[-1] // 2
    x1, x2 = x[:, :half], x[:, half:]
    return jnp.concatenate([x1 * cos - x2 * sin, x2 * cos + x1 * sin], axis=-1)


def _head_norm_gate(o, g, rnw):
    o = o * lax.rsqrt(jnp.mean(o * o, axis=-1, keepdims=True) + EPS)
    return (g * jax.nn.sigmoid(g)) * (o * rnw)


def _dot_t_lhs(a, b):
    return lax.dot_general(a, b, (((0,), (0,)), ((), ())), preferred_element_type=F32)


def _dot_t_rhs(a, b):
    return lax.dot_general(a, b, (((1,), (1,)), ((), ())), preferred_element_type=F32)


RET_P_ROWS = 512


def _ret_prompt_kernel(q_ref, k_ref, v_ref, g_ref, cos_ref, sin_ref, din_ref, dq_ref, dk_ref,
                       dc_ref, rnw_ref, o_ref, s_out_ref, s_ref):
    @pl.when(pl.program_id(1) == 0)
    def _():
        s_ref[...] = jnp.zeros_like(s_ref)

    for c in range(RET_P_ROWS // RET_CHUNK):
        rows = slice(c * RET_CHUNK, (c + 1) * RET_CHUNK)
        cos, sin = cos_ref[rows, :], sin_ref[rows, :]
        for h in range(RET_HEADS):
            cols = slice(h * RET_HEAD_DIM, (h + 1) * RET_HEAD_DIM)
            dq = jnp.concatenate([dq_ref[h], dq_ref[h]], axis=-1)
            dk = jnp.concatenate([dk_ref[h], dk_ref[h]], axis=-1)
            q = _rotate(q_ref[rows, cols], cos, sin)
            k = _rotate(k_ref[rows, cols], cos, sin) * (RET_HEAD_DIM ** -0.5)
            vb = v_ref[rows, cols].astype(BF16)
            s = s_ref[h]
            scores = _dot_t_rhs(q.astype(BF16), k.astype(BF16)) * din_ref[h]
            o = jnp.dot(scores.astype(BF16), vb, preferred_element_type=F32)
            o = o + jnp.dot((q * dq).astype(BF16), s.astype(BF16), preferred_element_type=F32)
            s_ref[h] = s * dc_ref[h] + _dot_t_lhs((k * dk).astype(BF16), vb)
            o_ref[rows, cols] = _head_norm_gate(o, g_ref[rows, cols], rnw_ref[:, cols]).astype(o_ref.dtype)

    @pl.when(pl.program_id(1) == pl.num_programs(1) - 1)
    def _():
        s_out_ref[...] = s_ref[...]


def retention_prompt(z, cos, sin, din, dq, dk, dc, rnw, layer, state_prev):
    nt = SEQ // RET_P_ROWS
    extra = _carried(state_prev)
    zspec = lambda part: pl.BlockSpec((RET_P_ROWS, D_RET), lambda b, t: (b * nt + t, part))
    whole = lambda a: pl.BlockSpec(a.shape, lambda b, t: (0,) * a.ndim)
    state_block = (RET_HEADS, RET_HEAD_DIM, RET_HEAD_DIM)
    return pl.pallas_call(
        _skip_refs(_ret_prompt_kernel, 11, len(extra)),
        out_shape=(jax.ShapeDtypeStruct((M_ROWS, D_RET), BF16),
                   jax.ShapeDtypeStruct((DEPTH, BATCH) + state_block, F32)),
        grid=(BATCH, nt),
        in_specs=[
            zspec(0), zspec(1), zspec(2), zspec(3),
            pl.BlockSpec((RET_P_ROWS, RET_HEAD_DIM // 2), lambda b, t: (t, 0)),
            pl.BlockSpec((RET_P_ROWS, RET_HEAD_DIM // 2), lambda b, t: (t, 0)),
            whole(din), whole(dq), whole(dk), whole(dc),
            pl.BlockSpec((None, 1, D_RET), lambda b, t: (layer, 0, 0)),
        ] + [_ANY] * len(extra),
        out_specs=(
            pl.BlockSpec((RET_P_ROWS, D_RET), lambda b, t: (b * nt + t, 0)),
            pl.BlockSpec((None, None) + state_block, lambda b, t: (layer, b, 0, 0, 0)),
        ),
        scratch_shapes=[pltpu.VMEM(state_block, F32)],
        input_output_aliases={11: 1} if extra else {},
        compiler_params=_cparams(("parallel", "arbitrary")),
        name="retention_prompt",
    )(z, z, z, z, cos, sin, din, dq, dk, dc, rnw, *extra)


def _ret_sample_kernel(dec_ref, z_ref, cos_ref, sin_ref, st_ref, rnw_ref, o_ref, st_out_ref):
    h = pl.program_id(1)
    c0 = pl.multiple_of(h * RET_HEAD_DIM, RET_HEAD_DIM)
    col = lambda part: pl.ds(pl.multiple_of(part * D_RET + c0, RET_HEAD_DIM), RET_HEAD_DIM)
    cos, sin = cos_ref[...], sin_ref[...]
    q = _rotate(z_ref[:, col(0)], cos, sin)
    k = _rotate(z_ref[:, col(1)], cos, sin) * (RET_HEAD_DIM ** -0.5)
    v = z_ref[:, col(2)]
    g = z_ref[:, col(3)]
    slab = lambda a, t: a[t * S_TILE_B:(t + 1) * S_TILE_B, :]

    intra = []
    for t in range(DEC_SEQ):
        acc = None
        for j in range(t + 1):
            w = jnp.sum(slab(q, t) * slab(k, j), axis=-1, keepdims=True) * dec_ref[h, t * DEC_SEQ + j]
            acc = w * slab(v, j) if acc is None else acc + w * slab(v, j)
        intra.append(acc)
    o = jnp.concatenate(intra, axis=0)

    dq_rows = jnp.concatenate(
        [jnp.full((S_TILE_B, 1), 1.0, F32) * dec_ref[h, 16 + t] for t in range(DEC_SEQ)], axis=0)
    dk_rows = jnp.concatenate(
        [jnp.full((S_TILE_B, 1), 1.0, F32) * dec_ref[h, 20 + t] for t in range(DEC_SEQ)], axis=0)
    qd = (q * dq_rows).astype(BF16)
    kd = k * dk_rows
    vb = v.astype(BF16)
    dchunk = dec_ref[h, 24]
    row_b = lax.broadcasted_iota(jnp.int32, (S_TILE_ROWS, 1), 0) % S_TILE_B
    for b in range(S_TILE_B):
        mine = row_b == b
        s = st_ref[b]
        o = o + jnp.where(mine, jnp.dot(qd, s.astype(BF16), preferred_element_type=F32), 0.0)
        st_out_ref[b] = s * dchunk + _dot_t_lhs(jnp.where(mine, kd, 0.0).astype(BF16), vb)
    o_ref[...] = _head_norm_gate(o, g, rnw_ref[...]).astype(o_ref.dtype)


def retention_sample(dec, z, cos_s, sin_s, state_ret, rnw, mix, layer, state_prev):
    extra = [mix] + _carried(state_prev)
    return pl.pallas_call(
        _skip_refs(_ret_sample_kernel, 6, len(extra)),
        out_shape=(jax.ShapeDtypeStruct(mix.shape, mix.dtype),
                   jax.ShapeDtypeStruct(state_ret.shape, F32)),
        grid=(N_S_TILES, RET_HEADS),
        in_specs=[
            pl.BlockSpec(memory_space=pltpu.SMEM),
            pl.BlockSpec((S_TILE_ROWS, D_IN), lambda i, h: (S_BLK0 + i, 0)),
            pl.BlockSpec((S_TILE_ROWS, RET_HEAD_DIM // 2), lambda i, h: (0, 0)),
            pl.BlockSpec((S_TILE_ROWS, RET_HEAD_DIM // 2), lambda i, h: (0, 0)),
            pl.BlockSpec((None, S_TILE_B, None, RET_HEAD_DIM, RET_HEAD_DIM),
                         lambda i, h: (layer, i, h, 0, 0)),
            pl.BlockSpec((None, 1, RET_HEAD_DIM), lambda i, h: (layer, 0, h)),
        ] + [_ANY] * len(extra),
        out_specs=(
            pl.BlockSpec((S_TILE_ROWS, RET_HEAD_DIM), lambda i, h: (S_BLK0 + i, h)),
            pl.BlockSpec((None, S_TILE_B, None, RET_HEAD_DIM, RET_HEAD_DIM),
                         lambda i, h: (layer, i, h, 0, 0)),
        ),
        input_output_aliases={6 + n: n for n in range(len(extra))},
        compiler_params=_cparams(("parallel", "arbitrary")),
        name="retention_sample",
    )(dec, z, cos_s, sin_s, state_ret, rnw, *extra)


POOL_P_ROWS = 256
POOL_HALO = 16


def _shift_rows(a, s):
    return pltpu.roll(a, s, axis=0)


def _pool_prompt_kernel(u_ref, pw_ref, ps_ref, o_ref, buf_ref, halo_ref):
    t = pl.program_id(1)

    @pl.when(t == 0)
    def _():
        halo_ref[...] = jnp.zeros_like(halo_ref)

    pos = t * POOL_P_ROWS + lax.broadcasted_iota(jnp.int32, (POOL_P_ROWS, 1), 0)
    for g, w in enumerate(POOL_WINDOWS):
        cols = slice(g * POOL_GROUP_DIM, (g + 1) * POOL_GROUP_DIM)
        u = u_ref[:, cols]
        a = jnp.concatenate([halo_ref[:, cols], u], axis=0)
        s = 1
        while s < w:
            a = a + _shift_rows(a, s)
            s *= 2
        cnt = jnp.minimum(pos + 1, w).astype(F32)
        pooled = a[POOL_HALO:, :] / cnt - u
        pm = jnp.dot(pooled.astype(BF16), pw_ref[g].astype(BF16), preferred_element_type=F32)
        o_ref[:, cols] = (pm * ps_ref[:, cols]).astype(o_ref.dtype)

    halo_ref[...] = u_ref[POOL_P_ROWS - POOL_HALO:, :]

    @pl.when(t == pl.num_programs(1) - 1)
    def _():
        buf_ref[...] = u_ref[POOL_P_ROWS - POOL_BUF:, :]


def pool_prompt(z, pool_w, pool_scale, layer, buf_prev):
    nt = SEQ // POOL_P_ROWS
    extra = _carried(buf_prev)
    return pl.pallas_call(
        _skip_refs(_pool_prompt_kernel, 3, len(extra)),
        out_shape=(jax.ShapeDtypeStruct((M_ROWS, D_POOL), BF16),
                   jax.ShapeDtypeStruct((DEPTH, BATCH, POOL_BUF, D_POOL), F32)),
        grid=(BATCH, nt),
        in_specs=[
            pl.BlockSpec((POOL_P_ROWS, D_POOL), lambda b, t: (b * nt + t, 4)),
            pl.BlockSpec((None, 4, POOL_GROUP_DIM, POOL_GROUP_DIM), lambda b, t: (layer, 0, 0, 0)),
            pl.BlockSpec((None, 1, D_POOL), lambda b, t: (layer, 0, 0)),
        ] + [_ANY] * len(extra),
        out_specs=(
            pl.BlockSpec((POOL_P_ROWS, D_POOL), lambda b, t: (b * nt + t, 0)),
            pl.BlockSpec((None, None, POOL_BUF, D_POOL), lambda b, t: (layer, b, 0, 0)),
        ),
        scratch_shapes=[pltpu.VMEM((POOL_HALO, D_POOL), F32)],
        input_output_aliases={3: 1} if extra else {},
        compiler_params=_cparams(("parallel", "arbitrary")),
        name="pool_prompt",
    )(z, pool_w, pool_scale, *extra)


def _pool_sample_kernel(u_ref, buf_ref, pw_ref, ps_ref, o_ref, nbuf_ref):

    def ext(r, cols):
        if r < POOL_BUF:
            return buf_ref[:, r, cols]
        return u_ref[(r - POOL_BUF) * S_TILE_B:(r - POOL_BUF + 1) * S_TILE_B, cols]

    for g, w in enumerate(POOL_WINDOWS):
        cols = slice(g * POOL_GROUP_DIM, (g + 1) * POOL_GROUP_DIM)
        pooled = []
        for t in range(DEC_SEQ):
            win = ext(POOL_BUF + t, cols)
            for r in range(POOL_BUF + t - w + 1, POOL_BUF + t):
                win = win + ext(r, cols)
            cnt = float(min(t + PAST_LEN + 1, w))
            pooled.append(win / cnt - ext(POOL_BUF + t, cols))
        pooled = jnp.concatenate(pooled, axis=0)
        pm = jnp.dot(pooled.astype(BF16), pw_ref[g].astype(BF16), preferred_element_type=F32)
        o_ref[:, cols] = (pm * ps_ref[:, cols]).astype(o_ref.dtype)

    full = slice(0, D_POOL)
    for r in range(POOL_BUF):
        nbuf_ref[:, r, :] = ext(r + DEC_SEQ, full)


def pool_sample(z, state_pool, pool_w, pool_scale, mix, layer, buf_prev):
    extra = [mix] + _carried(buf_prev)
    return pl.pallas_call(
        _skip_refs(_pool_sample_kernel, 4, len(extra)),
        out_shape=(jax.ShapeDtypeStruct(mix.shape, mix.dtype),
                   jax.ShapeDtypeStruct(state_pool.shape, F32)),
        grid=(N_S_TILES,),
        in_specs=[
            pl.BlockSpec((S_TILE_ROWS, D_POOL), lambda i: (S_BLK0 + i, 4)),
            pl.BlockSpec((None, S_TILE_B, POOL_BUF, D_POOL), lambda i: (layer, i, 0, 0)),
            pl.BlockSpec((None, 4, POOL_GROUP_DIM, POOL_GROUP_DIM), lambda i: (layer, 0, 0, 0)),
            pl.BlockSpec((None, 1, D_POOL), lambda i: (layer, 0, 0)),
        ] + [_ANY] * len(extra),
        out_specs=(
            pl.BlockSpec((S_TILE_ROWS, D_POOL), lambda i: (S_BLK0 + i, 0)),
            pl.BlockSpec((None, S_TILE_B, POOL_BUF, D_POOL), lambda i: (layer, i, 0, 0)),
        ),
        input_output_aliases={4 + n: n for n in range(len(extra))},
        compiler_params=_cparams(("arbitrary",)),
        name="pool_sample",
    )(z, state_pool, pool_w, pool_scale, *extra)


def _softmax_rows(s):
    m = jnp.max(s, axis=-1, keepdims=True)
    e = jnp.exp(s - m)
    return e / jnp.sum(e, axis=-1, keepdims=True)


XATTN_P_ROWS = 512


def _xattn_prompt_kernel(q_ref, mk_ref, mv_ref, o_ref):
    for h in range(MEM_HEADS):
        cols = slice(h * MEM_HEAD_DIM, (h + 1) * MEM_HEAD_DIM)
        s = _dot_t_rhs(q_ref[:, cols], mk_ref[:, cols].astype(BF16)) * (MEM_HEAD_DIM ** -0.5)
        p = _softmax_rows(s)
        o_ref[:, cols] = jnp.dot(p.astype(BF16), mv_ref[:, cols].astype(BF16),
                                 preferred_element_type=F32).astype(o_ref.dtype)


def xattn_prompt(qx, mk, mv, layer):
    nt = SEQ // XATTN_P_ROWS
    return pl.pallas_call(
        _xattn_prompt_kernel,
        out_shape=jax.ShapeDtypeStruct((M_ROWS, D_MODEL), BF16),
        grid=(BATCH, nt),
        in_specs=[
            pl.BlockSpec((XATTN_P_ROWS, D_MODEL), lambda b, t: (b * nt + t, 0)),
            pl.BlockSpec((None, N_MEM, D_MODEL), lambda b, t: (layer, b, 0)),
            pl.BlockSpec((None, N_MEM, D_MODEL), lambda b, t: (layer, b, 0)),
        ],
        out_specs=pl.BlockSpec((XATTN_P_ROWS, D_MODEL), lambda b, t: (b * nt + t, 0)),
        compiler_params=_cparams(("parallel", "arbitrary")),
        name="xattn_prompt",
    )(qx, mk, mv)


XATTN_S_B = 4
XATTN_S_ROWS = MEM_HEADS * DEC_SEQ
XATTN_S_KEYS = N_MEM * MEM_HEADS


def _xattn_sample_kernel(q_ref, k_ref, v_ref, o_ref):
    row_h = lax.broadcasted_iota(jnp.int32, (XATTN_S_ROWS, XATTN_S_KEYS), 0) // DEC_SEQ
    col_h = lax.broadcasted_iota(jnp.int32, (XATTN_S_ROWS, XATTN_S_KEYS), 1) % MEM_HEADS
    same_head = row_h == col_h
    for bb in range(XATTN_S_B):
        k2 = k_ref[bb].reshape(XATTN_S_KEYS, MEM_HEAD_DIM).astype(BF16)
        v2 = v_ref[bb].reshape(XATTN_S_KEYS, MEM_HEAD_DIM).astype(BF16)
        s = _dot_t_rhs(q_ref[bb], k2) * (MEM_HEAD_DIM ** -0.5)
        s = jnp.where(same_head, s, -jnp.inf)
        p = _softmax_rows(s)
        o_ref[bb] = jnp.dot(p.astype(BF16), v2, preferred_element_type=F32).astype(o_ref.dtype)


def xattn_sample(qs, cache_k, cache_v, layer):
    kv_spec = pl.BlockSpec((None, XATTN_S_B, N_MEM, MEM_HEADS, MEM_HEAD_DIM),
                           lambda i: (layer, i, 0, 0, 0))
    qo_spec = pl.BlockSpec((XATTN_S_B, XATTN_S_ROWS, MEM_HEAD_DIM), lambda i: (i, 0, 0))
    return pl.pallas_call(
        _xattn_sample_kernel,
        out_shape=jax.ShapeDtypeStruct(qs.shape, BF16),
        grid=(DEC_BATCH // XATTN_S_B,),
        in_specs=[qo_spec, kv_spec, kv_spec],
        out_specs=qo_spec,
        compiler_params=_cparams(("parallel",)),
        name="xattn_sample",
    )(qs, cache_k, cache_v)


def _sample_rows_to_heads(a):
    a = a.reshape(N_S_TILES, DEC_SEQ, S_TILE_B, MEM_HEADS, MEM_HEAD_DIM)
    return a.transpose(0, 2, 3, 1, 4).reshape(DEC_BATCH, XATTN_S_ROWS, MEM_HEAD_DIM)


def _heads_to_sample_rows(a):
    a = a.reshape(N_S_TILES, S_TILE_B, MEM_HEADS, DEC_SEQ, MEM_HEAD_DIM)
    return a.transpose(0, 3, 1, 2, 4).reshape(S_ROWS, D_MODEL)


FINAL_ROWS = 512


def _final_norm_kernel(x_ref, nw_ref, yp_ref, ys_ref):
    def run(dst_ref):
        def body(rows):
            dst_ref[rows, :] = _rms_rows(x_ref[rows, :], nw_ref[...])
        _for_row_chunks(FINAL_ROWS, _norm_chunk(FINAL_ROWS), body)

    is_prompt = pl.program_id(0) < P_ROWS // FINAL_ROWS
    pl.when(is_prompt)(lambda: run(yp_ref))
    pl.when(jnp.logical_not(is_prompt))(lambda: run(ys_ref))


def final_norm(x, nw):
    n_p = P_ROWS // FINAL_ROWS
    return pl.pallas_call(
        _final_norm_kernel,
        out_shape=(jax.ShapeDtypeStruct((P_ROWS, D_MODEL), F32),
                   jax.ShapeDtypeStruct((S_ROWS, D_MODEL), F32)),
        grid=(M_ROWS // FINAL_ROWS,),
        in_specs=[
            pl.BlockSpec((FINAL_ROWS, D_MODEL), lambda i: (i, 0)),
            pl.BlockSpec((1, D_MODEL), lambda i: (0, 0)),
        ],
        out_specs=(
            pl.BlockSpec((FINAL_ROWS, D_MODEL), lambda i: (jnp.minimum(i, n_p - 1), 0)),
            pl.BlockSpec((S_ROWS, D_MODEL), lambda i: (0, 0)),
        ),
        compiler_params=_cparams(("arbitrary",)),
        name="final_norm",
    )(x, nw)


def _rope_tables(pos):
    half = RET_HEAD_DIM // 2
    inv = ROPE_BASE ** (-jnp.arange(half, dtype=F32) / half)
    ang = pos.astype(F32)[:, None] * inv[None, :]
    return jnp.cos(ang), jnp.sin(ang)


def _log_gamma():
    return jnp.log1p(-jnp.exp2(-5.0 - jnp.arange(RET_HEADS, dtype=F32)))


def _decay_tables(chunk):
    lg = _log_gamma()
    idx = jnp.arange(chunk, dtype=F32)
    diff = idx[:, None] - idx[None, :]
    decay_in = jnp.where(diff[None] >= 0.0,
                         jnp.exp(lg[:, None, None] * jnp.maximum(diff, 0.0)[None]), 0.0)
    decay_q = jnp.exp(lg[:, None] * (idx[None, :] + 1.0))
    decay_k = jnp.exp(lg[:, None] * (chunk - 1.0 - idx[None, :]))
    decay_chunk = jnp.exp(lg * chunk)
    return decay_in, decay_q, decay_k, decay_chunk


def _to_sample_rows(a):
    d = a.shape[-1]
    return a.reshape(N_S_TILES, S_TILE_B, DEC_SEQ, d).transpose(0, 2, 1, 3).reshape(S_ROWS, d)


def _from_sample_rows(a):
    d = a.shape[-1]
    return a.reshape(N_S_TILES, DEC_SEQ, S_TILE_B, d).transpose(0, 2, 1, 3).reshape(DEC_BATCH, DEC_SEQ, d)


def kernel(x_prompt, x_sample, mem_prompt, state_ret, state_pool, cache_mem_k, cache_mem_v,
           attn_norm_w, w_in, ret_norm_w, pool_w, pool_scale, w_out, xattn_norm_w, mem_norm_w,
           w_xq, w_mk, w_mv, w_xo, mlp_norm_w, w_up, w_down, final_norm_w):
    cos_p, sin_p = _rope_tables(jnp.arange(SEQ))
    cos_s, sin_s = _rope_tables(jnp.arange(DEC_SEQ) + PAST_LEN)
    cos_s = jnp.repeat(cos_s, S_TILE_B, axis=0)
    sin_s = jnp.repeat(sin_s, S_TILE_B, axis=0)
    din_p, dq_p, dk_p, dc_p = _decay_tables(RET_CHUNK)
    half = RET_HEAD_DIM // 2
    dq_p = jnp.broadcast_to(dq_p[:, :, None], (RET_HEADS, RET_CHUNK, half))
    dk_p = jnp.broadcast_to(dk_p[:, :, None], (RET_HEADS, RET_CHUNK, half))
    dc_p = jnp.broadcast_to(dc_p[:, None, None], (RET_HEADS, 1, RET_HEAD_DIM))
    din_s, dq_s, dk_s, dc_s = _decay_tables(DEC_SEQ)
    dec_s = jnp.concatenate([din_s.reshape(RET_HEADS, DEC_SEQ * DEC_SEQ), dq_s, dk_s,
                             dc_s[:, None], jnp.zeros((RET_HEADS, 7), F32)], axis=1)

    row3 = lambda a: a.reshape(DEPTH, 1, a.shape[-1])
    attn_nw, xattn_nw, mem_nw, mlp_nw = map(row3, (attn_norm_w, xattn_norm_w, mem_norm_w, mlp_norm_w))
    ret_nw, pool_sc = row3(ret_norm_w), row3(pool_scale)
    w_in_b, w_up_b, w_down_b = w_in.astype(BF16), w_up.astype(BF16), w_down.astype(BF16)

    x = jnp.concatenate([x_prompt.reshape(P_ROWS, D_MODEL), _to_sample_rows(x_sample)], axis=0)
    mem = mem_prompt.reshape(BATCH * N_MEM, D_MODEL)

    ret_p = buf_p = mk_p = mv_p = ret_s = buf_s = None
    for l in range(DEPTH):
        z = norm_matmul(x, attn_nw, w_in_b, l, tm=TM, tn=1024, out_dtype=F32)
        mix_r, ret_p = retention_prompt(z, cos_p, sin_p, din_p, dq_p, dk_p, dc_p, ret_nw, l, ret_p)
        mix_r, ret_s = retention_sample(dec_s, z, cos_s, sin_s, state_ret, ret_nw, mix_r, l, ret_s)
        mix_p, buf_p = pool_prompt(z, pool_w, pool_sc, l, buf_p)
        mix_p, buf_s = pool_sample(z, state_pool, pool_w, pool_sc, mix_p, l, buf_s)
        x = matmul2_residual_resident(mix_r, mix_p, w_out, x, l, tm=TM_RESIDENT)

        mk_p = norm_matmul_stacked(mem, mem_nw, w_mk, l, mk_p, tm=1024, tn=512)
        mv_p = norm_matmul_stacked(mem, mem_nw, w_mv, l, mv_p, tm=1024, tn=512)
        qx = norm_matmul_resident(x, xattn_nw, w_xq, l, tm=TM_RESIDENT, out_dtype=BF16)
        att = xattn_prompt(qx, mk_p, mv_p, l)
        att_s = xattn_sample(_sample_rows_to_heads(qx[P_ROWS:]), cache_mem_k, cache_mem_v, l)
        att = lax.dynamic_update_slice(att, _heads_to_sample_rows(att_s), (P_ROWS, 0))
        x = matmul_residual_resident(att, w_xo, x, l, tm=TM_RESIDENT)

        x = mlp_residual(x, mlp_nw, w_up_b, w_down_b, l, tm=TM)

    y_p, y_s = final_norm(x, final_norm_w.reshape(1, D_MODEL))
    mem_shape = (DEPTH, BATCH, N_MEM, MEM_HEADS, MEM_HEAD_DIM)
    return (y_p.reshape(BATCH, SEQ, D_MODEL), _from_sample_rows(y_s),
            ret_p, buf_p, mk_p.reshape(mem_shape), mv_p.reshape(mem_shape), ret_s, buf_s)
```

```python
import functools

import jax
import jax.numpy as jnp
from jax import lax
from jax.experimental import pallas as pl
from jax.experimental.pallas import tpu as pltpu

D_MODEL = 2048
BATCH = 4
SEQ = 2048
DEPTH = 2
DEC_BATCH = 128
DEC_SEQ = 4
PAST_LEN = 16384
D_RET = 1024
RET_HEADS = 4
RET_HEAD_DIM = 256
D_POOL = 1024
POOL_WINDOWS = (2, 4, 8, 16)
POOL_GROUP_DIM = 256
POOL_BUF = 15
D_IN = 5120
N_MEM = 256
MEM_HEADS = 4
MEM_HEAD_DIM = 512
D_FF = 8192
RET_CHUNK = 128
ROPE_BASE = 10000.0
EPS = 1e-6

F32 = jnp.float32
BF16 = jnp.bfloat16

P_ROWS = BATCH * SEQ
S_ROWS = DEC_BATCH * DEC_SEQ
M_ROWS = P_ROWS + S_ROWS
S_TILE_B = 8
S_TILE_ROWS = S_TILE_B * DEC_SEQ
N_S_TILES = DEC_BATCH // S_TILE_B
S_BLK0 = P_ROWS // S_TILE_ROWS

TM = 1088
TM_RESIDENT = 544
VMEM_LIMIT = 58 * 1024 * 1024


def _cparams(sem):
    return pltpu.CompilerParams(dimension_semantics=sem, vmem_limit_bytes=VMEM_LIMIT)


_ANY = pl.BlockSpec(memory_space=pl.ANY)


def _skip_refs(kernel_fn, start, count):
    def wrapped(*refs):
        return kernel_fn(*refs[:start], *refs[start + count:])
    return wrapped


def _carried(prev):
    return [] if prev is None else [prev]


def _rms_rows(x, nw):
    ms = jnp.mean(x * x, axis=-1, keepdims=True)
    return x * lax.rsqrt(ms + EPS) * nw


def _for_row_chunks(n_rows, chunk, body):
    def step(c, carry):
        body(pl.ds(pl.multiple_of(c * chunk, chunk), chunk))
        return carry
    lax.fori_loop(0, n_rows // chunk, step, 0)


def _norm_chunk(n_rows):
    return 272 if n_rows % 272 == 0 else 128


def _cast_weight_once(w_ref, wb_ref):
    @pl.when(pl.program_id(0) == 0)
    def _():
        def body(rows):
            wb_ref[rows, :] = w_ref[rows, :].astype(BF16)
        _for_row_chunks(w_ref.shape[0], 256, body)


def _norm_matmul_kernel(x_ref, nw_ref, w_ref, o_ref, h_ref, *, tm):
    @pl.when(pl.program_id(1) == 0)
    def _():
        def body(rows):
            h_ref[rows, :] = _rms_rows(x_ref[rows, :], nw_ref[...]).astype(BF16)
        _for_row_chunks(tm, _norm_chunk(tm), body)

    o_ref[...] = jnp.dot(h_ref[...], w_ref[...].astype(BF16),
                         preferred_element_type=F32).astype(o_ref.dtype)


def norm_matmul(x, nw, w, layer, *, tm, tn, out_dtype):
    m, k = x.shape
    n = w.shape[-1]
    return pl.pallas_call(
        functools.partial(_norm_matmul_kernel, tm=tm),
        out_shape=jax.ShapeDtypeStruct((m, n), out_dtype),
        grid=(m // tm, n // tn),
        in_specs=[
            pl.BlockSpec((tm, k), lambda i, j: (i, 0)),
            pl.BlockSpec((None, 1, k), lambda i, j: (layer, 0, 0)),
            pl.BlockSpec((None, k, tn), lambda i, j: (layer, 0, j)),
        ],
        out_specs=pl.BlockSpec((tm, tn), lambda i, j: (i, j)),
        scratch_shapes=[pltpu.VMEM((tm, k), BF16)],
        compiler_params=_cparams(("parallel", "arbitrary")),
        name="norm_matmul",
    )(x, nw, w)


def norm_matmul_stacked(x, nw, w, layer, prev, *, tm, tn):
    m, k = x.shape
    n = w.shape[-1]
    extra = _carried(prev)
    return pl.pallas_call(
        _skip_refs(functools.partial(_norm_matmul_kernel, tm=tm), 3, len(extra)),
        out_shape=jax.ShapeDtypeStruct((DEPTH, m, n), F32),
        grid=(m // tm, n // tn),
        in_specs=[
            pl.BlockSpec((tm, k), lambda i, j: (i, 0)),
            pl.BlockSpec((None, 1, k), lambda i, j: (layer, 0, 0)),
            pl.BlockSpec((None, k, tn), lambda i, j: (layer, 0, j)),
        ] + [_ANY] * len(extra),
        out_specs=pl.BlockSpec((None, tm, tn), lambda i, j: (layer, i, j)),
        scratch_shapes=[pltpu.VMEM((tm, k), BF16)],
        input_output_aliases={3: 0} if extra else {},
        compiler_params=_cparams(("parallel", "arbitrary")),
        name="norm_matmul_stacked",
    )(x, nw, w, *extra)


def _resident_weight_spec(w, layer):
    return pl.BlockSpec((None,) + w.shape[1:], lambda i: (layer, 0, 0), pipeline_mode=pl.Buffered(1))


def _matmul_resident_kernel(a_ref, w_ref, o_ref, wb_ref):
    _cast_weight_once(w_ref, wb_ref)
    o_ref[...] = jnp.dot(a_ref[...], wb_ref[...], preferred_element_type=F32).astype(o_ref.dtype)


def matmul_resident(a, w, layer, *, tm):
    m, k = a.shape
    n = w.shape[-1]
    return pl.pallas_call(
        _matmul_resident_kernel,
        out_shape=jax.ShapeDtypeStruct((m, n), BF16),
        grid=(m // tm,),
        in_specs=[pl.BlockSpec((tm, k), lambda i: (i, 0)), _resident_weight_spec(w, layer)],
        out_specs=pl.BlockSpec((tm, n), lambda i: (i, 0)),
        scratch_shapes=[pltpu.VMEM((k, n), BF16)],
        compiler_params=_cparams(("arbitrary",)),
        name="matmul_resident",
    )(a, w)


def _matmul_res_resident_kernel(*refs, n_parts):
    a_refs = refs[:n_parts]
    w_ref, r_ref, nw_ref, o_ref, h_ref, wb_ref = refs[n_parts:]
    _cast_weight_once(w_ref, wb_ref)
    acc = r_ref[...]
    k0 = 0
    for a_ref in a_refs:
        k1 = k0 + a_ref.shape[1]
        acc = acc + jnp.dot(a_ref[...], wb_ref[k0:k1, :], preferred_element_type=F32)
        k0 = k1
    o_ref[...] = acc
    h_ref[...] = _rms_rows(acc, nw_ref[...]).astype(BF16)


def matmul_residual_resident(a_parts, w, res, nw, layer, *, tm):
    m, n = res.shape
    k = sum(a.shape[1] for a in a_parts)
    row_tile = lambda width: pl.BlockSpec((tm, width), lambda i: (i, 0))
    return pl.pallas_call(
        functools.partial(_matmul_res_resident_kernel, n_parts=len(a_parts)),
        out_shape=(jax.ShapeDtypeStruct((m, n), F32), jax.ShapeDtypeStruct((m, n), BF16)),
        grid=(m // tm,),
        in_specs=[row_tile(a.shape[1]) for a in a_parts] + [
            _resident_weight_spec(w, layer),
            row_tile(n),
            pl.BlockSpec((None, 1, n), lambda i: (layer, 0, 0)),
        ],
        out_specs=(row_tile(n), row_tile(n)),
        scratch_shapes=[pltpu.VMEM((k, n), BF16)],
        compiler_params=_cparams(("arbitrary",)),
        name="matmul_residual_resident",
    )(*a_parts, w, res, nw)


MLP_TF = 1024
MLP_TN = 512
MLP_KC = 4096
MLP_UP_STEPS = D_FF // MLP_TF
MLP_K_STEPS = D_FF // MLP_KC
MLP_DOWN_STEPS = (D_MODEL // MLP_TN) * MLP_K_STEPS


def _mlp_kernel(h_ref, x_ref, wu_ref, wd_ref, o_ref, hid_ref):
    s = pl.program_id(1)

    @pl.when(s < MLP_UP_STEPS)
    def _():
        hid = jnp.dot(h_ref[...], wu_ref[...], preferred_element_type=F32)
        per_chunk = MLP_KC // MLP_TF
        c0 = pl.multiple_of((s % per_chunk) * MLP_TF, MLP_TF)
        hid_ref[s // per_chunk, :, pl.ds(c0, MLP_TF)] = jnp.square(jnp.maximum(hid, 0.0)).astype(BF16)

    @pl.when(s >= MLP_UP_STEPS)
    def _():
        kc = (s - MLP_UP_STEPS) % MLP_K_STEPS
        part = jnp.dot(hid_ref[kc], wd_ref[...], preferred_element_type=F32)

        @pl.when(kc == 0)
        def _():
            o_ref[...] = x_ref[...] + part

        @pl.when(kc != 0)
        def _():
            o_ref[...] += part


def mlp_residual(h, x, w_up, w_down, layer, *, tm):
    m, k = x.shape
    down = lambda s: jnp.maximum(s - MLP_UP_STEPS, 0)
    out_tile = pl.BlockSpec((tm, MLP_TN), lambda i, s: (i, down(s) // MLP_K_STEPS))
    return pl.pallas_call(
        _mlp_kernel,
        out_shape=jax.ShapeDtypeStruct((m, k), F32),
        grid=(m // tm, MLP_UP_STEPS + MLP_DOWN_STEPS),
        in_specs=[
            pl.BlockSpec((tm, k), lambda i, s: (i, 0)),
            out_tile,
            pl.BlockSpec((None, k, MLP_TF), lambda i, s: (layer, 0, jnp.minimum(s, MLP_UP_STEPS - 1))),
            pl.BlockSpec((None, MLP_KC, MLP_TN),
                         lambda i, s: (layer, down(s) % MLP_K_STEPS, down(s) // MLP_K_STEPS)),
        ],
        out_specs=out_tile,
        scratch_shapes=[pltpu.VMEM((MLP_K_STEPS, tm, MLP_KC), BF16)],
        compiler_params=_cparams(("parallel", "arbitrary")),
        name="mlp_residual",
    )(h, x, w_up, w_down)


def _rotate(x, cos, sin):
    half = x.shape[-1] // 2
    x1, x2 = x[:, :half], x[:, half:]
    return jnp.concatenate([x1 * cos - x2 * sin, x2 * cos + x1 * sin], axis=-1)


def _head_norm_gate(o, g, rnw):
    o = o * lax.rsqrt(jnp.mean(o * o, axis=-1, keepdims=True) + EPS)
    return (g * jax.nn.sigmoid(g)) * (o * rnw)


def _dot_t_lhs(a, b):
    return lax.dot_general(a, b, (((0,), (0,)), ((), ())), preferred_element_type=F32)


def _dot_t_rhs(a, b):
    return lax.dot_general(a, b, (((1,), (1,)), ((), ())), preferred_element_type=F32)


RET_P_ROWS = 512


def _ret_prompt_kernel(q_ref, k_ref, v_ref, g_ref, cos_ref, sin_ref, din_ref, dq_ref, dk_ref,
                       dc_ref, rnw_ref, o_ref, s_out_ref, s_ref):
    @pl.when(pl.program_id(1) == 0)
    def _():
        s_ref[...] = jnp.zeros_like(s_ref)

    for c in range(RET_P_ROWS // RET_CHUNK):
        rows = slice(c * RET_CHUNK, (c + 1) * RET_CHUNK)
        cos, sin = cos_ref[rows, :], sin_ref[rows, :]
        for h in range(RET_HEADS):
            cols = slice(h * RET_HEAD_DIM, (h + 1) * RET_HEAD_DIM)
            dq = jnp.concatenate([dq_ref[h], dq_ref[h]], axis=-1)
            dk = jnp.concatenate([dk_ref[h], dk_ref[h]], axis=-1)
            q = _rotate(q_ref[rows, cols], cos, sin)
            k = _rotate(k_ref[rows, cols], cos, sin) * (RET_HEAD_DIM ** -0.5)
            vb = v_ref[rows, cols].astype(BF16)
            s = s_ref[h]
            scores = _dot_t_rhs(q.astype(BF16), k.astype(BF16)) * din_ref[h]
            o = jnp.dot(scores.astype(BF16), vb, preferred_element_type=F32)
            o = o + jnp.dot((q * dq).astype(BF16), s.astype(BF16), preferred_element_type=F32)
            s_ref[h] = s * dc_ref[h] + _dot_t_lhs((k * dk).astype(BF16), vb)
            o_ref[rows, cols] = _head_norm_gate(o, g_ref[rows, cols], rnw_ref[:, cols]).astype(o_ref.dtype)

    @pl.when(pl.program_id(1) == pl.num_programs(1) - 1)
    def _():
        s_out_ref[...] = s_ref[...]


def retention_prompt(z, cos, sin, din, dq, dk, dc, rnw, layer, state_prev):
    nt = SEQ // RET_P_ROWS
    extra = _carried(state_prev)
    zspec = lambda part: pl.BlockSpec((RET_P_ROWS, D_RET), lambda b, t: (b * nt + t, part))
    whole = lambda a: pl.BlockSpec(a.shape, lambda b, t: (0,) * a.ndim)
    state_block = (RET_HEADS, RET_HEAD_DIM, RET_HEAD_DIM)
    return pl.pallas_call(
        _skip_refs(_ret_prompt_kernel, 11, len(extra)),
        out_shape=(jax.ShapeDtypeStruct((M_ROWS, D_RET), BF16),
                   jax.ShapeDtypeStruct((DEPTH, BATCH) + state_block, F32)),
        grid=(BATCH, nt),
        in_specs=[
            zspec(0), zspec(1), zspec(2), zspec(3),
            pl.BlockSpec((RET_P_ROWS, RET_HEAD_DIM // 2), lambda b, t: (t, 0)),
            pl.BlockSpec((RET_P_ROWS, RET_HEAD_DIM // 2), lambda b, t: (t, 0)),
            whole(din), whole(dq), whole(dk), whole(dc),
            pl.BlockSpec((None, 1, D_RET), lambda b, t: (layer, 0, 0)),
        ] + [_ANY] * len(extra),
        out_specs=(
            pl.BlockSpec((RET_P_ROWS, D_RET), lambda b, t: (b * nt + t, 0)),
            pl.BlockSpec((None, None) + state_block, lambda b, t: (layer, b, 0, 0, 0)),
        ),
        scratch_shapes=[pltpu.VMEM(state_block, F32)],
        input_output_aliases={11: 1} if extra else {},
        compiler_params=_cparams(("parallel", "arbitrary")),
        name="retention_prompt",
    )(z, z, z, z, cos, sin, din, dq, dk, dc, rnw, *extra)


def _ret_sample_kernel(dec_ref, z_ref, cos_ref, sin_ref, st_ref, rnw_ref, o_ref, st_out_ref):
    h = pl.program_id(1)
    c0 = pl.multiple_of(h * RET_HEAD_DIM, RET_HEAD_DIM)
    col = lambda part: pl.ds(pl.multiple_of(part * D_RET + c0, RET_HEAD_DIM), RET_HEAD_DIM)
    cos, sin = cos_ref[...], sin_ref[...]
    q = _rotate(z_ref[:, col(0)], cos, sin)
    k = _rotate(z_ref[:, col(1)], cos, sin) * (RET_HEAD_DIM ** -0.5)
    v = z_ref[:, col(2)]
    g = z_ref[:, col(3)]
    slab = lambda a, t: a[t * S_TILE_B:(t + 1) * S_TILE_B, :]

    intra = []
    for t in range(DEC_SEQ):
        acc = None
        for j in range(t + 1):
            w = jnp.sum(slab(q, t) * slab(k, j), axis=-1, keepdims=True) * dec_ref[h, t * DEC_SEQ + j]
            acc = w * slab(v, j) if acc is None else acc + w * slab(v, j)
        intra.append(acc)
    o = jnp.concatenate(intra, axis=0)

    dq_rows = jnp.concatenate(
        [jnp.full((S_TILE_B, 1), 1.0, F32) * dec_ref[h, 16 + t] for t in range(DEC_SEQ)], axis=0)
    dk_rows = jnp.concatenate(
        [jnp.full((S_TILE_B, 1), 1.0, F32) * dec_ref[h, 20 + t] for t in range(DEC_SEQ)], axis=0)
    qd = (q * dq_rows).astype(BF16)
    kd = k * dk_rows
    vb = v.astype(BF16)
    dchunk = dec_ref[h, 24]
    row_b = lax.broadcasted_iota(jnp.int32, (S_TILE_ROWS, 1), 0) % S_TILE_B
    for b in range(S_TILE_B):
        mine = row_b == b
        s = st_ref[b]
        o = o + jnp.where(mine, jnp.dot(qd, s.astype(BF16), preferred_element_type=F32), 0.0)
        st_out_ref[b] = s * dchunk + _dot_t_lhs(jnp.where(mine, kd, 0.0).astype(BF16), vb)
    o_ref[...] = _head_norm_gate(o, g, rnw_ref[...]).astype(o_ref.dtype)


def retention_sample(dec, z, cos_s, sin_s, state_ret, rnw, mix, layer, state_prev):
    extra = [mix] + _carried(state_prev)
    return pl.pallas_call(
        _skip_refs(_ret_sample_kernel, 6, len(extra)),
        out_shape=(jax.ShapeDtypeStruct(mix.shape, mix.dtype),
                   jax.ShapeDtypeStruct(state_ret.shape, F32)),
        grid=(N_S_TILES, RET_HEADS),
        in_specs=[
            pl.BlockSpec(memory_space=pltpu.SMEM),
            pl.BlockSpec((S_TILE_ROWS, D_IN), lambda i, h: (S_BLK0 + i, 0)),
            pl.BlockSpec((S_TILE_ROWS, RET_HEAD_DIM // 2), lambda i, h: (0, 0)),
            pl.BlockSpec((S_TILE_ROWS, RET_HEAD_DIM // 2), lambda i, h: (0, 0)),
            pl.BlockSpec((None, S_TILE_B, None, RET_HEAD_DIM, RET_HEAD_DIM),
                         lambda i, h: (layer, i, h, 0, 0)),
            pl.BlockSpec((None, 1, RET_HEAD_DIM), lambda i, h: (layer, 0, h)),
        ] + [_ANY] * len(extra),
        out_specs=(
            pl.BlockSpec((S_TILE_ROWS, RET_HEAD_DIM), lambda i, h: (S_BLK0 + i, h)),
            pl.BlockSpec((None, S_TILE_B, None, RET_HEAD_DIM, RET_HEAD_DIM),
                         lambda i, h: (layer, i, h, 0, 0)),
        ),
        input_output_aliases={6 + n: n for n in range(len(extra))},
        compiler_params=_cparams(("parallel", "arbitrary")),
        name="retention_sample",
    )(dec, z, cos_s, sin_s, state_ret, rnw, *extra)


POOL_P_ROWS = 512
POOL_HALO = 16


def _shift_rows(a, s):
    return pltpu.roll(a, s, axis=0)


def _pool_prompt_kernel(u_ref, pw_ref, ps_ref, o_ref, buf_ref, halo_ref):
    t = pl.program_id(1)

    @pl.when(t == 0)
    def _():
        halo_ref[...] = jnp.zeros_like(halo_ref)

    pos = t * POOL_P_ROWS + lax.broadcasted_iota(jnp.int32, (POOL_P_ROWS, 1), 0)
    for g, w in enumerate(POOL_WINDOWS):
        cols = slice(g * POOL_GROUP_DIM, (g + 1) * POOL_GROUP_DIM)
        u = u_ref[:, cols]
        a = jnp.concatenate([halo_ref[:, cols], u], axis=0)
        s = 1
        while s < w:
            a = a + _shift_rows(a, s)
            s *= 2
        cnt = jnp.minimum(pos + 1, w).astype(F32)
        pooled = a[POOL_HALO:, :] / cnt - u
        pm = jnp.dot(pooled.astype(BF16), pw_ref[g].astype(BF16), preferred_element_type=F32)
        o_ref[:, cols] = (pm * ps_ref[:, cols]).astype(o_ref.dtype)

    halo_ref[...] = u_ref[POOL_P_ROWS - POOL_HALO:, :]

    @pl.when(t == pl.num_programs(1) - 1)
    def _():
        buf_ref[...] = u_ref[POOL_P_ROWS - POOL_BUF:, :]


def pool_prompt(z, pool_w, pool_scale, layer, buf_prev):
    nt = SEQ // POOL_P_ROWS
    extra = _carried(buf_prev)
    return pl.pallas_call(
        _skip_refs(_pool_prompt_kernel, 3, len(extra)),
        out_shape=(jax.ShapeDtypeStruct((M_ROWS, D_POOL), BF16),
                   jax.ShapeDtypeStruct((DEPTH, BATCH, POOL_BUF, D_POOL), F32)),
        grid=(BATCH, nt),
        in_specs=[
            pl.BlockSpec((POOL_P_ROWS, D_POOL), lambda b, t: (b * nt + t, 4)),
            pl.BlockSpec((None, 4, POOL_GROUP_DIM, POOL_GROUP_DIM), lambda b, t: (layer, 0, 0, 0)),
            pl.BlockSpec((None, 1, D_POOL), lambda b, t: (layer, 0, 0)),
        ] + [_ANY] * len(extra),
        out_specs=(
            pl.BlockSpec((POOL_P_ROWS, D_POOL), lambda b, t: (b * nt + t, 0)),
            pl.BlockSpec((None, None, POOL_BUF, D_POOL), lambda b, t: (layer, b, 0, 0)),
        ),
        scratch_shapes=[pltpu.VMEM((POOL_HALO, D_POOL), F32)],
        input_output_aliases={3: 1} if extra else {},
        compiler_params=_cparams(("parallel", "arbitrary")),
        name="pool_prompt",
    )(z, pool_w, pool_scale, *extra)


def _pool_sample_kernel(u_ref, buf_ref, pw_ref, ps_ref, o_ref, nbuf_ref):

    def ext(r, cols):
        if r < POOL_BUF:
            return buf_ref[:, r, cols]
        return u_ref[(r - POOL_BUF) * S_TILE_B:(r - POOL_BUF + 1) * S_TILE_B, cols]

    for g, w in enumerate(POOL_WINDOWS):
        cols = slice(g * POOL_GROUP_DIM, (g + 1) * POOL_GROUP_DIM)
        pooled = []
        for t in range(DEC_SEQ):
            win = ext(POOL_BUF + t, cols)
            for r in range(POOL_BUF + t - w + 1, POOL_BUF + t):
                win = win + ext(r, cols)
            cnt = float(min(t + PAST_LEN + 1, w))
            pooled.append(win / cnt - ext(POOL_BUF + t, cols))
        pooled = jnp.concatenate(pooled, axis=0)
        pm = jnp.dot(pooled.astype(BF16), pw_ref[g].astype(BF16), preferred_element_type=F32)
        o_ref[:, cols] = (pm * ps_ref[:, cols]).astype(o_ref.dtype)

    full = slice(0, D_POOL)
    for r in range(POOL_BUF):
        nbuf_ref[:, r, :] = ext(r + DEC_SEQ, full)


def pool_sample(z, state_pool, pool_w, pool_scale, mix, layer, buf_prev):
    extra = [mix] + _carried(buf_prev)
    return pl.pallas_call(
        _skip_refs(_pool_sample_kernel, 4, len(extra)),
        out_shape=(jax.ShapeDtypeStruct(mix.shape, mix.dtype),
                   jax.ShapeDtypeStruct(state_pool.shape, F32)),
        grid=(N_S_TILES,),
        in_specs=[
            pl.BlockSpec((S_TILE_ROWS, D_POOL), lambda i: (S_BLK0 + i, 4)),
            pl.BlockSpec((None, S_TILE_B, POOL_BUF, D_POOL), lambda i: (layer, i, 0, 0)),
            pl.BlockSpec((None, 4, POOL_GROUP_DIM, POOL_GROUP_DIM), lambda i: (layer, 0, 0, 0)),
            pl.BlockSpec((None, 1, D_POOL), lambda i: (layer, 0, 0)),
        ] + [_ANY] * len(extra),
        out_specs=(
            pl.BlockSpec((S_TILE_ROWS, D_POOL), lambda i: (S_BLK0 + i, 0)),
            pl.BlockSpec((None, S_TILE_B, POOL_BUF, D_POOL), lambda i: (layer, i, 0, 0)),
        ),
        input_output_aliases={4 + n: n for n in range(len(extra))},
        compiler_params=_cparams(("arbitrary",)),
        name="pool_sample",
    )(z, state_pool, pool_w, pool_scale, *extra)


def _softmax_rows(s):
    m = jnp.max(s, axis=-1, keepdims=True)
    e = jnp.exp(s - m)
    return e / jnp.sum(e, axis=-1, keepdims=True)


XATTN_P_ROWS = 1024


def _xattn_prompt_kernel(q_ref, mk_ref, mv_ref, o_ref):
    for h in range(MEM_HEADS):
        cols = slice(h * MEM_HEAD_DIM, (h + 1) * MEM_HEAD_DIM)
        s = _dot_t_rhs(q_ref[:, cols], mk_ref[:, cols].astype(BF16)) * (MEM_HEAD_DIM ** -0.5)
        p = _softmax_rows(s)
        o_ref[:, cols] = jnp.dot(p.astype(BF16), mv_ref[:, cols].astype(BF16),
                                 preferred_element_type=F32).astype(o_ref.dtype)


def xattn_prompt(qx, mk, mv, layer):
    nt = SEQ // XATTN_P_ROWS
    return pl.pallas_call(
        _xattn_prompt_kernel,
        out_shape=jax.ShapeDtypeStruct((M_ROWS, D_MODEL), BF16),
        grid=(BATCH, nt),
        in_specs=[
            pl.BlockSpec((XATTN_P_ROWS, D_MODEL), lambda b, t: (b * nt + t, 0)),
            pl.BlockSpec((None, N_MEM, D_MODEL), lambda b, t: (layer, b, 0)),
            pl.BlockSpec((None, N_MEM, D_MODEL), lambda b, t: (layer, b, 0)),
        ],
        out_specs=pl.BlockSpec((XATTN_P_ROWS, D_MODEL), lambda b, t: (b * nt + t, 0)),
        compiler_params=_cparams(("parallel", "arbitrary")),
        name="xattn_prompt",
    )(qx, mk, mv)


XATTN_S_B = 4
XATTN_S_ROWS = MEM_HEADS * DEC_SEQ
XATTN_S_KEYS = N_MEM * MEM_HEADS


def _xattn_sample_kernel(q_ref, k_ref, v_ref, o_ref):
    row_h = lax.broadcasted_iota(jnp.int32, (XATTN_S_ROWS, XATTN_S_KEYS), 0) // DEC_SEQ
    col_h = lax.broadcasted_iota(jnp.int32, (XATTN_S_ROWS, XATTN_S_KEYS), 1) % MEM_HEADS
    same_head = row_h == col_h
    for bb in range(XATTN_S_B):
        k2 = k_ref[bb].reshape(XATTN_S_KEYS, MEM_HEAD_DIM).astype(BF16)
        v2 = v_ref[bb].reshape(XATTN_S_KEYS, MEM_HEAD_DIM).astype(BF16)
        s = _dot_t_rhs(q_ref[bb], k2) * (MEM_HEAD_DIM ** -0.5)
        s = jnp.where(same_head, s, -jnp.inf)
        p = _softmax_rows(s)
        o_ref[bb] = jnp.dot(p.astype(BF16), v2, preferred_element_type=F32).astype(o_ref.dtype)


def xattn_sample(qs, cache_k, cache_v, layer):
    kv_spec = pl.BlockSpec((None, XATTN_S_B, N_MEM, MEM_HEADS, MEM_HEAD_DIM),
                           lambda i: (layer, i, 0, 0, 0))
    qo_spec = pl.BlockSpec((XATTN_S_B, XATTN_S_ROWS, MEM_HEAD_DIM), lambda i: (i, 0, 0))
    return pl.pallas_call(
        _xattn_sample_kernel,
        out_shape=jax.ShapeDtypeStruct(qs.shape, BF16),
        grid=(DEC_BATCH // XATTN_S_B,),
        in_specs=[qo_spec, kv_spec, kv_spec],
        out_specs=qo_spec,
        compiler_params=_cparams(("parallel",)),
        name="xattn_sample",
    )(qs, cache_k, cache_v)


def _sample_rows_to_heads(a):
    a = a.reshape(N_S_TILES, DEC_SEQ, S_TILE_B, MEM_HEADS, MEM_HEAD_DIM)
    return a.transpose(0, 2, 3, 1, 4).reshape(DEC_BATCH, XATTN_S_ROWS, MEM_HEAD_DIM)


def _heads_to_sample_rows(a):
    a = a.reshape(N_S_TILES, S_TILE_B, MEM_HEADS, DEC_SEQ, MEM_HEAD_DIM)
    return a.transpose(0, 3, 1, 2, 4).reshape(S_ROWS, D_MODEL)


FINAL_ROWS = 512


def _final_norm_kernel(x_ref, nw_ref, yp_ref, ys_ref):
    def run(dst_ref):
        def body(rows):
            dst_ref[rows, :] = _rms_rows(x_ref[rows, :], nw_ref[...])
        _for_row_chunks(FINAL_ROWS, _norm_chunk(FINAL_ROWS), body)

    is_prompt = pl.program_id(0) < P_ROWS // FINAL_ROWS
    pl.when(is_prompt)(lambda: run(yp_ref))
    pl.when(jnp.logical_not(is_prompt))(lambda: run(ys_ref))


def final_norm(x, nw):
    n_p = P_ROWS // FINAL_ROWS
    return pl.pallas_call(
        _final_norm_kernel,
        out_shape=(jax.ShapeDtypeStruct((P_ROWS, D_MODEL), F32),
                   jax.ShapeDtypeStruct((S_ROWS, D_MODEL), F32)),
        grid=(M_ROWS // FINAL_ROWS,),
        in_specs=[
            pl.BlockSpec((FINAL_ROWS, D_MODEL), lambda i: (i, 0)),
            pl.BlockSpec((1, D_MODEL), lambda i: (0, 0)),
        ],
        out_specs=(
            pl.BlockSpec((FINAL_ROWS, D_MODEL), lambda i: (jnp.minimum(i, n_p - 1), 0)),
            pl.BlockSpec((S_ROWS, D_MODEL), lambda i: (0, 0)),
        ),
        compiler_params=_cparams(("arbitrary",)),
        name="final_norm",
    )(x, nw)


def _rope_tables(pos):
    half = RET_HEAD_DIM // 2
    inv = ROPE_BASE ** (-jnp.arange(half, dtype=F32) / half)
    ang = pos.astype(F32)[:, None] * inv[None, :]
    return jnp.cos(ang), jnp.sin(ang)


def _log_gamma():
    return jnp.log1p(-jnp.exp2(-5.0 - jnp.arange(RET_HEADS, dtype=F32)))


def _decay_tables(chunk):
    lg = _log_gamma()
    idx = jnp.arange(chunk, dtype=F32)
    diff = idx[:, None] - idx[None, :]
    decay_in = jnp.where(diff[None] >= 0.0,
                         jnp.exp(lg[:, None, None] * jnp.maximum(diff, 0.0)[None]), 0.0)
    decay_q = jnp.exp(lg[:, None] * (idx[None, :] + 1.0))
    decay_k = jnp.exp(lg[:, None] * (chunk - 1.0 - idx[None, :]))
    decay_chunk = jnp.exp(lg * chunk)
    return decay_in, decay_q, decay_k, decay_chunk


def _to_sample_rows(a):
    d = a.shape[-1]
    return a.reshape(N_S_TILES, S_TILE_B, DEC_SEQ, d).transpose(0, 2, 1, 3).reshape(S_ROWS, d)


def _from_sample_rows(a):
    d = a.shape[-1]
    return a.reshape(N_S_TILES, DEC_SEQ, S_TILE_B, d).transpose(0, 2, 1, 3).reshape(DEC_BATCH, DEC_SEQ, d)


def kernel(x_prompt, x_sample, mem_prompt, state_ret, state_pool, cache_mem_k, cache_mem_v,
           attn_norm_w, w_in, ret_norm_w, pool_w, pool_scale, w_out, xattn_norm_w, mem_norm_w,
           w_xq, w_mk, w_mv, w_xo, mlp_norm_w, w_up, w_down, final_norm_w):
    cos_p, sin_p = _rope_tables(jnp.arange(SEQ))
    cos_s, sin_s = _rope_tables(jnp.arange(DEC_SEQ) + PAST_LEN)
    cos_s = jnp.repeat(cos_s, S_TILE_B, axis=0)
    sin_s = jnp.repeat(sin_s, S_TILE_B, axis=0)
    din_p, dq_p, dk_p, dc_p = _decay_tables(RET_CHUNK)
    half = RET_HEAD_DIM // 2
    dq_p = jnp.broadcast_to(dq_p[:, :, None], (RET_HEADS, RET_CHUNK, half))
    dk_p = jnp.broadcast_to(dk_p[:, :, None], (RET_HEADS, RET_CHUNK, half))
    dc_p = jnp.broadcast_to(dc_p[:, None, None], (RET_HEADS, 1, RET_HEAD_DIM))
    din_s, dq_s, dk_s, dc_s = _decay_tables(DEC_SEQ)
    dec_s = jnp.concatenate([din_s.reshape(RET_HEADS, DEC_SEQ * DEC_SEQ), dq_s, dk_s,
                             dc_s[:, None], jnp.zeros((RET_HEADS, 7), F32)], axis=1)

    row3 = lambda a: a.reshape(DEPTH, 1, a.shape[-1])
    attn_nw, xattn_nw, mem_nw, mlp_nw = map(row3, (attn_norm_w, xattn_norm_w, mem_norm_w, mlp_norm_w))
    ret_nw, pool_sc = row3(ret_norm_w), row3(pool_scale)
    w_in_b, w_up_b, w_down_b = w_in.astype(BF16), w_up.astype(BF16), w_down.astype(BF16)

    x = jnp.concatenate([x_prompt.reshape(P_ROWS, D_MODEL), _to_sample_rows(x_sample)], axis=0)
    mem = mem_prompt.reshape(BATCH * N_MEM, D_MODEL)

    ret_p = buf_p = mk_p = mv_p = ret_s = buf_s = None
    for l in range(DEPTH):
        z = norm_matmul(x, attn_nw, w_in_b, l, tm=TM, tn=1024, out_dtype=F32)
        mix_r, ret_p = retention_prompt(z, cos_p, sin_p, din_p, dq_p, dk_p, dc_p, ret_nw, l, ret_p)
        mix_r, ret_s = retention_sample(dec_s, z, cos_s, sin_s, state_ret, ret_nw, mix_r, l, ret_s)
        mix_p, buf_p = pool_prompt(z, pool_w, pool_sc, l, buf_p)
        mix_p, buf_s = pool_sample(z, state_pool, pool_w, pool_sc, mix_p, l, buf_s)
        x, h = matmul_residual_resident([mix_r, mix_p], w_out, x, xattn_nw, l, tm=TM_RESIDENT)

        mk_p = norm_matmul_stacked(mem, mem_nw, w_mk, l, mk_p, tm=1024, tn=512)
        mv_p = norm_matmul_stacked(mem, mem_nw, w_mv, l, mv_p, tm=1024, tn=512)
        qx = matmul_resident(h, w_xq, l, tm=TM)
        att = xattn_prompt(qx, mk_p, mv_p, l)
        att_s = xattn_sample(_sample_rows_to_heads(qx[P_ROWS:]), cache_mem_k, cache_mem_v, l)
        att = lax.dynamic_update_slice(att, _heads_to_sample_rows(att_s), (P_ROWS, 0))
        x, h = matmul_residual_resident([att], w_xo, x, mlp_nw, l, tm=TM_RESIDENT)
        x = mlp_residual(h, x, w_up_b, w_down_b, l, tm=TM)

    y_p, y_s = final_norm(x, final_norm_w.reshape(1, D_MODEL))
    mem_shape = (DEPTH, BATCH, N_MEM, MEM_HEADS, MEM_HEAD_DIM)
    return (y_p.reshape(BATCH, SEQ, D_MODEL), _from_sample_rows(y_s),
            ret_p, buf_p, mk_p.reshape(mem_shape), mv_p.reshape(mem_shape), ret_s, buf_s)
```

```python
import functools

import jax
import jax.numpy as jnp
from jax import lax
from jax.experimental import pallas as pl
from jax.experimental.pallas import tpu as pltpu

D_MODEL = 2048
BATCH = 4
SEQ = 2048
DEPTH = 2
DEC_BATCH = 128
DEC_SEQ = 4
PAST_LEN = 16384
D_RET = 1024
RET_HEADS = 4
RET_HEAD_DIM = 256
D_POOL = 1024
POOL_WINDOWS = (2, 4, 8, 16)
POOL_GROUP_DIM = 256
POOL_BUF = 15
D_IN = 5120
N_MEM = 256
MEM_HEADS = 4
MEM_HEAD_DIM = 512
D_FF = 8192
RET_CHUNK = 128
ROPE_BASE = 10000.0
EPS = 1e-6

F32 = jnp.float32
BF16 = jnp.bfloat16

P_ROWS = BATCH * SEQ
S_ROWS = DEC_BATCH * DEC_SEQ
M_ROWS = P_ROWS + S_ROWS
S_TILE_B = 8
S_TILE_ROWS = S_TILE_B * DEC_SEQ
N_S_TILES = DEC_BATCH // S_TILE_B
S_BLK0 = P_ROWS // S_TILE_ROWS

TM = 1088
TM_RESIDENT = 544
VMEM_LIMIT = 58 * 1024 * 1024


def _cparams(sem):
    return pltpu.CompilerParams(dimension_semantics=sem, vmem_limit_bytes=VMEM_LIMIT)


_ANY = pl.BlockSpec(memory_space=pl.ANY)


def _skip_refs(kernel_fn, start, count):
    def wrapped(*refs):
        return kernel_fn(*refs[:start], *refs[start + count:])
    return wrapped


def _carried(prev):
    return [] if prev is None else [prev]


def _rms_rows(x, nw):
    ms = jnp.mean(x * x, axis=-1, keepdims=True)
    return x * lax.rsqrt(ms + EPS) * nw


def _for_row_chunks(n_rows, chunk, body):
    def step(c, carry):
        body(pl.ds(pl.multiple_of(c * chunk, chunk), chunk))
        return carry
    lax.fori_loop(0, n_rows // chunk, step, 0)


def _norm_chunk(n_rows):
    return 272 if n_rows % 272 == 0 else 128


def _cast_weight_once(w_ref, wb_ref):
    @pl.when(pl.program_id(0) == 0)
    def _():
        def body(rows):
            wb_ref[rows, :] = w_ref[rows, :].astype(BF16)
        _for_row_chunks(w_ref.shape[0], 256, body)


def _norm_matmul_kernel(x_ref, nw_ref, w_ref, o_ref, h_ref, *, tm):
    @pl.when(pl.program_id(1) == 0)
    def _():
        def body(rows):
            h_ref[rows, :] = _rms_rows(x_ref[rows, :], nw_ref[...]).astype(BF16)
        _for_row_chunks(tm, _norm_chunk(tm), body)

    o_ref[...] = jnp.dot(h_ref[...], w_ref[...].astype(BF16),
                         preferred_element_type=F32).astype(o_ref.dtype)


def _norm_matmul_resident_kernel(x_ref, nw_ref, w_ref, o_ref):
    h = _rms_rows(x_ref[...], nw_ref[...]).astype(BF16)
    o_ref[...] = jnp.dot(h, w_ref[...], preferred_element_type=F32).astype(o_ref.dtype)


def norm_matmul_resident(x, nw, w, layer, *, tm, out_dtype):
    m, k = x.shape
    n = w.shape[-1]
    return pl.pallas_call(
        _norm_matmul_resident_kernel,
        out_shape=jax.ShapeDtypeStruct((m, n), out_dtype),
        grid=(m // tm,),
        in_specs=[
            pl.BlockSpec((tm, k), lambda i: (i, 0)),
            pl.BlockSpec((None, 1, k), lambda i: (layer, 0, 0)),
            pl.BlockSpec((None, k, n), lambda i: (layer, 0, 0), pipeline_mode=pl.Buffered(1)),
        ],
        out_specs=pl.BlockSpec((tm, n), lambda i: (i, 0)),
        compiler_params=_cparams(("parallel",)),
        name="norm_matmul_resident",
    )(x, nw, w)


def norm_matmul_stacked(x, nw, w, layer, prev, *, tm, tn):
    m, k = x.shape
    n = w.shape[-1]
    extra = _carried(prev)
    return pl.pallas_call(
        _skip_refs(functools.partial(_norm_matmul_kernel, tm=tm), 3, len(extra)),
        out_shape=jax.ShapeDtypeStruct((DEPTH, m, n), F32),
        grid=(m // tm, n // tn),
        in_specs=[
            pl.BlockSpec((tm, k), lambda i, j: (i, 0)),
            pl.BlockSpec((None, 1, k), lambda i, j: (layer, 0, 0)),
            pl.BlockSpec((None, k, tn), lambda i, j: (layer, 0, j)),
        ] + [_ANY] * len(extra),
        out_specs=pl.BlockSpec((None, tm, tn), lambda i, j: (layer, i, j)),
        scratch_shapes=[pltpu.VMEM((tm, k), BF16)],
        input_output_aliases={3: 0} if extra else {},
        compiler_params=_cparams(("parallel", "arbitrary")),
        name="norm_matmul_stacked",
    )(x, nw, w, *extra)


def _resident_weight_spec(w, layer):
    return pl.BlockSpec((None,) + w.shape[1:], lambda i: (layer, 0, 0), pipeline_mode=pl.Buffered(1))


def _matmul_resident_kernel(a_ref, w_ref, o_ref, wb_ref):
    _cast_weight_once(w_ref, wb_ref)
    o_ref[...] = jnp.dot(a_ref[...], wb_ref[...], preferred_element_type=F32).astype(o_ref.dtype)


def matmul_resident(a, w, layer, *, tm):
    m, k = a.shape
    n = w.shape[-1]
    return pl.pallas_call(
        _matmul_resident_kernel,
        out_shape=jax.ShapeDtypeStruct((m, n), BF16),
        grid=(m // tm,),
        in_specs=[pl.BlockSpec((tm, k), lambda i: (i, 0)), _resident_weight_spec(w, layer)],
        out_specs=pl.BlockSpec((tm, n), lambda i: (i, 0)),
        scratch_shapes=[pltpu.VMEM((k, n), BF16)],
        compiler_params=_cparams(("arbitrary",)),
        name="matmul_resident",
    )(a, w)


def _matmul_res_resident_kernel(*refs, n_parts):
    a_refs = refs[:n_parts]
    w_ref, r_ref, nw_ref, o_ref, h_ref, wb_ref = refs[n_parts:]
    _cast_weight_once(w_ref, wb_ref)
    acc = r_ref[...]
    k0 = 0
    for a_ref in a_refs:
        k1 = k0 + a_ref.shape[1]
        acc = acc + jnp.dot(a_ref[...], wb_ref[k0:k1, :], preferred_element_type=F32)
        k0 = k1
    o_ref[...] = acc
    h_ref[...] = _rms_rows(acc, nw_ref[...]).astype(BF16)


def matmul_residual_resident(a_parts, w, res, nw, layer, *, tm):
    m, n = res.shape
    k = sum(a.shape[1] for a in a_parts)
    row_tile = lambda width: pl.BlockSpec((tm, width), lambda i: (i, 0))
    return pl.pallas_call(
        functools.partial(_matmul_res_resident_kernel, n_parts=len(a_parts)),
        out_shape=(jax.ShapeDtypeStruct((m, n), F32), jax.ShapeDtypeStruct((m, n), BF16)),
        grid=(m // tm,),
        in_specs=[row_tile(a.shape[1]) for a in a_parts] + [
            _resident_weight_spec(w, layer),
            row_tile(n),
            pl.BlockSpec((None, 1, n), lambda i: (layer, 0, 0)),
        ],
        out_specs=(row_tile(n), row_tile(n)),
        scratch_shapes=[pltpu.VMEM((k, n), BF16)],
        compiler_params=_cparams(("arbitrary",)),
        name="matmul_residual_resident",
    )(*a_parts, w, res, nw)


MLP_TF = 1024
MLP_TN = 512
MLP_KC = 4096
MLP_UP_STEPS = D_FF // MLP_TF
MLP_K_STEPS = D_FF // MLP_KC
MLP_DOWN_STEPS = (D_MODEL // MLP_TN) * MLP_K_STEPS


def _mlp_kernel(h_ref, x_ref, wu_ref, wd_ref, o_ref, hid_ref):
    s = pl.program_id(1)

    @pl.when(s < MLP_UP_STEPS)
    def _():
        hid = jnp.dot(h_ref[...], wu_ref[...], preferred_element_type=F32)
        per_chunk = MLP_KC // MLP_TF
        c0 = pl.multiple_of((s % per_chunk) * MLP_TF, MLP_TF)
        hid_ref[s // per_chunk, :, pl.ds(c0, MLP_TF)] = jnp.square(jnp.maximum(hid, 0.0)).astype(BF16)

    @pl.when(s >= MLP_UP_STEPS)
    def _():
        kc = (s - MLP_UP_STEPS) % MLP_K_STEPS
        part = jnp.dot(hid_ref[kc], wd_ref[...], preferred_element_type=F32)

        @pl.when(kc == 0)
        def _():
            o_ref[...] = x_ref[...] + part

        @pl.when(kc != 0)
        def _():
            o_ref[...] += part


def mlp_residual(h, x, w_up, w_down, layer, *, tm):
    m, k = x.shape
    down = lambda s: jnp.maximum(s - MLP_UP_STEPS, 0)
    out_tile = pl.BlockSpec((tm, MLP_TN), lambda i, s: (i, down(s) // MLP_K_STEPS))
    return pl.pallas_call(
        _mlp_kernel,
        out_shape=jax.ShapeDtypeStruct((m, k), F32),
        grid=(m // tm, MLP_UP_STEPS + MLP_DOWN_STEPS),
        in_specs=[
            pl.BlockSpec((tm, k), lambda i, s: (i, 0)),
            out_tile,
            pl.BlockSpec((None, k, MLP_TF), lambda i, s: (layer, 0, jnp.minimum(s, MLP_UP_STEPS - 1))),
            pl.BlockSpec((None, MLP_KC, MLP_TN),
                         lambda i, s: (layer, down(s) % MLP_K_STEPS, down(s) // MLP_K_STEPS)),
        ],
        out_specs=out_tile,
        scratch_shapes=[pltpu.VMEM((MLP_K_STEPS, tm, MLP_KC), BF16)],
        compiler_params=_cparams(("parallel", "arbitrary")),
        name="mlp_residual",
    )(h, x, w_up, w_down)


def _rotate(x, cos, sin):
    half = x.shape[-1] // 2
    x1, x2 = x[:, :half], x[:, half:]
    return jnp.concatenate([x1 * cos - x2 * sin, x2 * cos + x1 * sin], axis=-1)


def _head_norm_gate(o, g, rnw):
    o = o * lax.rsqrt(jnp.mean(o * o, axis=-1, keepdims=True) + EPS)
    return (g * jax.nn.sigmoid(g)) * (o * rnw)


def _dot_t_lhs(a, b):
    return lax.dot_general(a, b, (((0,), (0,)), ((), ())), preferred_element_type=F32)


def _dot_t_rhs(a, b):
    return lax.dot_general(a, b, (((1,), (1,)), ((), ())), preferred_element_type=F32)


RET_P_ROWS = 512


def _ret_prompt_kernel(q_ref, k_ref, v_ref, g_ref, cos_ref, sin_ref, din_ref, dq_ref, dk_ref,
                       dc_ref, rnw_ref, o_ref, s_out_ref, s_ref):
    @pl.when(pl.program_id(1) == 0)
    def _():
        s_ref[...] = jnp.zeros_like(s_ref)

    for c in range(RET_P_ROWS // RET_CHUNK):
        rows = slice(c * RET_CHUNK, (c + 1) * RET_CHUNK)
        cos, sin = cos_ref[rows, :], sin_ref[rows, :]
        for h in range(RET_HEADS):
            cols = slice(h * RET_HEAD_DIM, (h + 1) * RET_HEAD_DIM)
            dq = jnp.concatenate([dq_ref[h], dq_ref[h]], axis=-1)
            dk = jnp.concatenate([dk_ref[h], dk_ref[h]], axis=-1)
            q = _rotate(q_ref[rows, cols], cos, sin)
            k = _rotate(k_ref[rows, cols], cos, sin) * (RET_HEAD_DIM ** -0.5)
            vb = v_ref[rows, cols].astype(BF16)
            s = s_ref[h]
            scores = _dot_t_rhs(q.astype(BF16), k.astype(BF16)) * din_ref[h]
            o = jnp.dot(scores.astype(BF16), vb, preferred_element_type=F32)
            o = o + jnp.dot((q * dq).astype(BF16), s.astype(BF16), preferred_element_type=F32)
            s_ref[h] = s * dc_ref[h] + _dot_t_lhs((k * dk).astype(BF16), vb)
            o_ref[rows, cols] = _head_norm_gate(o, g_ref[rows, cols], rnw_ref[:, cols]).astype(o_ref.dtype)

    @pl.when(pl.program_id(1) == pl.num_programs(1) - 1)
    def _():
        s_out_ref[...] = s_ref[...]


def retention_prompt(z, cos, sin, din, dq, dk, dc, rnw, layer, state_prev):
    nt = SEQ // RET_P_ROWS
    extra = _carried(state_prev)
    zspec = lambda part: pl.BlockSpec((RET_P_ROWS, D_RET), lambda b, t: (b * nt + t, part))
    whole = lambda a: pl.BlockSpec(a.shape, lambda b, t: (0,) * a.ndim)
    state_block = (RET_HEADS, RET_HEAD_DIM, RET_HEAD_DIM)
    return pl.pallas_call(
        _skip_refs(_ret_prompt_kernel, 11, len(extra)),
        out_shape=(jax.ShapeDtypeStruct((M_ROWS, D_RET), BF16),
                   jax.ShapeDtypeStruct((DEPTH, BATCH) + state_block, F32)),
        grid=(BATCH, nt),
        in_specs=[
            zspec(0), zspec(1), zspec(2), zspec(3),
            pl.BlockSpec((RET_P_ROWS, RET_HEAD_DIM // 2), lambda b, t: (t, 0)),
            pl.BlockSpec((RET_P_ROWS, RET_HEAD_DIM // 2), lambda b, t: (t, 0)),
            whole(din), whole(dq), whole(dk), whole(dc),
            pl.BlockSpec((None, 1, D_RET), lambda b, t: (layer, 0, 0)),
        ] + [_ANY] * len(extra),
        out_specs=(
            pl.BlockSpec((RET_P_ROWS, D_RET), lambda b, t: (b * nt + t, 0)),
            pl.BlockSpec((None, None) + state_block, lambda b, t: (layer, b, 0, 0, 0)),
        ),
        scratch_shapes=[pltpu.VMEM(state_block, F32)],
        input_output_aliases={11: 1} if extra else {},
        compiler_params=_cparams(("parallel", "arbitrary")),
        name="retention_prompt",
    )(z, z, z, z, cos, sin, din, dq, dk, dc, rnw, *extra)


def _ret_sample_head(h, dec_ref, z_ref, cos, sin, st_ref, rnw_ref, o_ref, st_out_ref):
    cols = slice(h * RET_HEAD_DIM, (h + 1) * RET_HEAD_DIM)
    part = lambda p: z_ref[:, p * D_RET + h * RET_HEAD_DIM:p * D_RET + (h + 1) * RET_HEAD_DIM]
    q = _rotate(part(0), cos, sin)
    k = _rotate(part(1), cos, sin) * (RET_HEAD_DIM ** -0.5)
    v = part(2)
    g = part(3)
    slab = lambda a, t: a[t * S_TILE_B:(t + 1) * S_TILE_B, :]

    intra = []
    for t in range(DEC_SEQ):
        acc = None
        for j in range(t + 1):
            w = jnp.sum(slab(q, t) * slab(k, j), axis=-1, keepdims=True) * dec_ref[h, t * DEC_SEQ + j]
            acc = w * slab(v, j) if acc is None else acc + w * slab(v, j)
        intra.append(acc)
    o = jnp.concatenate(intra, axis=0)

    dq_rows = jnp.concatenate(
        [jnp.full((S_TILE_B, 1), 1.0, F32) * dec_ref[h, 16 + t] for t in range(DEC_SEQ)], axis=0)
    dk_rows = jnp.concatenate(
        [jnp.full((S_TILE_B, 1), 1.0, F32) * dec_ref[h, 20 + t] for t in range(DEC_SEQ)], axis=0)
    qd = (q * dq_rows).astype(BF16)
    kd = k * dk_rows
    vb = v.astype(BF16)
    dchunk = dec_ref[h, 24]
    row_b = lax.broadcasted_iota(jnp.int32, (S_TILE_ROWS, 1), 0) % S_TILE_B
    for b in range(S_TILE_B):
        mine = row_b == b
        s = st_ref[b, h]
        o = o + jnp.where(mine, jnp.dot(qd, s.astype(BF16), preferred_element_type=F32), 0.0)
        st_out_ref[b, h] = s * dchunk + _dot_t_lhs(jnp.where(mine, kd, 0.0).astype(BF16), vb)
    o_ref[:, cols] = _head_norm_gate(o, g, rnw_ref[:, cols]).astype(o_ref.dtype)


def _ret_sample_kernel(dec_ref, z_ref, cos_ref, sin_ref, st_ref, rnw_ref, o_ref, st_out_ref):
    cos, sin = cos_ref[...], sin_ref[...]
    for h in range(RET_HEADS):
        _ret_sample_head(h, dec_ref, z_ref, cos, sin, st_ref, rnw_ref, o_ref, st_out_ref)


def retention_sample(dec, z, cos_s, sin_s, state_ret, rnw, mix, layer, state_prev):
    extra = [mix] + _carried(state_prev)
    state_spec = pl.BlockSpec((None, S_TILE_B, RET_HEADS, RET_HEAD_DIM, RET_HEAD_DIM),
                              lambda i: (layer, i, 0, 0, 0))
    return pl.pallas_call(
        _skip_refs(_ret_sample_kernel, 6, len(extra)),
        out_shape=(jax.ShapeDtypeStruct(mix.shape, mix.dtype),
                   jax.ShapeDtypeStruct(state_ret.shape, F32)),
        grid=(N_S_TILES,),
        in_specs=[
            pl.BlockSpec(memory_space=pltpu.SMEM),
            pl.BlockSpec((S_TILE_ROWS, D_IN), lambda i: (S_BLK0 + i, 0)),
            pl.BlockSpec((S_TILE_ROWS, RET_HEAD_DIM // 2), lambda i: (0, 0)),
            pl.BlockSpec((S_TILE_ROWS, RET_HEAD_DIM // 2), lambda i: (0, 0)),
            state_spec,
            pl.BlockSpec((None, 1, D_RET), lambda i: (layer, 0, 0)),
        ] + [_ANY] * len(extra),
        out_specs=(pl.BlockSpec((S_TILE_ROWS, D_RET), lambda i: (S_BLK0 + i, 0)), state_spec),
        input_output_aliases={6 + n: n for n in range(len(extra))},
        compiler_params=_cparams(("arbitrary",)),
        name="retention_sample",
    )(dec, z, cos_s, sin_s, state_ret, rnw, *extra)


POOL_P_ROWS = 512
POOL_HALO = 16


def _shift_rows(a, s):
    return pltpu.roll(a, s, axis=0)


def _pool_prompt_kernel(u_ref, pw_ref, ps_ref, o_ref, buf_ref, halo_ref):
    t = pl.program_id(1)

    @pl.when(t == 0)
    def _():
        halo_ref[...] = jnp.zeros_like(halo_ref)

    pos = t * POOL_P_ROWS + lax.broadcasted_iota(jnp.int32, (POOL_P_ROWS, 1), 0)
    for g, w in enumerate(POOL_WINDOWS):
        cols = slice(g * POOL_GROUP_DIM, (g + 1) * POOL_GROUP_DIM)
        u = u_ref[:, cols]
        a = jnp.concatenate([halo_ref[:, cols], u], axis=0)
        s = 1
        while s < w:
            a = a + _shift_rows(a, s)
            s *= 2
        cnt = jnp.minimum(pos + 1, w).astype(F32)
        pooled = a[POOL_HALO:, :] / cnt - u
        pm = jnp.dot(pooled.astype(BF16), pw_ref[g].astype(BF16), preferred_element_type=F32)
        o_ref[:, cols] = (pm * ps_ref[:, cols]).astype(o_ref.dtype)

    halo_ref[...] = u_ref[POOL_P_ROWS - POOL_HALO:, :]

    @pl.when(t == pl.num_programs(1) - 1)
    def _():
        buf_ref[...] = u_ref[POOL_P_ROWS - POOL_BUF:, :]


def pool_prompt(z, pool_w, pool_scale, layer, buf_prev):
    nt = SEQ // POOL_P_ROWS
    extra = _carried(buf_prev)
    return pl.pallas_call(
        _skip_refs(_pool_prompt_kernel, 3, len(extra)),
        out_shape=(jax.ShapeDtypeStruct((M_ROWS, D_POOL), BF16),
                   jax.ShapeDtypeStruct((DEPTH, BATCH, POOL_BUF, D_POOL), F32)),
        grid=(BATCH, nt),
        in_specs=[
            pl.BlockSpec((POOL_P_ROWS, D_POOL), lambda b, t: (b * nt + t, 4)),
            pl.BlockSpec((None, 4, POOL_GROUP_DIM, POOL_GROUP_DIM), lambda b, t: (layer, 0, 0, 0)),
            pl.BlockSpec((None, 1, D_POOL), lambda b, t: (layer, 0, 0)),
        ] + [_ANY] * len(extra),
        out_specs=(
            pl.BlockSpec((POOL_P_ROWS, D_POOL), lambda b, t: (b * nt + t, 0)),
            pl.BlockSpec((None, None, POOL_BUF, D_POOL), lambda b, t: (layer, b, 0, 0)),
        ),
        scratch_shapes=[pltpu.VMEM((POOL_HALO, D_POOL), F32)],
        input_output_aliases={3: 1} if extra else {},
        compiler_params=_cparams(("parallel", "arbitrary")),
        name="pool_prompt",
    )(z, pool_w, pool_scale, *extra)


def _pool_sample_kernel(u_ref, buf_ref, pw_ref, ps_ref, o_ref, nbuf_ref):

    def ext(r, cols):
        if r < POOL_BUF:
            return buf_ref[:, r, cols]
        return u_ref[(r - POOL_BUF) * S_TILE_B:(r - POOL_BUF + 1) * S_TILE_B, cols]

    for g, w in enumerate(POOL_WINDOWS):
        cols = slice(g * POOL_GROUP_DIM, (g + 1) * POOL_GROUP_DIM)
        pooled = []
        for t in range(DEC_SEQ):
            win = ext(POOL_BUF + t, cols)
            for r in range(POOL_BUF + t - w + 1, POOL_BUF + t):
                win = win + ext(r, cols)
            cnt = float(min(t + PAST_LEN + 1, w))
            pooled.append(win / cnt - ext(POOL_BUF + t, cols))
        pooled = jnp.concatenate(pooled, axis=0)
        pm = jnp.dot(pooled.astype(BF16), pw_ref[g].astype(BF16), preferred_element_type=F32)
        o_ref[:, cols] = (pm * ps_ref[:, cols]).astype(o_ref.dtype)

    full = slice(0, D_POOL)
    for r in range(POOL_BUF):
        nbuf_ref[:, r, :] = ext(r + DEC_SEQ, full)


def pool_sample(z, state_pool, pool_w, pool_scale, mix, layer, buf_prev):
    extra = [mix] + _carried(buf_prev)
    return pl.pallas_call(
        _skip_refs(_pool_sample_kernel, 4, len(extra)),
        out_shape=(jax.ShapeDtypeStruct(mix.shape, mix.dtype),
                   jax.ShapeDtypeStruct(state_pool.shape, F32)),
        grid=(N_S_TILES,),
        in_specs=[
            pl.BlockSpec((S_TILE_ROWS, D_POOL), lambda i: (S_BLK0 + i, 4)),
            pl.BlockSpec((None, S_TILE_B, POOL_BUF, D_POOL), lambda i: (layer, i, 0, 0)),
            pl.BlockSpec((None, 4, POOL_GROUP_DIM, POOL_GROUP_DIM), lambda i: (layer, 0, 0, 0)),
            pl.BlockSpec((None, 1, D_POOL), lambda i: (layer, 0, 0)),
        ] + [_ANY] * len(extra),
        out_specs=(
            pl.BlockSpec((S_TILE_ROWS, D_POOL), lambda i: (S_BLK0 + i, 0)),
            pl.BlockSpec((None, S_TILE_B, POOL_BUF, D_POOL), lambda i: (layer, i, 0, 0)),
        ),
        input_output_aliases={4 + n: n for n in range(len(extra))},
        compiler_params=_cparams(("arbitrary",)),
        name="pool_sample",
    )(z, state_pool, pool_w, pool_scale, *extra)


def _softmax_rows(s):
    m = jnp.max(s, axis=-1, keepdims=True)
    e = jnp.exp(s - m)
    return e / jnp.sum(e, axis=-1, keepdims=True)


XATTN_P_ROWS = 1024


def _xattn_prompt_kernel(q_ref, mk_ref, mv_ref, o_ref):
    for h in range(MEM_HEADS):
        cols = slice(h * MEM_HEAD_DIM, (h + 1) * MEM_HEAD_DIM)
        s = _dot_t_rhs(q_ref[:, cols], mk_ref[:, cols].astype(BF16)) * (MEM_HEAD_DIM ** -0.5)
        p = _softmax_rows(s)
        o_ref[:, cols] = jnp.dot(p.astype(BF16), mv_ref[:, cols].astype(BF16),
                                 preferred_element_type=F32).astype(o_ref.dtype)


def xattn_prompt(qx, mk, mv, layer):
    nt = SEQ // XATTN_P_ROWS
    return pl.pallas_call(
        _xattn_prompt_kernel,
        out_shape=jax.ShapeDtypeStruct((M_ROWS, D_MODEL), BF16),
        grid=(BATCH, nt),
        in_specs=[
            pl.BlockSpec((XATTN_P_ROWS, D_MODEL), lambda b, t: (b * nt + t, 0)),
            pl.BlockSpec((None, N_MEM, D_MODEL), lambda b, t: (layer, b, 0)),
            pl.BlockSpec((None, N_MEM, D_MODEL), lambda b, t: (layer, b, 0)),
        ],
        out_specs=pl.BlockSpec((XATTN_P_ROWS, D_MODEL), lambda b, t: (b * nt + t, 0)),
        compiler_params=_cparams(("parallel", "arbitrary")),
        name="xattn_prompt",
    )(qx, mk, mv)


XATTN_S_B = 4
XATTN_S_ROWS = MEM_HEADS * DEC_SEQ
XATTN_S_KEYS = N_MEM * MEM_HEADS


def _xattn_sample_kernel(q_ref, k_ref, v_ref, o_ref):
    row_h = lax.broadcasted_iota(jnp.int32, (XATTN_S_ROWS, XATTN_S_KEYS), 0) // DEC_SEQ
    col_h = lax.broadcasted_iota(jnp.int32, (XATTN_S_ROWS, XATTN_S_KEYS), 1) % MEM_HEADS
    same_head = row_h == col_h
    for bb in range(XATTN_S_B):
        k2 = k_ref[bb].reshape(XATTN_S_KEYS, MEM_HEAD_DIM).astype(BF16)
        v2 = v_ref[bb].reshape(XATTN_S_KEYS, MEM_HEAD_DIM).astype(BF16)
        s = _dot_t_rhs(q_ref[bb], k2) * (MEM_HEAD_DIM ** -0.5)
        s = jnp.where(same_head, s, -jnp.inf)
        p = _softmax_rows(s)
        o_ref[bb] = jnp.dot(p.astype(BF16), v2, preferred_element_type=F32).astype(o_ref.dtype)


def xattn_sample(qs, cache_k, cache_v, layer):
    kv_spec = pl.BlockSpec((None, XATTN_S_B, N_MEM, MEM_HEADS, MEM_HEAD_DIM),
                           lambda i: (layer, i, 0, 0, 0))
    qo_spec = pl.BlockSpec((XATTN_S_B, XATTN_S_ROWS, MEM_HEAD_DIM), lambda i: (i, 0, 0))
    return pl.pallas_call(
        _xattn_sample_kernel,
        out_shape=jax.ShapeDtypeStruct(qs.shape, BF16),
        grid=(DEC_BATCH // XATTN_S_B,),
        in_specs=[qo_spec, kv_spec, kv_spec],
        out_specs=qo_spec,
        compiler_params=_cparams(("parallel",)),
        name="xattn_sample",
    )(qs, cache_k, cache_v)


def _sample_rows_to_heads(a):
    a = a.reshape(N_S_TILES, DEC_SEQ, S_TILE_B, MEM_HEADS, MEM_HEAD_DIM)
    return a.transpose(0, 2, 3, 1, 4).reshape(DEC_BATCH, XATTN_S_ROWS, MEM_HEAD_DIM)


def _heads_to_sample_rows(a):
    a = a.reshape(N_S_TILES, S_TILE_B, MEM_HEADS, DEC_SEQ, MEM_HEAD_DIM)
    return a.transpose(0, 3, 1, 2, 4).reshape(S_ROWS, D_MODEL)


FINAL_ROWS = 512


def _final_norm_kernel(x_ref, nw_ref, yp_ref, ys_ref):
    def run(dst_ref):
        def body(rows):
            dst_ref[rows, :] = _rms_rows(x_ref[rows, :], nw_ref[...])
        _for_row_chunks(FINAL_ROWS, _norm_chunk(FINAL_ROWS), body)

    is_prompt = pl.program_id(0) < P_ROWS // FINAL_ROWS
    pl.when(is_prompt)(lambda: run(yp_ref))
    pl.when(jnp.logical_not(is_prompt))(lambda: run(ys_ref))


def final_norm(x, nw):
    n_p = P_ROWS // FINAL_ROWS
    return pl.pallas_call(
        _final_norm_kernel,
        out_shape=(jax.ShapeDtypeStruct((P_ROWS, D_MODEL), F32),
                   jax.ShapeDtypeStruct((S_ROWS, D_MODEL), F32)),
        grid=(M_ROWS // FINAL_ROWS,),
        in_specs=[
            pl.BlockSpec((FINAL_ROWS, D_MODEL), lambda i: (i, 0)),
            pl.BlockSpec((1, D_MODEL), lambda i: (0, 0)),
        ],
        out_specs=(
            pl.BlockSpec((FINAL_ROWS, D_MODEL), lambda i: (jnp.minimum(i, n_p - 1), 0)),
            pl.BlockSpec((S_ROWS, D_MODEL), lambda i: (0, 0)),
        ),
        compiler_params=_cparams(("arbitrary",)),
        name="final_norm",
    )(x, nw)


def _rope_tables(pos):
    half = RET_HEAD_DIM // 2
    inv = ROPE_BASE ** (-jnp.arange(half, dtype=F32) / half)
    ang = pos.astype(F32)[:, None] * inv[None, :]
    return jnp.cos(ang), jnp.sin(ang)


def _log_gamma():
    return jnp.log1p(-jnp.exp2(-5.0 - jnp.arange(RET_HEADS, dtype=F32)))


def _decay_tables(chunk):
    lg = _log_gamma()
    idx = jnp.arange(chunk, dtype=F32)
    diff = idx[:, None] - idx[None, :]
    decay_in = jnp.where(diff[None] >= 0.0,
                         jnp.exp(lg[:, None, None] * jnp.maximum(diff, 0.0)[None]), 0.0)
    decay_q = jnp.exp(lg[:, None] * (idx[None, :] + 1.0))
    decay_k = jnp.exp(lg[:, None] * (chunk - 1.0 - idx[None, :]))
    decay_chunk = jnp.exp(lg * chunk)
    return decay_in, decay_q, decay_k, decay_chunk


def _to_sample_rows(a):
    d = a.shape[-1]
    return a.reshape(N_S_TILES, S_TILE_B, DEC_SEQ, d).transpose(0, 2, 1, 3).reshape(S_ROWS, d)


def _from_sample_rows(a):
    d = a.shape[-1]
    return a.reshape(N_S_TILES, DEC_SEQ, S_TILE_B, d).transpose(0, 2, 1, 3).reshape(DEC_BATCH, DEC_SEQ, d)


def kernel(x_prompt, x_sample, mem_prompt, state_ret, state_pool, cache_mem_k, cache_mem_v,
           attn_norm_w, w_in, ret_norm_w, pool_w, pool_scale, w_out, xattn_norm_w, mem_norm_w,
           w_xq, w_mk, w_mv, w_xo, mlp_norm_w, w_up, w_down, final_norm_w):
    cos_p, sin_p = _rope_tables(jnp.arange(SEQ))
    cos_s, sin_s = _rope_tables(jnp.arange(DEC_SEQ) + PAST_LEN)
    cos_s = jnp.repeat(cos_s, S_TILE_B, axis=0)
    sin_s = jnp.repeat(sin_s, S_TILE_B, axis=0)
    din_p, dq_p, dk_p, dc_p = _decay_tables(RET_CHUNK)
    half = RET_HEAD_DIM // 2
    dq_p = jnp.broadcast_to(dq_p[:, :, None], (RET_HEADS, RET_CHUNK, half))
    dk_p = jnp.broadcast_to(dk_p[:, :, None], (RET_HEADS, RET_CHUNK, half))
    dc_p = jnp.broadcast_to(dc_p[:, None, None], (RET_HEADS, 1, RET_HEAD_DIM))
    din_s, dq_s, dk_s, dc_s = _decay_tables(DEC_SEQ)
    dec_s = jnp.concatenate([din_s.reshape(RET_HEADS, DEC_SEQ * DEC_SEQ), dq_s, dk_s,
                             dc_s[:, None], jnp.zeros((RET_HEADS, 7), F32)], axis=1)

    row3 = lambda a: a.reshape(DEPTH, 1, a.shape[-1])
    attn_nw, xattn_nw, mem_nw, mlp_nw = map(row3, (attn_norm_w, xattn_norm_w, mem_norm_w, mlp_norm_w))
    ret_nw, pool_sc = row3(ret_norm_w), row3(pool_scale)
    w_in_b, w_up_b, w_down_b = w_in.astype(BF16), w_up.astype(BF16), w_down.astype(BF16)

    x = jnp.concatenate([x_prompt.reshape(P_ROWS, D_MODEL), _to_sample_rows(x_sample)], axis=0)
    mem = mem_prompt.reshape(BATCH * N_MEM, D_MODEL)

    ret_p = buf_p = mk_p = mv_p = ret_s = buf_s = None
    for l in range(DEPTH):
        z = norm_matmul_resident(x, attn_nw, w_in_b, l, tm=TM_RESIDENT, out_dtype=F32)
        mix_r, ret_p = retention_prompt(z, cos_p, sin_p, din_p, dq_p, dk_p, dc_p, ret_nw, l, ret_p)
        mix_r, ret_s = retention_sample(dec_s, z, cos_s, sin_s, state_ret, ret_nw, mix_r, l, ret_s)
        mix_p, buf_p = pool_prompt(z, pool_w, pool_sc, l, buf_p)
        mix_p, buf_s = pool_sample(z, state_pool, pool_w, pool_sc, mix_p, l, buf_s)
        x, h = matmul_residual_resident([mix_r, mix_p], w_out, x, xattn_nw, l, tm=TM_RESIDENT)

        mk_p = norm_matmul_stacked(mem, mem_nw, w_mk, l, mk_p, tm=1024, tn=512)
        mv_p = norm_matmul_stacked(mem, mem_nw, w_mv, l, mv_p, tm=1024, tn=512)
        qx = matmul_resident(h, w_xq, l, tm=TM)
        att = xattn_prompt(qx, mk_p, mv_p, l)
        att_s = xattn_sample(_sample_rows_to_heads(qx[P_ROWS:]), cache_mem_k, cache_mem_v, l)
        att = lax.dynamic_update_slice(att, _heads_to_sample_rows(att_s), (P_ROWS, 0))
        x, h = matmul_residual_resident([att], w_xo, x, mlp_nw, l, tm=TM_RESIDENT)
        x = mlp_residual(h, x, w_up_b, w_down_b, l, tm=TM)

    y_p, y_s = final_norm(x, final_norm_w.reshape(1, D_MODEL))
    mem_shape = (DEPTH, BATCH, N_MEM, MEM_HEADS, MEM_HEAD_DIM)
    return (y_p.reshape(BATCH, SEQ, D_MODEL), _from_sample_rows(y_s),
            ret_p, buf_p, mk_p.reshape(mem_shape), mv_p.reshape(mem_shape), ret_s, buf_s)
```

```python
import functools

import jax
import jax.numpy as jnp
from jax import lax
from jax.experimental import pallas as pl
from jax.experimental.pallas import tpu as pltpu

D_MODEL = 2048
BATCH = 4
SEQ = 2048
DEPTH = 2
DEC_BATCH = 128
DEC_SEQ = 4
PAST_LEN = 16384
D_RET = 1024
RET_HEADS = 4
RET_HEAD_DIM = 256
D_POOL = 1024
POOL_WINDOWS = (2, 4, 8, 16)
POOL_GROUP_DIM = 256
POOL_BUF = 15
D_IN = 5120
N_MEM = 256
MEM_HEADS = 4
MEM_HEAD_DIM = 512
D_FF = 8192
RET_CHUNK = 128
ROPE_BASE = 10000.0
EPS = 1e-6

F32 = jnp.float32
BF16 = jnp.bfloat16

P_ROWS = BATCH * SEQ
S_ROWS = DEC_BATCH * DEC_SEQ
M_ROWS = P_ROWS + S_ROWS
S_TILE_B = 8
S_TILE_ROWS = S_TILE_B * DEC_SEQ
N_S_TILES = DEC_BATCH // S_TILE_B
S_BLK0 = P_ROWS // S_TILE_ROWS

TM = 1088
TM_RESIDENT = 544
TM_IN_PROJ = 272
VMEM_LIMIT = 58 * 1024 * 1024


def _cparams(sem):
    return pltpu.CompilerParams(dimension_semantics=sem, vmem_limit_bytes=VMEM_LIMIT)


_ANY = pl.BlockSpec(memory_space=pl.ANY)


def _skip_refs(kernel_fn, start, count):
    def wrapped(*refs):
        return kernel_fn(*refs[:start], *refs[start + count:])
    return wrapped


def _carried(prev):
    return [] if prev is None else [prev]


def _rms_rows(x, nw):
    ms = jnp.mean(x * x, axis=-1, keepdims=True)
    return x * lax.rsqrt(ms + EPS) * nw


def _for_row_chunks(n_rows, chunk, body):
    def step(c, carry):
        body(pl.ds(pl.multiple_of(c * chunk, chunk), chunk))
        return carry
    lax.fori_loop(0, n_rows // chunk, step, 0)


def _norm_chunk(n_rows):
    return 272 if n_rows % 272 == 0 else 128


def _cast_weight_once(w_ref, wb_ref):
    @pl.when(pl.program_id(0) == 0)
    def _():
        def body(rows):
            wb_ref[rows, :] = w_ref[rows, :].astype(BF16)
        _for_row_chunks(w_ref.shape[0], 256, body)


def _norm_matmul_kernel(x_ref, nw_ref, w_ref, o_ref, h_ref, *, tm):
    @pl.when(pl.program_id(1) == 0)
    def _():
        def body(rows):
            h_ref[rows, :] = _rms_rows(x_ref[rows, :], nw_ref[...]).astype(BF16)
        _for_row_chunks(tm, _norm_chunk(tm), body)

    o_ref[...] = jnp.dot(h_ref[...], w_ref[...].astype(BF16),
                         preferred_element_type=F32).astype(o_ref.dtype)


def _in_proj_kernel(x_ref, nw_ref, w_ref, wu_ref, wd_ref, o_ref, wub_ref, wdb_ref):
    h = _rms_rows(x_ref[...], nw_ref[...]).astype(BF16)
    o_ref[...] = jnp.dot(h, w_ref[...], preferred_element_type=F32)
    wub_ref[...] = wu_ref[...].astype(BF16)
    wdb_ref[...] = wd_ref[...].astype(BF16)


def in_proj_and_mlp_weight_cast(x, nw, w, w_up, w_down, layer, *, tm):
    m, k = x.shape
    n = w.shape[-1]
    ff = w_up.shape[-1]
    steps = m // tm
    slab = ff // steps
    return pl.pallas_call(
        _in_proj_kernel,
        out_shape=(jax.ShapeDtypeStruct((m, n), F32),
                   jax.ShapeDtypeStruct((k, ff), BF16),
                   jax.ShapeDtypeStruct((ff, k), BF16)),
        grid=(steps,),
        in_specs=[
            pl.BlockSpec((tm, k), lambda i: (i, 0)),
            pl.BlockSpec((None, 1, k), lambda i: (layer, 0, 0)),
            pl.BlockSpec((None, k, n), lambda i: (layer, 0, 0), pipeline_mode=pl.Buffered(1)),
            pl.BlockSpec((None, k, slab), lambda i: (layer, 0, i)),
            pl.BlockSpec((None, slab, k), lambda i: (layer, i, 0)),
        ],
        out_specs=(
            pl.BlockSpec((tm, n), lambda i: (i, 0)),
            pl.BlockSpec((k, slab), lambda i: (0, i)),
            pl.BlockSpec((slab, k), lambda i: (i, 0)),
        ),
        compiler_params=_cparams(("parallel",)),
        name="in_proj",
    )(x, nw, w, w_up, w_down)


def norm_matmul_stacked(x, nw, w, layer, prev, *, tm, tn):
    m, k = x.shape
    n = w.shape[-1]
    extra = _carried(prev)
    return pl.pallas_call(
        _skip_refs(functools.partial(_norm_matmul_kernel, tm=tm), 3, len(extra)),
        out_shape=jax.ShapeDtypeStruct((DEPTH, m, n), F32),
        grid=(m // tm, n // tn),
        in_specs=[
            pl.BlockSpec((tm, k), lambda i, j: (i, 0)),
            pl.BlockSpec((None, 1, k), lambda i, j: (layer, 0, 0)),
            pl.BlockSpec((None, k, tn), lambda i, j: (layer, 0, j)),
        ] + [_ANY] * len(extra),
        out_specs=pl.BlockSpec((None, tm, tn), lambda i, j: (layer, i, j)),
        scratch_shapes=[pltpu.VMEM((tm, k), BF16)],
        input_output_aliases={3: 0} if extra else {},
        compiler_params=_cparams(("parallel", "arbitrary")),
        name="norm_matmul_stacked",
    )(x, nw, w, *extra)


def _resident_weight_spec(w, layer):
    return pl.BlockSpec((None,) + w.shape[1:], lambda i: (layer, 0, 0), pipeline_mode=pl.Buffered(1))


def _matmul_resident_kernel(a_ref, w_ref, o_ref, wb_ref):
    _cast_weight_once(w_ref, wb_ref)
    o_ref[...] = jnp.dot(a_ref[...], wb_ref[...], preferred_element_type=F32).astype(o_ref.dtype)


def matmul_resident(a, w, layer, *, tm):
    m, k = a.shape
    n = w.shape[-1]
    return pl.pallas_call(
        _matmul_resident_kernel,
        out_shape=jax.ShapeDtypeStruct((m, n), BF16),
        grid=(m // tm,),
        in_specs=[pl.BlockSpec((tm, k), lambda i: (i, 0)), _resident_weight_spec(w, layer)],
        out_specs=pl.BlockSpec((tm, n), lambda i: (i, 0)),
        scratch_shapes=[pltpu.VMEM((k, n), BF16)],
        compiler_params=_cparams(("arbitrary",)),
        name="matmul_resident",
    )(a, w)


def _matmul_res_resident_kernel(*refs, n_parts):
    a_refs = refs[:n_parts]
    w_ref, r_ref, nw_ref, o_ref, h_ref, wb_ref = refs[n_parts:]
    _cast_weight_once(w_ref, wb_ref)
    acc = r_ref[...]
    k0 = 0
    for a_ref in a_refs:
        k1 = k0 + a_ref.shape[1]
        acc = acc + jnp.dot(a_ref[...], wb_ref[k0:k1, :], preferred_element_type=F32)
        k0 = k1
    o_ref[...] = acc
    h_ref[...] = _rms_rows(acc, nw_ref[...]).astype(BF16)


def matmul_residual_resident(a_parts, w, res, nw, layer, *, tm):
    m, n = res.shape
    k = sum(a.shape[1] for a in a_parts)
    row_tile = lambda width: pl.BlockSpec((tm, width), lambda i: (i, 0))
    return pl.pallas_call(
        functools.partial(_matmul_res_resident_kernel, n_parts=len(a_parts)),
        out_shape=(jax.ShapeDtypeStruct((m, n), F32), jax.ShapeDtypeStruct((m, n), BF16)),
        grid=(m // tm,),
        in_specs=[row_tile(a.shape[1]) for a in a_parts] + [
            _resident_weight_spec(w, layer),
            row_tile(n),
            pl.BlockSpec((None, 1, n), lambda i: (layer, 0, 0)),
        ],
        out_specs=(row_tile(n), row_tile(n)),
        scratch_shapes=[pltpu.VMEM((k, n), BF16)],
        compiler_params=_cparams(("arbitrary",)),
        name="matmul_residual_resident",
    )(*a_parts, w, res, nw)


MLP_TF = 1024
MLP_TN = 512
MLP_KC = 4096
MLP_UP_STEPS = D_FF // MLP_TF
MLP_K_STEPS = D_FF // MLP_KC
MLP_DOWN_STEPS = (D_MODEL // MLP_TN) * MLP_K_STEPS


def _mlp_kernel(h_ref, x_ref, wu_ref, wd_ref, o_ref, hid_ref):
    s = pl.program_id(1)

    @pl.when(s < MLP_UP_STEPS)
    def _():
        hid = jnp.dot(h_ref[...], wu_ref[...], preferred_element_type=F32)
        per_chunk = MLP_KC // MLP_TF
        c0 = pl.multiple_of((s % per_chunk) * MLP_TF, MLP_TF)
        hid_ref[s // per_chunk, :, pl.ds(c0, MLP_TF)] = jnp.square(jnp.maximum(hid, 0.0)).astype(BF16)

    @pl.when(s >= MLP_UP_STEPS)
    def _():
        kc = (s - MLP_UP_STEPS) % MLP_K_STEPS
        part = jnp.dot(hid_ref[kc], wd_ref[...], preferred_element_type=F32)

        @pl.when(kc == 0)
        def _():
            o_ref[...] = x_ref[...] + part

        @pl.when(kc != 0)
        def _():
            o_ref[...] += part


def mlp_residual(h, x, w_up, w_down, *, tm):
    m, k = x.shape
    down = lambda s: jnp.maximum(s - MLP_UP_STEPS, 0)
    out_tile = pl.BlockSpec((tm, MLP_TN), lambda i, s: (i, down(s) // MLP_K_STEPS))
    return pl.pallas_call(
        _mlp_kernel,
        out_shape=jax.ShapeDtypeStruct((m, k), F32),
        grid=(m // tm, MLP_UP_STEPS + MLP_DOWN_STEPS),
        in_specs=[
            pl.BlockSpec((tm, k), lambda i, s: (i, 0)),
            out_tile,
            pl.BlockSpec((k, MLP_TF), lambda i, s: (0, jnp.minimum(s, MLP_UP_STEPS - 1))),
            pl.BlockSpec((MLP_KC, MLP_TN), lambda i, s: (down(s) % MLP_K_STEPS, down(s) // MLP_K_STEPS)),
        ],
        out_specs=out_tile,
        scratch_shapes=[pltpu.VMEM((MLP_K_STEPS, tm, MLP_KC), BF16)],
        compiler_params=_cparams(("parallel", "arbitrary")),
        name="mlp_residual",
    )(h, x, w_up, w_down)


def _rotate(x, cos, sin):
    half = x.shape[-1] // 2
    x1, x2 = x[:, :half], x[:, half:]
    return jnp.concatenate([x1 * cos - x2 * sin, x2 * cos + x1 * sin], axis=-1)


def _head_norm_gate(o, g, rnw):
    o = o * lax.rsqrt(jnp.mean(o * o, axis=-1, keepdims=True) + EPS)
    return (g * jax.nn.sigmoid(g)) * (o * rnw)


def _dot_t_lhs(a, b):
    return lax.dot_general(a, b, (((0,), (0,)), ((), ())), preferred_element_type=F32)


def _dot_t_rhs(a, b):
    return lax.dot_general(a, b, (((1,), (1,)), ((), ())), preferred_element_type=F32)


RET_P_ROWS = 512


def _ret_prompt_kernel(q_ref, k_ref, v_ref, g_ref, cos_ref, sin_ref, din_ref, dq_ref, dk_ref,
                       dc_ref, rnw_ref, o_ref, s_out_ref, s_ref):
    @pl.when(pl.program_id(1) == 0)
    def _():
        s_ref[...] = jnp.zeros_like(s_ref)

    for c in range(RET_P_ROWS // RET_CHUNK):
        rows = slice(c * RET_CHUNK, (c + 1) * RET_CHUNK)
        cos, sin = cos_ref[rows, :], sin_ref[rows, :]
        for h in range(RET_HEADS):
            cols = slice(h * RET_HEAD_DIM, (h + 1) * RET_HEAD_DIM)
            dq = jnp.concatenate([dq_ref[h], dq_ref[h]], axis=-1)
            dk = jnp.concatenate([dk_ref[h], dk_ref[h]], axis=-1)
            q = _rotate(q_ref[rows, cols], cos, sin)
            k = _rotate(k_ref[rows, cols], cos, sin) * (RET_HEAD_DIM ** -0.5)
            vb = v_ref[rows, cols].astype(BF16)
            s = s_ref[h]
            scores = _dot_t_rhs(q.astype(BF16), k.astype(BF16)) * din_ref[h]
            o = jnp.dot(scores.astype(BF16), vb, preferred_element_type=F32)
            o = o + jnp.dot((q * dq).astype(BF16), s.astype(BF16), preferred_element_type=F32)
            s_ref[h] = s * dc_ref[h] + _dot_t_lhs((k * dk).astype(BF16), vb)
            o_ref[rows, cols] = _head_norm_gate(o, g_ref[rows, cols], rnw_ref[:, cols]).astype(o_ref.dtype)

    @pl.when(pl.program_id(1) == pl.num_programs(1) - 1)
    def _():
        s_out_ref[...] = s_ref[...]


def retention_prompt(z, cos, sin, din, dq, dk, dc, rnw, layer, state_prev):
    nt = SEQ // RET_P_ROWS
    extra = _carried(state_prev)
    zspec = lambda part: pl.BlockSpec((RET_P_ROWS, D_RET), lambda b, t: (b * nt + t, part))
    whole = lambda a: pl.BlockSpec(a.shape, lambda b, t: (0,) * a.ndim)
    state_block = (RET_HEADS, RET_HEAD_DIM, RET_HEAD_DIM)
    return pl.pallas_call(
        _skip_refs(_ret_prompt_kernel, 11, len(extra)),
        out_shape=(jax.ShapeDtypeStruct((M_ROWS, D_RET), BF16),
                   jax.ShapeDtypeStruct((DEPTH, BATCH) + state_block, F32)),
        grid=(BATCH, nt),
        in_specs=[
            zspec(0), zspec(1), zspec(2), zspec(3),
            pl.BlockSpec((RET_P_ROWS, RET_HEAD_DIM // 2), lambda b, t: (t, 0)),
            pl.BlockSpec((RET_P_ROWS, RET_HEAD_DIM // 2), lambda b, t: (t, 0)),
            whole(din), whole(dq), whole(dk), whole(dc),
            pl.BlockSpec((None, 1, D_RET), lambda b, t: (layer, 0, 0)),
        ] + [_ANY] * len(extra),
        out_specs=(
            pl.BlockSpec((RET_P_ROWS, D_RET), lambda b, t: (b * nt + t, 0)),
            pl.BlockSpec((None, None) + state_block, lambda b, t: (layer, b, 0, 0, 0)),
        ),
        scratch_shapes=[pltpu.VMEM(state_block, F32)],
        input_output_aliases={11: 1} if extra else {},
        compiler_params=_cparams(("parallel", "arbitrary")),
        name="retention_prompt",
    )(z, z, z, z, cos, sin, din, dq, dk, dc, rnw, *extra)


def _ret_sample_head(h, dec_ref, z_ref, cos, sin, st_ref, rnw_ref, o_ref, st_out_ref):
    cols = slice(h * RET_HEAD_DIM, (h + 1) * RET_HEAD_DIM)
    part = lambda p: z_ref[:, p * D_RET + h * RET_HEAD_DIM:p * D_RET + (h + 1) * RET_HEAD_DIM]
    q = _rotate(part(0), cos, sin)
    k = _rotate(part(1), cos, sin) * (RET_HEAD_DIM ** -0.5)
    v = part(2)
    g = part(3)
    slab = lambda a, t: a[t * S_TILE_B:(t + 1) * S_TILE_B, :]

    intra = []
    for t in range(DEC_SEQ):
        acc = None
        for j in range(t + 1):
            w = jnp.sum(slab(q, t) * slab(k, j), axis=-1, keepdims=True) * dec_ref[h, t * DEC_SEQ + j]
            acc = w * slab(v, j) if acc is None else acc + w * slab(v, j)
        intra.append(acc)
    o = jnp.concatenate(intra, axis=0)

    dq_rows = jnp.concatenate(
        [jnp.full((S_TILE_B, 1), 1.0, F32) * dec_ref[h, 16 + t] for t in range(DEC_SEQ)], axis=0)
    dk_rows = jnp.concatenate(
        [jnp.full((S_TILE_B, 1), 1.0, F32) * dec_ref[h, 20 + t] for t in range(DEC_SEQ)], axis=0)
    qd = (q * dq_rows).astype(BF16)
    kd = k * dk_rows
    vb = v.astype(BF16)
    dchunk = dec_ref[h, 24]
    row_b = lax.broadcasted_iota(jnp.int32, (S_TILE_ROWS, 1), 0) % S_TILE_B
    for b in range(S_TILE_B):
        mine = row_b == b
        s = st_ref[b, h]
        o = o + jnp.where(mine, jnp.dot(qd, s.astype(BF16), preferred_element_type=F32), 0.0)
        st_out_ref[b, h] = s * dchunk + _dot_t_lhs(jnp.where(mine, kd, 0.0).astype(BF16), vb)
    o_ref[:, cols] = _head_norm_gate(o, g, rnw_ref[:, cols]).astype(o_ref.dtype)


def _ret_sample_kernel(dec_ref, z_ref, cos_ref, sin_ref, st_ref, rnw_ref, o_ref, st_out_ref):
    cos, sin = cos_ref[...], sin_ref[...]
    for h in range(RET_HEADS):
        _ret_sample_head(h, dec_ref, z_ref, cos, sin, st_ref, rnw_ref, o_ref, st_out_ref)


def retention_sample(dec, z, cos_s, sin_s, state_ret, rnw, mix, layer, state_prev):
    extra = [mix] + _carried(state_prev)
    state_spec = pl.BlockSpec((None, S_TILE_B, RET_HEADS, RET_HEAD_DIM, RET_HEAD_DIM),
                              lambda i: (layer, i, 0, 0, 0))
    return pl.pallas_call(
        _skip_refs(_ret_sample_kernel, 6, len(extra)),
        out_shape=(jax.ShapeDtypeStruct(mix.shape, mix.dtype),
                   jax.ShapeDtypeStruct(state_ret.shape, F32)),
        grid=(N_S_TILES,),
        in_specs=[
            pl.BlockSpec(memory_space=pltpu.SMEM),
            pl.BlockSpec((S_TILE_ROWS, D_IN), lambda i: (S_BLK0 + i, 0)),
            pl.BlockSpec((S_TILE_ROWS, RET_HEAD_DIM // 2), lambda i: (0, 0)),
            pl.BlockSpec((S_TILE_ROWS, RET_HEAD_DIM // 2), lambda i: (0, 0)),
            state_spec,
            pl.BlockSpec((None, 1, D_RET), lambda i: (layer, 0, 0)),
        ] + [_ANY] * len(extra),
        out_specs=(pl.BlockSpec((S_TILE_ROWS, D_RET), lambda i: (S_BLK0 + i, 0)), state_spec),
        input_output_aliases={6 + n: n for n in range(len(extra))},
        compiler_params=_cparams(("arbitrary",)),
        name="retention_sample",
    )(dec, z, cos_s, sin_s, state_ret, rnw, *extra)


POOL_P_ROWS = 512
POOL_HALO = 16


def _shift_rows(a, s):
    return pltpu.roll(a, s, axis=0)


def _pool_prompt_kernel(u_ref, pw_ref, ps_ref, o_ref, buf_ref, halo_ref):
    t = pl.program_id(1)

    @pl.when(t == 0)
    def _():
        halo_ref[...] = jnp.zeros_like(halo_ref)

    pos = t * POOL_P_ROWS + lax.broadcasted_iota(jnp.int32, (POOL_P_ROWS, 1), 0)
    for g, w in enumerate(POOL_WINDOWS):
        cols = slice(g * POOL_GROUP_DIM, (g + 1) * POOL_GROUP_DIM)
        u = u_ref[:, cols]
        a = jnp.concatenate([halo_ref[:, cols], u], axis=0)
        s = 1
        while s < w:
            a = a + _shift_rows(a, s)
            s *= 2
        cnt = jnp.minimum(pos + 1, w).astype(F32)
        pooled = a[POOL_HALO:, :] / cnt - u
        pm = jnp.dot(pooled.astype(BF16), pw_ref[g].astype(BF16), preferred_element_type=F32)
        o_ref[:, cols] = (pm * ps_ref[:, cols]).astype(o_ref.dtype)

    halo_ref[...] = u_ref[POOL_P_ROWS - POOL_HALO:, :]

    @pl.when(t == pl.num_programs(1) - 1)
    def _():
        buf_ref[...] = u_ref[POOL_P_ROWS - POOL_BUF:, :]


def pool_prompt(z, pool_w, pool_scale, layer, buf_prev):
    nt = SEQ // POOL_P_ROWS
    extra = _carried(buf_prev)
    return pl.pallas_call(
        _skip_refs(_pool_prompt_kernel, 3, len(extra)),
        out_shape=(jax.ShapeDtypeStruct((M_ROWS, D_POOL), BF16),
                   jax.ShapeDtypeStruct((DEPTH, BATCH, POOL_BUF, D_POOL), F32)),
        grid=(BATCH, nt),
        in_specs=[
            pl.BlockSpec((POOL_P_ROWS, D_POOL), lambda b, t: (b * nt + t, 4)),
            pl.BlockSpec((None, 4, POOL_GROUP_DIM, POOL_GROUP_DIM), lambda b, t: (layer, 0, 0, 0)),
            pl.BlockSpec((None, 1, D_POOL), lambda b, t: (layer, 0, 0)),
        ] + [_ANY] * len(extra),
        out_specs=(
            pl.BlockSpec((POOL_P_ROWS, D_POOL), lambda b, t: (b * nt + t, 0)),
            pl.BlockSpec((None, None, POOL_BUF, D_POOL), lambda b, t: (layer, b, 0, 0)),
        ),
        scratch_shapes=[pltpu.VMEM((POOL_HALO, D_POOL), F32)],
        input_output_aliases={3: 1} if extra else {},
        compiler_params=_cparams(("parallel", "arbitrary")),
        name="pool_prompt",
    )(z, pool_w, pool_scale, *extra)


def _pool_sample_kernel(u_ref, buf_ref, pw_ref, ps_ref, o_ref, nbuf_ref):

    def ext(r, cols):
        if r < POOL_BUF:
            return buf_ref[:, r, cols]
        return u_ref[(r - POOL_BUF) * S_TILE_B:(r - POOL_BUF + 1) * S_TILE_B, cols]

    for g, w in enumerate(POOL_WINDOWS):
        cols = slice(g * POOL_GROUP_DIM, (g + 1) * POOL_GROUP_DIM)
        pooled = []
        for t in range(DEC_SEQ):
            win = ext(POOL_BUF + t, cols)
            for r in range(POOL_BUF + t - w + 1, POOL_BUF + t):
                win = win + ext(r, cols)
            cnt = float(min(t + PAST_LEN + 1, w))
            pooled.append(win / cnt - ext(POOL_BUF + t, cols))
        pooled = jnp.concatenate(pooled, axis=0)
        pm = jnp.dot(pooled.astype(BF16), pw_ref[g].astype(BF16), preferred_element_type=F32)
        o_ref[:, cols] = (pm * ps_ref[:, cols]).astype(o_ref.dtype)

    full = slice(0, D_POOL)
    for r in range(POOL_BUF):
        nbuf_ref[:, r, :] = ext(r + DEC_SEQ, full)


def pool_sample(z, state_pool, pool_w, pool_scale, mix, layer, buf_prev):
    extra = [mix] + _carried(buf_prev)
    return pl.pallas_call(
        _skip_refs(_pool_sample_kernel, 4, len(extra)),
        out_shape=(jax.ShapeDtypeStruct(mix.shape, mix.dtype),
                   jax.ShapeDtypeStruct(state_pool.shape, F32)),
        grid=(N_S_TILES,),
        in_specs=[
            pl.BlockSpec((S_TILE_ROWS, D_POOL), lambda i: (S_BLK0 + i, 4)),
            pl.BlockSpec((None, S_TILE_B, POOL_BUF, D_POOL), lambda i: (layer, i, 0, 0)),
            pl.BlockSpec((None, 4, POOL_GROUP_DIM, POOL_GROUP_DIM), lambda i: (layer, 0, 0, 0)),
            pl.BlockSpec((None, 1, D_POOL), lambda i: (layer, 0, 0)),
        ] + [_ANY] * len(extra),
        out_specs=(
            pl.BlockSpec((S_TILE_ROWS, D_POOL), lambda i: (S_BLK0 + i, 0)),
            pl.BlockSpec((None, S_TILE_B, POOL_BUF, D_POOL), lambda i: (layer, i, 0, 0)),
        ),
        input_output_aliases={4 + n: n for n in range(len(extra))},
        compiler_params=_cparams(("arbitrary",)),
        name="pool_sample",
    )(z, state_pool, pool_w, pool_scale, *extra)


def _softmax_rows(s):
    m = jnp.max(s, axis=-1, keepdims=True)
    e = jnp.exp(s - m)
    return e / jnp.sum(e, axis=-1, keepdims=True)


XATTN_P_ROWS = 1024


def _xattn_prompt_kernel(q_ref, mk_ref, mv_ref, o_ref):
    for h in range(MEM_HEADS):
        cols = slice(h * MEM_HEAD_DIM, (h + 1) * MEM_HEAD_DIM)
        s = _dot_t_rhs(q_ref[:, cols], mk_ref[:, cols].astype(BF16)) * (MEM_HEAD_DIM ** -0.5)
        p = _softmax_rows(s)
        o_ref[:, cols] = jnp.dot(p.astype(BF16), mv_ref[:, cols].astype(BF16),
                                 preferred_element_type=F32).astype(o_ref.dtype)


def xattn_prompt(qx, mk, mv, layer):
    nt = SEQ // XATTN_P_ROWS
    return pl.pallas_call(
        _xattn_prompt_kernel,
        out_shape=jax.ShapeDtypeStruct((M_ROWS, D_MODEL), BF16),
        grid=(BATCH, nt),
        in_specs=[
            pl.BlockSpec((XATTN_P_ROWS, D_MODEL), lambda b, t: (b * nt + t, 0)),
            pl.BlockSpec((None, N_MEM, D_MODEL), lambda b, t: (layer, b, 0)),
            pl.BlockSpec((None, N_MEM, D_MODEL), lambda b, t: (layer, b, 0)),
        ],
        out_specs=pl.BlockSpec((XATTN_P_ROWS, D_MODEL), lambda b, t: (b * nt + t, 0)),
        compiler_params=_cparams(("parallel", "arbitrary")),
        name="xattn_prompt",
    )(qx, mk, mv)


XATTN_S_B = 4
XATTN_S_ROWS = MEM_HEADS * DEC_SEQ
XATTN_S_KEYS = N_MEM * MEM_HEADS


def _xattn_sample_kernel(q_ref, k_ref, v_ref, o_ref):
    row_h = lax.broadcasted_iota(jnp.int32, (XATTN_S_ROWS, XATTN_S_KEYS), 0) // DEC_SEQ
    col_h = lax.broadcasted_iota(jnp.int32, (XATTN_S_ROWS, XATTN_S_KEYS), 1) % MEM_HEADS
    same_head = row_h == col_h
    for bb in range(XATTN_S_B):
        k2 = k_ref[bb].reshape(XATTN_S_KEYS, MEM_HEAD_DIM).astype(BF16)
        v2 = v_ref[bb].reshape(XATTN_S_KEYS, MEM_HEAD_DIM).astype(BF16)
        s = _dot_t_rhs(q_ref[bb], k2) * (MEM_HEAD_DIM ** -0.5)
        s = jnp.where(same_head, s, -jnp.inf)
        p = _softmax_rows(s)
        o_ref[bb] = jnp.dot(p.astype(BF16), v2, preferred_element_type=F32).astype(o_ref.dtype)


def xattn_sample(qs, cache_k, cache_v, layer):
    kv_spec = pl.BlockSpec((None, XATTN_S_B, N_MEM, MEM_HEADS, MEM_HEAD_DIM),
                           lambda i: (layer, i, 0, 0, 0))
    qo_spec = pl.BlockSpec((XATTN_S_B, XATTN_S_ROWS, MEM_HEAD_DIM), lambda i: (i, 0, 0))
    return pl.pallas_call(
        _xattn_sample_kernel,
        out_shape=jax.ShapeDtypeStruct(qs.shape, BF16),
        grid=(DEC_BATCH // XATTN_S_B,),
        in_specs=[qo_spec, kv_spec, kv_spec],
        out_specs=qo_spec,
        compiler_params=_cparams(("parallel",)),
        name="xattn_sample",
    )(qs, cache_k, cache_v)


def _sample_rows_to_heads(a):
    a = a.reshape(N_S_TILES, DEC_SEQ, S_TILE_B, MEM_HEADS, MEM_HEAD_DIM)
    return a.transpose(0, 2, 3, 1, 4).reshape(DEC_BATCH, XATTN_S_ROWS, MEM_HEAD_DIM)


def _heads_to_sample_rows(a):
    a = a.reshape(N_S_TILES, S_TILE_B, MEM_HEADS, DEC_SEQ, MEM_HEAD_DIM)
    return a.transpose(0, 3, 1, 2, 4).reshape(S_ROWS, D_MODEL)


FINAL_ROWS = 512


def _final_norm_kernel(x_ref, nw_ref, yp_ref, ys_ref):
    def run(dst_ref):
        def body(rows):
            dst_ref[rows, :] = _rms_rows(x_ref[rows, :], nw_ref[...])
        _for_row_chunks(FINAL_ROWS, _norm_chunk(FINAL_ROWS), body)

    is_prompt = pl.program_id(0) < P_ROWS // FINAL_ROWS
    pl.when(is_prompt)(lambda: run(yp_ref))
    pl.when(jnp.logical_not(is_prompt))(lambda: run(ys_ref))


def final_norm(x, nw):
    n_p = P_ROWS // FINAL_ROWS
    return pl.pallas_call(
        _final_norm_kernel,
        out_shape=(jax.ShapeDtypeStruct((P_ROWS, D_MODEL), F32),
                   jax.ShapeDtypeStruct((S_ROWS, D_MODEL), F32)),
        grid=(M_ROWS // FINAL_ROWS,),
        in_specs=[
            pl.BlockSpec((FINAL_ROWS, D_MODEL), lambda i: (i, 0)),
            pl.BlockSpec((1, D_MODEL), lambda i: (0, 0)),
        ],
        out_specs=(
            pl.BlockSpec((FINAL_ROWS, D_MODEL), lambda i: (jnp.minimum(i, n_p - 1), 0)),
            pl.BlockSpec((S_ROWS, D_MODEL), lambda i: (0, 0)),
        ),
        compiler_params=_cparams(("arbitrary",)),
        name="final_norm",
    )(x, nw)


def _rope_tables(pos):
    half = RET_HEAD_DIM // 2
    inv = ROPE_BASE ** (-jnp.arange(half, dtype=F32) / half)
    ang = pos.astype(F32)[:, None] * inv[None, :]
    return jnp.cos(ang), jnp.sin(ang)


def _log_gamma():
    return jnp.log1p(-jnp.exp2(-5.0 - jnp.arange(RET_HEADS, dtype=F32)))


def _decay_tables(chunk):
    lg = _log_gamma()
    idx = jnp.arange(chunk, dtype=F32)
    diff = idx[:, None] - idx[None, :]
    decay_in = jnp.where(diff[None] >= 0.0,
                         jnp.exp(lg[:, None, None] * jnp.maximum(diff, 0.0)[None]), 0.0)
    decay_q = jnp.exp(lg[:, None] * (idx[None, :] + 1.0))
    decay_k = jnp.exp(lg[:, None] * (chunk - 1.0 - idx[None, :]))
    decay_chunk = jnp.exp(lg * chunk)
    return decay_in, decay_q, decay_k, decay_chunk


def _to_sample_rows(a):
    d = a.shape[-1]
    return a.reshape(N_S_TILES, S_TILE_B, DEC_SEQ, d).transpose(0, 2, 1, 3).reshape(S_ROWS, d)


def _from_sample_rows(a):
    d = a.shape[-1]
    return a.reshape(N_S_TILES, DEC_SEQ, S_TILE_B, d).transpose(0, 2, 1, 3).reshape(DEC_BATCH, DEC_SEQ, d)


def kernel(x_prompt, x_sample, mem_prompt, state_ret, state_pool, cache_mem_k, cache_mem_v,
           attn_norm_w, w_in, ret_norm_w, pool_w, pool_scale, w_out, xattn_norm_w, mem_norm_w,
           w_xq, w_mk, w_mv, w_xo, mlp_norm_w, w_up, w_down, final_norm_w):
    cos_p, sin_p = _rope_tables(jnp.arange(SEQ))
    cos_s, sin_s = _rope_tables(jnp.arange(DEC_SEQ) + PAST_LEN)
    cos_s = jnp.repeat(cos_s, S_TILE_B, axis=0)
    sin_s = jnp.repeat(sin_s, S_TILE_B, axis=0)
    din_p, dq_p, dk_p, dc_p = _decay_tables(RET_CHUNK)
    half = RET_HEAD_DIM // 2
    dq_p = jnp.broadcast_to(dq_p[:, :, None], (RET_HEADS, RET_CHUNK, half))
    dk_p = jnp.broadcast_to(dk_p[:, :, None], (RET_HEADS, RET_CHUNK, half))
    dc_p = jnp.broadcast_to(dc_p[:, None, None], (RET_HEADS, 1, RET_HEAD_DIM))
    din_s, dq_s, dk_s, dc_s = _decay_tables(DEC_SEQ)
    dec_s = jnp.concatenate([din_s.reshape(RET_HEADS, DEC_SEQ * DEC_SEQ), dq_s, dk_s,
                             dc_s[:, None], jnp.zeros((RET_HEADS, 7), F32)], axis=1)

    row3 = lambda a: a.reshape(DEPTH, 1, a.shape[-1])
    attn_nw, xattn_nw, mem_nw, mlp_nw = map(row3, (attn_norm_w, xattn_norm_w, mem_norm_w, mlp_norm_w))
    ret_nw, pool_sc = row3(ret_norm_w), row3(pool_scale)
    w_in_b = w_in.astype(BF16)

    x = jnp.concatenate([x_prompt.reshape(P_ROWS, D_MODEL), _to_sample_rows(x_sample)], axis=0)
    mem = mem_prompt.reshape(BATCH * N_MEM, D_MODEL)

    ret_p = buf_p = mk_p = mv_p = ret_s = buf_s = None
    for l in range(DEPTH):
        z, w_up_b, w_down_b = in_proj_and_mlp_weight_cast(x, attn_nw, w_in_b, w_up, w_down, l,
                                                          tm=TM_IN_PROJ)
        mix_r, ret_p = retention_prompt(z, cos_p, sin_p, din_p, dq_p, dk_p, dc_p, ret_nw, l, ret_p)
        mix_r, ret_s = retention_sample(dec_s, z, cos_s, sin_s, state_ret, ret_nw, mix_r, l, ret_s)
        mix_p, buf_p = pool_prompt(z, pool_w, pool_sc, l, buf_p)
        mix_p, buf_s = pool_sample(z, state_pool, pool_w, pool_sc, mix_p, l, buf_s)
        x, h = matmul_residual_resident([mix_r, mix_p], w_out, x, xattn_nw, l, tm=TM_RESIDENT)

        mk_p = norm_matmul_stacked(mem, mem_nw, w_mk, l, mk_p, tm=1024, tn=512)
        mv_p = norm_matmul_stacked(mem, mem_nw, w_mv, l, mv_p, tm=1024, tn=512)
        qx = matmul_resident(h, w_xq, l, tm=TM)
        att = xattn_prompt(qx, mk_p, mv_p, l)
        att_s = xattn_sample(_sample_rows_to_heads(qx[P_ROWS:]), cache_mem_k, cache_mem_v, l)
        att = lax.dynamic_update_slice(att, _heads_to_sample_rows(att_s), (P_ROWS, 0))
        x, h = matmul_residual_resident([att], w_xo, x, mlp_nw, l, tm=TM_RESIDENT)
        x = mlp_residual(h, x, w_up_b, w_down_b, tm=TM)

    y_p, y_s = final_norm(x, final_norm_w.reshape(1, D_MODEL))
    mem_shape = (DEPTH, BATCH, N_MEM, MEM_HEADS, MEM_HEAD_DIM)
    return (y_p.reshape(BATCH, SEQ, D_MODEL), _from_sample_rows(y_s),
            ret_p, buf_p, mk_p.reshape(mem_shape), mv_p.reshape(mem_shape), ret_s, buf_s)
```

```python
import functools

import jax
import jax.numpy as jnp
from jax import lax
from jax.experimental import pallas as pl
from jax.experimental.pallas import tpu as pltpu

D_MODEL = 2048
BATCH = 4
SEQ = 2048
DEPTH = 2
DEC_BATCH = 128
DEC_SEQ = 4
PAST_LEN = 16384
D_RET = 1024
RET_HEADS = 4
RET_HEAD_DIM = 256
D_POOL = 1024
POOL_WINDOWS = (2, 4, 8, 16)
POOL_GROUP_DIM = 256
POOL_BUF = 15
D_IN = 5120
N_MEM = 256
MEM_HEADS = 4
MEM_HEAD_DIM = 512
D_FF = 8192
RET_CHUNK = 128
ROPE_BASE = 10000.0
EPS = 1e-6

F32 = jnp.float32
BF16 = jnp.bfloat16

P_ROWS = BATCH * SEQ
S_ROWS = DEC_BATCH * DEC_SEQ
M_ROWS = P_ROWS + S_ROWS
S_TILE_B = 8
S_TILE_ROWS = S_TILE_B * DEC_SEQ
N_S_TILES = DEC_BATCH // S_TILE_B
S_BLK0 = P_ROWS // S_TILE_ROWS

TM = 1088
TM_RESIDENT = 512
TM_IN_PROJ = 256
VMEM_LIMIT = 58 * 1024 * 1024


def _cparams(sem):
    return pltpu.CompilerParams(dimension_semantics=sem, vmem_limit_bytes=VMEM_LIMIT)


_ANY = pl.BlockSpec(memory_space=pl.ANY)


def _skip_refs(kernel_fn, start, count):
    def wrapped(*refs):
        return kernel_fn(*refs[:start], *refs[start + count:])
    return wrapped


def _carried(prev):
    return [] if prev is None else [prev]


def _stacked_row_specs(sources, tm):
    specs, ends, start = [], [], 0
    for a in sources:
        n = a.shape[0] // tm
        specs.append(pl.BlockSpec((tm, a.shape[1]),
                                  lambda i, start=start, n=n: (jnp.clip(i - start, 0, n - 1), 0)))
        start += n
        ends.append(start)
    return specs, tuple(ends)


def _read_stacked_rows(refs, ends, rows=slice(None)):
    i = pl.program_id(0)
    x = refs[-1][rows, :]
    for ref, end in zip(reversed(refs[:-1]), reversed(ends[:-1])):
        x = jnp.where(i < end, ref[rows, :], x)
    return x


def _rms_rows(x, nw):
    ms = jnp.mean(x * x, axis=-1, keepdims=True)
    return x * lax.rsqrt(ms + EPS) * nw


def _for_row_chunks(n_rows, chunk, body):
    def step(c, carry):
        body(pl.ds(pl.multiple_of(c * chunk, chunk), chunk))
        return carry
    lax.fori_loop(0, n_rows // chunk, step, 0)


def _norm_chunk(n_rows):
    return 272 if n_rows % 272 == 0 else 128


def _cast_weight_once(w_ref, wb_ref):
    @pl.when(pl.program_id(0) == 0)
    def _():
        def body(rows):
            wb_ref[rows, :] = w_ref[rows, :].astype(BF16)
        _for_row_chunks(w_ref.shape[0], 256, body)


def _norm_matmul_kernel(x_ref, nw_ref, w_ref, o_ref, h_ref, *, tm):
    @pl.when(pl.program_id(1) == 0)
    def _():
        def body(rows):
            h_ref[rows, :] = _rms_rows(x_ref[rows, :], nw_ref[...]).astype(BF16)
        _for_row_chunks(tm, _norm_chunk(tm), body)

    o_ref[...] = jnp.dot(h_ref[...], w_ref[...].astype(BF16),
                         preferred_element_type=F32).astype(o_ref.dtype)


def _in_proj_kernel(*refs, src_ends):
    x_refs = refs[:len(src_ends)]
    nw_ref, w_ref, wu_ref, o_ref, wub_ref = refs[len(src_ends):]
    h = _rms_rows(_read_stacked_rows(x_refs, src_ends), nw_ref[...]).astype(BF16)
    o_ref[...] = jnp.dot(h, w_ref[...], preferred_element_type=F32)
    wub_ref[...] = wu_ref[...].astype(BF16)


def in_proj_and_w_up_cast(x_sources, nw, w, w_up, layer, *, tm):
    x_specs, src_ends = _stacked_row_specs(x_sources, tm)
    m = src_ends[-1] * tm
    k, n = w.shape[1:]
    ff = w_up.shape[-1]
    slab = 256
    last = ff // slab - 1
    assert src_ends[-1] > last
    return pl.pallas_call(
        functools.partial(_in_proj_kernel, src_ends=src_ends),
        out_shape=(jax.ShapeDtypeStruct((m, n), F32), jax.ShapeDtypeStruct((k, ff), BF16)),
        grid=(src_ends[-1],),
        in_specs=x_specs + [
            pl.BlockSpec((None, 1, k), lambda i: (layer, 0, 0)),
            pl.BlockSpec((None, k, n), lambda i: (layer, 0, 0), pipeline_mode=pl.Buffered(1)),
            pl.BlockSpec((None, k, slab), lambda i: (layer, 0, jnp.minimum(i, last))),
        ],
        out_specs=(
            pl.BlockSpec((tm, n), lambda i: (i, 0)),
            pl.BlockSpec((k, slab), lambda i: (0, jnp.minimum(i, last))),
        ),
        compiler_params=_cparams(("arbitrary",)),
        name="in_proj",
    )(*x_sources, nw, w, w_up)


def norm_matmul_stacked(x, nw, w, layer, prev, *, tm, tn):
    m, k = x.shape
    n = w.shape[-1]
    extra = _carried(prev)
    return pl.pallas_call(
        _skip_refs(functools.partial(_norm_matmul_kernel, tm=tm), 3, len(extra)),
        out_shape=jax.ShapeDtypeStruct((DEPTH, m, n), F32),
        grid=(m // tm, n // tn),
        in_specs=[
            pl.BlockSpec((tm, k), lambda i, j: (i, 0)),
            pl.BlockSpec((None, 1, k), lambda i, j: (layer, 0, 0)),
            pl.BlockSpec((None, k, tn), lambda i, j: (layer, 0, j)),
        ] + [_ANY] * len(extra),
        out_specs=pl.BlockSpec((None, tm, tn), lambda i, j: (layer, i, j)),
        scratch_shapes=[pltpu.VMEM((tm, k), BF16)],
        input_output_aliases={3: 0} if extra else {},
        compiler_params=_cparams(("parallel", "arbitrary")),
        name="norm_matmul_stacked",
    )(x, nw, w, *extra)


def _resident_weight_spec(w, layer):
    return pl.BlockSpec((None,) + w.shape[1:], lambda i: (layer, 0, 0), pipeline_mode=pl.Buffered(1))


def _matmul_resident_kernel(a_ref, w_ref, wd_ref, o_ref, wdb_ref, wb_ref):
    _cast_weight_once(w_ref, wb_ref)
    o_ref[...] = jnp.dot(a_ref[...], wb_ref[...], preferred_element_type=F32).astype(o_ref.dtype)
    wdb_ref[...] = wd_ref[...].astype(BF16)


def matmul_resident_and_w_down_cast(a, w, w_down, layer, *, tm):
    m, k = a.shape
    n = w.shape[-1]
    ff = w_down.shape[1]
    slab = 512
    last = ff // slab - 1
    assert m // tm > last
    return pl.pallas_call(
        _matmul_resident_kernel,
        out_shape=(jax.ShapeDtypeStruct((m, n), BF16), jax.ShapeDtypeStruct((ff, n), BF16)),
        grid=(m // tm,),
        in_specs=[
            pl.BlockSpec((tm, k), lambda i: (i, 0)),
            _resident_weight_spec(w, layer),
            pl.BlockSpec((None, slab, n), lambda i: (layer, jnp.minimum(i, last), 0)),
        ],
        out_specs=(
            pl.BlockSpec((tm, n), lambda i: (i, 0)),
            pl.BlockSpec((slab, n), lambda i: (jnp.minimum(i, last), 0)),
        ),
        scratch_shapes=[pltpu.VMEM((k, n), BF16)],
        compiler_params=_cparams(("arbitrary",)),
        name="matmul_resident",
    )(a, w, w_down)


def _matmul_res_resident_kernel(*refs, n_parts, res_ends):
    a_refs = refs[:n_parts]
    w_ref = refs[n_parts]
    r_refs = refs[n_parts + 1:n_parts + 1 + len(res_ends)]
    nw_ref, o_ref, h_ref, wb_ref = refs[n_parts + 1 + len(res_ends):]
    _cast_weight_once(w_ref, wb_ref)
    half = o_ref.shape[0] // 2
    for rows in (slice(0, half), slice(half, 2 * half)):
        acc = _read_stacked_rows(r_refs, res_ends, rows)
        k0 = 0
        for a_ref in a_refs:
            k1 = k0 + a_ref.shape[1]
            acc = acc + jnp.dot(a_ref[rows, :], wb_ref[k0:k1, :], preferred_element_type=F32)
            k0 = k1
        o_ref[rows, :] = acc
        h_ref[rows, :] = _rms_rows(acc, nw_ref[...]).astype(BF16)


def matmul_residual_resident(a_parts, w, res_sources, nw, layer, *, tm):
    res_specs, res_ends = _stacked_row_specs(res_sources, tm)
    m, n = res_ends[-1] * tm, w.shape[-1]
    k = sum(a.shape[1] for a in a_parts)
    row_tile = lambda width: pl.BlockSpec((tm, width), lambda i: (i, 0))
    return pl.pallas_call(
        functools.partial(_matmul_res_resident_kernel, n_parts=len(a_parts), res_ends=res_ends),
        out_shape=(jax.ShapeDtypeStruct((m, n), F32), jax.ShapeDtypeStruct((m, n), BF16)),
        grid=(m // tm,),
        in_specs=[row_tile(a.shape[1]) for a in a_parts] + [_resident_weight_spec(w, layer)]
        + res_specs + [pl.BlockSpec((None, 1, n), lambda i: (layer, 0, 0))],
        out_specs=(row_tile(n), row_tile(n)),
        scratch_shapes=[pltpu.VMEM((k, n), BF16)],
        compiler_params=_cparams(("arbitrary",)),
        name="matmul_residual_resident",
    )(*a_parts, w, *res_sources, nw)


MLP_TF = 1024
MLP_TN = 512
MLP_KC = 4096
MLP_UP_STEPS = D_FF // MLP_TF
MLP_K_STEPS = D_FF // MLP_KC
MLP_DOWN_STEPS = (D_MODEL // MLP_TN) * MLP_K_STEPS


def _mlp_kernel(h_ref, x_ref, wu_ref, wd_ref, o_ref, hid_ref):
    s = pl.program_id(1)

    @pl.when(s < MLP_UP_STEPS)
    def _():
        hid = jnp.dot(h_ref[...], wu_ref[...], preferred_element_type=F32)
        per_chunk = MLP_KC // MLP_TF
        c0 = pl.multiple_of((s % per_chunk) * MLP_TF, MLP_TF)
        hid_ref[s // per_chunk, :, pl.ds(c0, MLP_TF)] = jnp.square(jnp.maximum(hid, 0.0)).astype(BF16)

    @pl.when(s >= MLP_UP_STEPS)
    def _():
        kc = (s - MLP_UP_STEPS) % MLP_K_STEPS
        part = jnp.dot(hid_ref[kc], wd_ref[...], preferred_element_type=F32)

        @pl.when(kc == 0)
        def _():
            o_ref[...] = x_ref[...] + part

        @pl.when(kc != 0)
        def _():
            o_ref[...] += part


def mlp_residual(h, x, w_up, w_down, *, tm):
    m, k = x.shape
    down = lambda s: jnp.maximum(s - MLP_UP_STEPS, 0)
    out_tile = pl.BlockSpec((tm, MLP_TN), lambda i, s: (i, down(s) // MLP_K_STEPS))
    return pl.pallas_call(
        _mlp_kernel,
        out_shape=jax.ShapeDtypeStruct((m, k), F32),
        grid=(m // tm, MLP_UP_STEPS + MLP_DOWN_STEPS),
        in_specs=[
            pl.BlockSpec((tm, k), lambda i, s: (i, 0)),
            out_tile,
            pl.BlockSpec((k, MLP_TF), lambda i, s: (0, jnp.minimum(s, MLP_UP_STEPS - 1))),
            pl.BlockSpec((MLP_KC, MLP_TN), lambda i, s: (down(s) % MLP_K_STEPS, down(s) // MLP_K_STEPS)),
        ],
        out_specs=out_tile,
        scratch_shapes=[pltpu.VMEM((MLP_K_STEPS, tm, MLP_KC), BF16)],
        compiler_params=_cparams(("parallel", "arbitrary")),
        name="mlp_residual",
    )(h, x, w_up, w_down)


def _rotate(x, cos, sin):
    half = x.shape[-1] // 2
    x1, x2 = x[:, :half], x[:, half:]
    return jnp.concatenate([x1 * cos - x2 * sin, x2 * cos + x1 * sin], axis=-1)


def _head_norm_gate(o, g, rnw):
    o = o * lax.rsqrt(jnp.mean(o * o, axis=-1, keepdims=True) + EPS)
    return (g * jax.nn.sigmoid(g)) * (o * rnw)


def _dot_t_lhs(a, b):
    return lax.dot_general(a, b, (((0,), (0,)), ((), ())), preferred_element_type=F32)


def _dot_t_rhs(a, b):
    return lax.dot_general(a, b, (((1,), (1,)), ((), ())), preferred_element_type=F32)


RET_P_ROWS = 512


def _ret_prompt_kernel(q_ref, k_ref, v_ref, g_ref, cos_ref, sin_ref, din_ref, dq_ref, dk_ref,
                       dc_ref, rnw_ref, o_ref, s_out_ref, s_ref):
    @pl.when(pl.program_id(1) == 0)
    def _():
        s_ref[...] = jnp.zeros_like(s_ref)

    for c in range(RET_P_ROWS // RET_CHUNK):
        rows = slice(c * RET_CHUNK, (c + 1) * RET_CHUNK)
        cos, sin = cos_ref[rows, :], sin_ref[rows, :]
        for h in range(RET_HEADS):
            cols = slice(h * RET_HEAD_DIM, (h + 1) * RET_HEAD_DIM)
            dq = jnp.concatenate([dq_ref[h], dq_ref[h]], axis=-1)
            dk = jnp.concatenate([dk_ref[h], dk_ref[h]], axis=-1)
            q = _rotate(q_ref[rows, cols], cos, sin)
            k = _rotate(k_ref[rows, cols], cos, sin) * (RET_HEAD_DIM ** -0.5)
            vb = v_ref[rows, cols].astype(BF16)
            s = s_ref[h]
            scores = _dot_t_rhs(q.astype(BF16), k.astype(BF16)) * din_ref[h]
            o = jnp.dot(scores.astype(BF16), vb, preferred_element_type=F32)
            o = o + jnp.dot((q * dq).astype(BF16), s.astype(BF16), preferred_element_type=F32)
            s_ref[h] = s * dc_ref[h] + _dot_t_lhs((k * dk).astype(BF16), vb)
            o_ref[rows, cols] = _head_norm_gate(o, g_ref[rows, cols], rnw_ref[:, cols]).astype(o_ref.dtype)

    @pl.when(pl.program_id(1) == pl.num_programs(1) - 1)
    def _():
        s_out_ref[...] = s_ref[...]


def retention_prompt(z, cos, sin, din, dq, dk, dc, rnw, layer, state_prev):
    nt = SEQ // RET_P_ROWS
    extra = _carried(state_prev)
    zspec = lambda part: pl.BlockSpec((RET_P_ROWS, D_RET), lambda b, t: (b * nt + t, part))
    whole = lambda a: pl.BlockSpec(a.shape, lambda b, t: (0,) * a.ndim)
    state_block = (RET_HEADS, RET_HEAD_DIM, RET_HEAD_DIM)
    return pl.pallas_call(
        _skip_refs(_ret_prompt_kernel, 11, len(extra)),
        out_shape=(jax.ShapeDtypeStruct((M_ROWS, D_RET), BF16),
                   jax.ShapeDtypeStruct((DEPTH, BATCH) + state_block, F32)),
        grid=(BATCH, nt),
        in_specs=[
            zspec(0), zspec(1), zspec(2), zspec(3),
            pl.BlockSpec((RET_P_ROWS, RET_HEAD_DIM // 2), lambda b, t: (t, 0)),
            pl.BlockSpec((RET_P_ROWS, RET_HEAD_DIM // 2), lambda b, t: (t, 0)),
            whole(din), whole(dq), whole(dk), whole(dc),
            pl.BlockSpec((None, 1, D_RET), lambda b, t: (layer, 0, 0)),
        ] + [_ANY] * len(extra),
        out_specs=(
            pl.BlockSpec((RET_P_ROWS, D_RET), lambda b, t: (b * nt + t, 0)),
            pl.BlockSpec((None, None) + state_block, lambda b, t: (layer, b, 0, 0, 0)),
        ),
        scratch_shapes=[pltpu.VMEM(state_block, F32)],
        input_output_aliases={11: 1} if extra else {},
        compiler_params=_cparams(("parallel", "arbitrary")),
        name="retention_prompt",
    )(z, z, z, z, cos, sin, din, dq, dk, dc, rnw, *extra)


def _ret_sample_head(h, dec_ref, z_ref, cos, sin, st_ref, rnw_ref, o_ref, st_out_ref):
    cols = slice(h * RET_HEAD_DIM, (h + 1) * RET_HEAD_DIM)
    part = lambda p: z_ref[:, p * D_RET + h * RET_HEAD_DIM:p * D_RET + (h + 1) * RET_HEAD_DIM]
    q = _rotate(part(0), cos, sin)
    k = _rotate(part(1), cos, sin) * (RET_HEAD_DIM ** -0.5)
    v = part(2)
    g = part(3)
    slab = lambda a, t: a[t * S_TILE_B:(t + 1) * S_TILE_B, :]

    intra = []
    for t in range(DEC_SEQ):
        acc = None
        for j in range(t + 1):
            w = jnp.sum(slab(q, t) * slab(k, j), axis=-1, keepdims=True) * dec_ref[h, t * DEC_SEQ + j]
            acc = w * slab(v, j) if acc is None else acc + w * slab(v, j)
        intra.append(acc)
    o = jnp.concatenate(intra, axis=0)

    dq_rows = jnp.concatenate(
        [jnp.full((S_TILE_B, 1), 1.0, F32) * dec_ref[h, 16 + t] for t in range(DEC_SEQ)], axis=0)
    dk_rows = jnp.concatenate(
        [jnp.full((S_TILE_B, 1), 1.0, F32) * dec_ref[h, 20 + t] for t in range(DEC_SEQ)], axis=0)
    qd = (q * dq_rows).astype(BF16)
    kd = k * dk_rows
    vb = v.astype(BF16)
    dchunk = dec_ref[h, 24]
    row_b = lax.broadcasted_iota(jnp.int32, (S_TILE_ROWS, 1), 0) % S_TILE_B
    for b in range(S_TILE_B):
        mine = row_b == b
        s = st_ref[b, h]
        o = o + jnp.where(mine, jnp.dot(qd, s.astype(BF16), preferred_element_type=F32), 0.0)
        st_out_ref[b, h] = s * dchunk + _dot_t_lhs(jnp.where(mine, kd, 0.0).astype(BF16), vb)
    o_ref[:, cols] = _head_norm_gate(o, g, rnw_ref[:, cols]).astype(o_ref.dtype)


def _ret_sample_kernel(dec_ref, z_ref, cos_ref, sin_ref, st_ref, rnw_ref, o_ref, st_out_ref):
    cos, sin = cos_ref[...], sin_ref[...]
    for h in range(RET_HEADS):
        _ret_sample_head(h, dec_ref, z_ref, cos, sin, st_ref, rnw_ref, o_ref, st_out_ref)


def retention_sample(dec, z, cos_s, sin_s, state_ret, rnw, mix, layer, state_prev):
    extra = [mix] + _carried(state_prev)
    state_spec = pl.BlockSpec((None, S_TILE_B, RET_HEADS, RET_HEAD_DIM, RET_HEAD_DIM),
                              lambda i: (layer, i, 0, 0, 0))
    return pl.pallas_call(
        _skip_refs(_ret_sample_kernel, 6, len(extra)),
        out_shape=(jax.ShapeDtypeStruct(mix.shape, mix.dtype),
                   jax.ShapeDtypeStruct(state_ret.shape, F32)),
        grid=(N_S_TILES,),
        in_specs=[
            pl.BlockSpec(memory_space=pltpu.SMEM),
            pl.BlockSpec((S_TILE_ROWS, D_IN), lambda i: (S_BLK0 + i, 0)),
            pl.BlockSpec((S_TILE_ROWS, RET_HEAD_DIM // 2), lambda i: (0, 0)),
            pl.BlockSpec((S_TILE_ROWS, RET_HEAD_DIM // 2), lambda i: (0, 0)),
            state_spec,
            pl.BlockSpec((None, 1, D_RET), lambda i: (layer, 0, 0)),
        ] + [_ANY] * len(extra),
        out_specs=(pl.BlockSpec((S_TILE_ROWS, D_RET), lambda i: (S_BLK0 + i, 0)), state_spec),
        input_output_aliases={6 + n: n for n in range(len(extra))},
        compiler_params=_cparams(("arbitrary",)),
        name="retention_sample",
    )(dec, z, cos_s, sin_s, state_ret, rnw, *extra)


POOL_P_ROWS = 512
POOL_HALO = 16


def _shift_rows(a, s):
    return pltpu.roll(a, s, axis=0)


def _pool_prompt_kernel(u_ref, pw_ref, ps_ref, o_ref, buf_ref, halo_ref):
    t = pl.program_id(1)

    @pl.when(t == 0)
    def _():
        halo_ref[...] = jnp.zeros_like(halo_ref)

    pos = t * POOL_P_ROWS + lax.broadcasted_iota(jnp.int32, (POOL_P_ROWS, 1), 0)
    for g, w in enumerate(POOL_WINDOWS):
        cols = slice(g * POOL_GROUP_DIM, (g + 1) * POOL_GROUP_DIM)
        u = u_ref[:, cols]
        a = jnp.concatenate([halo_ref[:, cols], u], axis=0)
        s = 1
        while s < w:
            a = a + _shift_rows(a, s)
            s *= 2
        cnt = jnp.minimum(pos + 1, w).astype(F32)
        pooled = a[POOL_HALO:, :] / cnt - u
        pm = jnp.dot(pooled.astype(BF16), pw_ref[g].astype(BF16), preferred_element_type=F32)
        o_ref[:, cols] = (pm * ps_ref[:, cols]).astype(o_ref.dtype)

    halo_ref[...] = u_ref[POOL_P_ROWS - POOL_HALO:, :]

    @pl.when(t == pl.num_programs(1) - 1)
    def _():
        buf_ref[...] = u_ref[POOL_P_ROWS - POOL_BUF:, :]


def pool_prompt(z, pool_w, pool_scale, layer, buf_prev):
    nt = SEQ // POOL_P_ROWS
    extra = _carried(buf_prev)
    return pl.pallas_call(
        _skip_refs(_pool_prompt_kernel, 3, len(extra)),
        out_shape=(jax.ShapeDtypeStruct((M_ROWS, D_POOL), BF16),
                   jax.ShapeDtypeStruct((DEPTH, BATCH, POOL_BUF, D_POOL), F32)),
        grid=(BATCH, nt),
        in_specs=[
            pl.BlockSpec((POOL_P_ROWS, D_POOL), lambda b, t: (b * nt + t, 4)),
            pl.BlockSpec((None, 4, POOL_GROUP_DIM, POOL_GROUP_DIM), lambda b, t: (layer, 0, 0, 0)),
            pl.BlockSpec((None, 1, D_POOL), lambda b, t: (layer, 0, 0)),
        ] + [_ANY] * len(extra),
        out_specs=(
            pl.BlockSpec((POOL_P_ROWS, D_POOL), lambda b, t: (b * nt + t, 0)),
            pl.BlockSpec((None, None, POOL_BUF, D_POOL), lambda b, t: (layer, b, 0, 0)),
        ),
        scratch_shapes=[pltpu.VMEM((POOL_HALO, D_POOL), F32)],
        input_output_aliases={3: 1} if extra else {},
        compiler_params=_cparams(("parallel", "arbitrary")),
        name="pool_prompt",
    )(z, pool_w, pool_scale, *extra)


def _pool_sample_kernel(u_ref, buf_ref, pw_ref, ps_ref, o_ref, nbuf_ref):

    def ext(r, cols):
        if r < POOL_BUF:
            return buf_ref[:, r, cols]
        return u_ref[(r - POOL_BUF) * S_TILE_B:(r - POOL_BUF + 1) * S_TILE_B, cols]

    for g, w in enumerate(POOL_WINDOWS):
        cols = slice(g * POOL_GROUP_DIM, (g + 1) * POOL_GROUP_DIM)
        pooled = []
        for t in range(DEC_SEQ):
            win = ext(POOL_BUF + t, cols)
            for r in range(POOL_BUF + t - w + 1, POOL_BUF + t):
                win = win + ext(r, cols)
            cnt = float(min(t + PAST_LEN + 1, w))
            pooled.append(win / cnt - ext(POOL_BUF + t, cols))
        pooled = jnp.concatenate(pooled, axis=0)
        pm = jnp.dot(pooled.astype(BF16), pw_ref[g].astype(BF16), preferred_element_type=F32)
        o_ref[:, cols] = (pm * ps_ref[:, cols]).astype(o_ref.dtype)

    full = slice(0, D_POOL)
    for r in range(POOL_BUF):
        nbuf_ref[:, r, :] = ext(r + DEC_SEQ, full)


def pool_sample(z, state_pool, pool_w, pool_scale, mix, layer, buf_prev):
    extra = [mix] + _carried(buf_prev)
    return pl.pallas_call(
        _skip_refs(_pool_sample_kernel, 4, len(extra)),
        out_shape=(jax.ShapeDtypeStruct(mix.shape, mix.dtype),
                   jax.ShapeDtypeStruct(state_pool.shape, F32)),
        grid=(N_S_TILES,),
        in_specs=[
            pl.BlockSpec((S_TILE_ROWS, D_POOL), lambda i: (S_BLK0 + i, 4)),
            pl.BlockSpec((None, S_TILE_B, POOL_BUF, D_POOL), lambda i: (layer, i, 0, 0)),
            pl.BlockSpec((None, 4, POOL_GROUP_DIM, POOL_GROUP_DIM), lambda i: (layer, 0, 0, 0)),
            pl.BlockSpec((None, 1, D_POOL), lambda i: (layer, 0, 0)),
        ] + [_ANY] * len(extra),
        out_specs=(
            pl.BlockSpec((S_TILE_ROWS, D_POOL), lambda i: (S_BLK0 + i, 0)),
            pl.BlockSpec((None, S_TILE_B, POOL_BUF, D_POOL), lambda i: (layer, i, 0, 0)),
        ),
        input_output_aliases={4 + n: n for n in range(len(extra))},
        compiler_params=_cparams(("arbitrary",)),
        name="pool_sample",
    )(z, state_pool, pool_w, pool_scale, *extra)


def _softmax_rows(s):
    m = jnp.max(s, axis=-1, keepdims=True)
    e = jnp.exp(s - m)
    return e / jnp.sum(e, axis=-1, keepdims=True)


XATTN_P_ROWS = 1024


def _xattn_prompt_kernel(q_ref, mk_ref, mv_ref, o_ref):
    for h in range(MEM_HEADS):
        cols = slice(h * MEM_HEAD_DIM, (h + 1) * MEM_HEAD_DIM)
        s = _dot_t_rhs(q_ref[:, cols], mk_ref[:, cols].astype(BF16)) * (MEM_HEAD_DIM ** -0.5)
        p = _softmax_rows(s)
        o_ref[:, cols] = jnp.dot(p.astype(BF16), mv_ref[:, cols].astype(BF16),
                                 preferred_element_type=F32).astype(o_ref.dtype)


def xattn_prompt(qx, mk, mv, layer):
    nt = SEQ // XATTN_P_ROWS
    return pl.pallas_call(
        _xattn_prompt_kernel,
        out_shape=jax.ShapeDtypeStruct((M_ROWS, D_MODEL), BF16),
        grid=(BATCH, nt),
        in_specs=[
            pl.BlockSpec((XATTN_P_ROWS, D_MODEL), lambda b, t: (b * nt + t, 0)),
            pl.BlockSpec((None, N_MEM, D_MODEL), lambda b, t: (layer, b, 0)),
            pl.BlockSpec((None, N_MEM, D_MODEL), lambda b, t: (layer, b, 0)),
        ],
        out_specs=pl.BlockSpec((XATTN_P_ROWS, D_MODEL), lambda b, t: (b * nt + t, 0)),
        compiler_params=_cparams(("parallel", "arbitrary")),
        name="xattn_prompt",
    )(qx, mk, mv)


XATTN_S_B = 4
XATTN_S_ROWS = MEM_HEADS * DEC_SEQ
XATTN_S_KEYS = N_MEM * MEM_HEADS


def _xattn_sample_kernel(q_ref, k_ref, v_ref, o_ref):
    row_h = lax.broadcasted_iota(jnp.int32, (XATTN_S_ROWS, XATTN_S_KEYS), 0) // DEC_SEQ
    col_h = lax.broadcasted_iota(jnp.int32, (XATTN_S_ROWS, XATTN_S_KEYS), 1) % MEM_HEADS
    same_head = row_h == col_h
    for bb in range(XATTN_S_B):
        k2 = k_ref[bb].reshape(XATTN_S_KEYS, MEM_HEAD_DIM).astype(BF16)
        v2 = v_ref[bb].reshape(XATTN_S_KEYS, MEM_HEAD_DIM).astype(BF16)
        s = _dot_t_rhs(q_ref[bb], k2) * (MEM_HEAD_DIM ** -0.5)
        s = jnp.where(same_head, s, -jnp.inf)
        p = _softmax_rows(s)
        o_ref[bb] = jnp.dot(p.astype(BF16), v2, preferred_element_type=F32).astype(o_ref.dtype)


def xattn_sample(qs, cache_k, cache_v, layer):
    kv_spec = pl.BlockSpec((None, XATTN_S_B, N_MEM, MEM_HEADS, MEM_HEAD_DIM),
                           lambda i: (layer, i, 0, 0, 0))
    qo_spec = pl.BlockSpec((XATTN_S_B, XATTN_S_ROWS, MEM_HEAD_DIM), lambda i: (i, 0, 0))
    return pl.pallas_call(
        _xattn_sample_kernel,
        out_shape=jax.ShapeDtypeStruct(qs.shape, BF16),
        grid=(DEC_BATCH // XATTN_S_B,),
        in_specs=[qo_spec, kv_spec, kv_spec],
        out_specs=qo_spec,
        compiler_params=_cparams(("parallel",)),
        name="xattn_sample",
    )(qs, cache_k, cache_v)


def _sample_rows_to_heads(a):
    a = a.reshape(N_S_TILES, DEC_SEQ, S_TILE_B, MEM_HEADS, MEM_HEAD_DIM)
    return a.transpose(0, 2, 3, 1, 4).reshape(DEC_BATCH, XATTN_S_ROWS, MEM_HEAD_DIM)


def _heads_to_sample_rows(a):
    a = a.reshape(N_S_TILES, S_TILE_B, MEM_HEADS, DEC_SEQ, MEM_HEAD_DIM)
    return a.transpose(0, 3, 1, 2, 4).reshape(S_ROWS, D_MODEL)


FINAL_ROWS = 512


def _final_norm_kernel(x_ref, nw_ref, yp_ref, ys_ref):
    def run(dst_ref):
        def body(rows):
            dst_ref[rows, :] = _rms_rows(x_ref[rows, :], nw_ref[...])
        _for_row_chunks(FINAL_ROWS, _norm_chunk(FINAL_ROWS), body)

    is_prompt = pl.program_id(0) < P_ROWS // FINAL_ROWS
    pl.when(is_prompt)(lambda: run(yp_ref))
    pl.when(jnp.logical_not(is_prompt))(lambda: run(ys_ref))


def final_norm(x, nw):
    n_p = P_ROWS // FINAL_ROWS
    return pl.pallas_call(
        _final_norm_kernel,
        out_shape=(jax.ShapeDtypeStruct((P_ROWS, D_MODEL), F32),
                   jax.ShapeDtypeStruct((S_ROWS, D_MODEL), F32)),
        grid=(M_ROWS // FINAL_ROWS,),
        in_specs=[
            pl.BlockSpec((FINAL_ROWS, D_MODEL), lambda i: (i, 0)),
            pl.BlockSpec((1, D_MODEL), lambda i: (0, 0)),
        ],
        out_specs=(
            pl.BlockSpec((FINAL_ROWS, D_MODEL), lambda i: (jnp.minimum(i, n_p - 1), 0)),
            pl.BlockSpec((S_ROWS, D_MODEL), lambda i: (0, 0)),
        ),
        compiler_params=_cparams(("arbitrary",)),
        name="final_norm",
    )(x, nw)


def _rope_tables(pos):
    half = RET_HEAD_DIM // 2
    inv = ROPE_BASE ** (-jnp.arange(half, dtype=F32) / half)
    ang = pos.astype(F32)[:, None] * inv[None, :]
    return jnp.cos(ang), jnp.sin(ang)


def _log_gamma():
    return jnp.log1p(-jnp.exp2(-5.0 - jnp.arange(RET_HEADS, dtype=F32)))


def _decay_tables(chunk):
    lg = _log_gamma()
    idx = jnp.arange(chunk, dtype=F32)
    diff = idx[:, None] - idx[None, :]
    decay_in = jnp.where(diff[None] >= 0.0,
                         jnp.exp(lg[:, None, None] * jnp.maximum(diff, 0.0)[None]), 0.0)
    decay_q = jnp.exp(lg[:, None] * (idx[None, :] + 1.0))
    decay_k = jnp.exp(lg[:, None] * (chunk - 1.0 - idx[None, :]))
    decay_chunk = jnp.exp(lg * chunk)
    return decay_in, decay_q, decay_k, decay_chunk


def _to_sample_rows(a):
    d = a.shape[-1]
    return a.reshape(N_S_TILES, S_TILE_B, DEC_SEQ, d).transpose(0, 2, 1, 3).reshape(S_ROWS, d)


def _from_sample_rows(a):
    d = a.shape[-1]
    return a.reshape(N_S_TILES, DEC_SEQ, S_TILE_B, d).transpose(0, 2, 1, 3).reshape(DEC_BATCH, DEC_SEQ, d)


def kernel(x_prompt, x_sample, mem_prompt, state_ret, state_pool, cache_mem_k, cache_mem_v,
           attn_norm_w, w_in, ret_norm_w, pool_w, pool_scale, w_out, xattn_norm_w, mem_norm_w,
           w_xq, w_mk, w_mv, w_xo, mlp_norm_w, w_up, w_down, final_norm_w):
    cos_p, sin_p = _rope_tables(jnp.arange(SEQ))
    cos_s, sin_s = _rope_tables(jnp.arange(DEC_SEQ) + PAST_LEN)
    cos_s = jnp.repeat(cos_s, S_TILE_B, axis=0)
    sin_s = jnp.repeat(sin_s, S_TILE_B, axis=0)
    din_p, dq_p, dk_p, dc_p = _decay_tables(RET_CHUNK)
    half = RET_HEAD_DIM // 2
    dq_p = jnp.broadcast_to(dq_p[:, :, None], (RET_HEADS, RET_CHUNK, half))
    dk_p = jnp.broadcast_to(dk_p[:, :, None], (RET_HEADS, RET_CHUNK, half))
    dc_p = jnp.broadcast_to(dc_p[:, None, None], (RET_HEADS, 1, RET_HEAD_DIM))
    din_s, dq_s, dk_s, dc_s = _decay_tables(DEC_SEQ)
    dec_s = jnp.concatenate([din_s.reshape(RET_HEADS, DEC_SEQ * DEC_SEQ), dq_s, dk_s,
                             dc_s[:, None], jnp.zeros((RET_HEADS, 7), F32)], axis=1)

    row3 = lambda a: a.reshape(DEPTH, 1, a.shape[-1])
    attn_nw, xattn_nw, mem_nw, mlp_nw = map(row3, (attn_norm_w, xattn_norm_w, mem_norm_w, mlp_norm_w))
    ret_nw, pool_sc = row3(ret_norm_w), row3(pool_scale)
    w_in_b = w_in.astype(BF16)

    x_sources = [x_prompt.reshape(P_ROWS, D_MODEL), _to_sample_rows(x_sample)]
    mem = mem_prompt.reshape(BATCH * N_MEM, D_MODEL)

    ret_p = buf_p = mk_p = mv_p = ret_s = buf_s = None
    for l in range(DEPTH):
        z, w_up_b = in_proj_and_w_up_cast(x_sources, attn_nw, w_in_b, w_up, l, tm=TM_IN_PROJ)
        mix_r, ret_p = retention_prompt(z, cos_p, sin_p, din_p, dq_p, dk_p, dc_p, ret_nw, l, ret_p)
        mix_r, ret_s = retention_sample(dec_s, z, cos_s, sin_s, state_ret, ret_nw, mix_r, l, ret_s)
        mix_p, buf_p = pool_prompt(z, pool_w, pool_sc, l, buf_p)
        mix_p, buf_s = pool_sample(z, state_pool, pool_w, pool_sc, mix_p, l, buf_s)
        x, h = matmul_residual_resident([mix_r, mix_p], w_out, x_sources, xattn_nw, l, tm=TM_RESIDENT)

        mk_p = norm_matmul_stacked(mem, mem_nw, w_mk, l, mk_p, tm=1024, tn=512)
        mv_p = norm_matmul_stacked(mem, mem_nw, w_mv, l, mv_p, tm=1024, tn=512)
        qx, w_down_b = matmul_resident_and_w_down_cast(h, w_xq, w_down, l, tm=TM_RESIDENT)
        att = xattn_prompt(qx, mk_p, mv_p, l)
        att_s = xattn_sample(_sample_rows_to_heads(qx[P_ROWS:]), cache_mem_k, cache_mem_v, l)
        att = lax.dynamic_update_slice(att, _heads_to_sample_rows(att_s), (P_ROWS, 0))
        x, h = matmul_residual_resident([att], w_xo, [x], mlp_nw, l, tm=TM_RESIDENT)
        x = mlp_residual(h, x, w_up_b, w_down_b, tm=TM)
        x_sources = [x]

    y_p, y_s = final_norm(x, final_norm_w.reshape(1, D_MODEL))
    mem_shape = (DEPTH, BATCH, N_MEM, MEM_HEADS, MEM_HEAD_DIM)
    return (y_p.reshape(BATCH, SEQ, D_MODEL), _from_sample_rows(y_s),
            ret_p, buf_p, mk_p.reshape(mem_shape), mv_p.reshape(mem_shape), ret_s, buf_s)
```

```python
import functools

import jax
import jax.numpy as jnp
from jax import lax
from jax.experimental import pallas as pl
from jax.experimental.pallas import tpu as pltpu

D_MODEL = 2048
BATCH = 4
SEQ = 2048
DEPTH = 2
DEC_BATCH = 128
DEC_SEQ = 4
PAST_LEN = 16384
D_RET = 1024
RET_HEADS = 4
RET_HEAD_DIM = 256
D_POOL = 1024
POOL_WINDOWS = (2, 4, 8, 16)
POOL_GROUP_DIM = 256
POOL_BUF = 15
D_IN = 5120
N_MEM = 256
MEM_HEADS = 4
MEM_HEAD_DIM = 512
D_FF = 8192
RET_CHUNK = 128
ROPE_BASE = 10000.0
EPS = 1e-6

F32 = jnp.float32
BF16 = jnp.bfloat16

P_ROWS = BATCH * SEQ
S_ROWS = DEC_BATCH * DEC_SEQ
M_ROWS = P_ROWS + S_ROWS
S_TILE_B = 8
S_TILE_ROWS = S_TILE_B * DEC_SEQ
N_S_TILES = DEC_BATCH // S_TILE_B
S_BLK0 = P_ROWS // S_TILE_ROWS

TM = 1088
TM_RESIDENT = 512
TM_IN_PROJ = 256
VMEM_LIMIT = 58 * 1024 * 1024


def _cparams(sem):
    return pltpu.CompilerParams(dimension_semantics=sem, vmem_limit_bytes=VMEM_LIMIT)


_ANY = pl.BlockSpec(memory_space=pl.ANY)


def _skip_refs(kernel_fn, start, count):
    def wrapped(*refs):
        return kernel_fn(*refs[:start], *refs[start + count:])
    return wrapped


def _carried(prev):
    return [] if prev is None else [prev]


def _stacked_row_specs(sources, tm):
    specs, ends, start = [], [], 0
    for a in sources:
        n = a.shape[0] // tm
        specs.append(pl.BlockSpec((tm, a.shape[1]),
                                  lambda i, start=start, n=n: (jnp.clip(i - start, 0, n - 1), 0)))
        start += n
        ends.append(start)
    return specs, tuple(ends)


def _read_stacked_rows(refs, ends, rows=slice(None)):
    i = pl.program_id(0)
    x = refs[-1][rows, :]
    for ref, end in zip(reversed(refs[:-1]), reversed(ends[:-1])):
        x = jnp.where(i < end, ref[rows, :], x)
    return x


def _rms_rows(x, nw):
    ms = jnp.mean(x * x, axis=-1, keepdims=True)
    return x * lax.rsqrt(ms + EPS) * nw


def _for_row_chunks(n_rows, chunk, body):
    def step(c, carry):
        body(pl.ds(pl.multiple_of(c * chunk, chunk), chunk))
        return carry
    lax.fori_loop(0, n_rows // chunk, step, 0)


def _norm_chunk(n_rows):
    return 272 if n_rows % 272 == 0 else 128


def _cast_weight_once(w_ref, wb_ref):
    @pl.when(pl.program_id(0) == 0)
    def _():
        def body(rows):
            wb_ref[rows, :] = w_ref[rows, :].astype(BF16)
        _for_row_chunks(w_ref.shape[0], 256, body)


def _mem_kv_kernel(x_ref, nw_ref, wk_ref, wv_ref, k_ref, v_ref, h_ref, *, tm):
    @pl.when(pl.program_id(1) == 0)
    def _():
        def body(rows):
            h_ref[rows, :] = _rms_rows(x_ref[rows, :], nw_ref[...]).astype(BF16)
        _for_row_chunks(tm, _norm_chunk(tm), body)

    k_ref[...] = jnp.dot(h_ref[...], wk_ref[...].astype(BF16), preferred_element_type=F32)
    v_ref[...] = jnp.dot(h_ref[...], wv_ref[...].astype(BF16), preferred_element_type=F32)


def mem_kv_stacked(x, nw, w_k, w_v, layer, prev_k, prev_v, *, tm, tn):
    m, k = x.shape
    n = w_k.shape[-1]
    extra = _carried(prev_k) + _carried(prev_v)
    w_spec = pl.BlockSpec((None, k, tn), lambda i, j: (layer, 0, j))
    o_spec = pl.BlockSpec((None, tm, tn), lambda i, j: (layer, i, j))
    stacked = jax.ShapeDtypeStruct((DEPTH, m, n), F32)
    return pl.pallas_call(
        _skip_refs(functools.partial(_mem_kv_kernel, tm=tm), 4, len(extra)),
        out_shape=(stacked, stacked),
        grid=(m // tm, n // tn),
        in_specs=[
            pl.BlockSpec((tm, k), lambda i, j: (i, 0)),
            pl.BlockSpec((None, 1, k), lambda i, j: (layer, 0, 0)),
            w_spec, w_spec,
        ] + [_ANY] * len(extra),
        out_specs=(o_spec, o_spec),
        scratch_shapes=[pltpu.VMEM((tm, k), BF16)],
        input_output_aliases={4: 0, 5: 1} if extra else {},
        compiler_params=_cparams(("parallel", "arbitrary")),
        name="mem_kv",
    )(x, nw, w_k, w_v, *extra)


def _in_proj_kernel(*refs, src_ends):
    x_refs = refs[:len(src_ends)]
    nw_ref, w_ref, wu_ref, wd_ref, o_ref, wub_ref, wdb_ref = refs[len(src_ends):]
    h = _rms_rows(_read_stacked_rows(x_refs, src_ends), nw_ref[...]).astype(BF16)
    o_ref[...] = jnp.dot(h, w_ref[...], preferred_element_type=F32)
    wub_ref[...] = wu_ref[...].astype(BF16)
    wdb_ref[...] = wd_ref[...].astype(BF16)


def in_proj_and_mlp_weight_cast(x_sources, nw, w, w_up, w_down, layer, *, tm):
    x_specs, src_ends = _stacked_row_specs(x_sources, tm)
    m = src_ends[-1] * tm
    k, n = w.shape[1:]
    ff = w_up.shape[-1]
    slab = 256
    last = ff // slab - 1
    assert src_ends[-1] > last
    return pl.pallas_call(
        functools.partial(_in_proj_kernel, src_ends=src_ends),
        out_shape=(jax.ShapeDtypeStruct((m, n), F32),
                   jax.ShapeDtypeStruct((k, ff), BF16),
                   jax.ShapeDtypeStruct((ff, k), BF16)),
        grid=(src_ends[-1],),
        in_specs=x_specs + [
            pl.BlockSpec((None, 1, k), lambda i: (layer, 0, 0)),
            pl.BlockSpec((None, k, n), lambda i: (layer, 0, 0), pipeline_mode=pl.Buffered(1)),
            pl.BlockSpec((None, k, slab), lambda i: (layer, 0, jnp.minimum(i, last))),
            pl.BlockSpec((None, slab, k), lambda i: (layer, jnp.minimum(i, last), 0)),
        ],
        out_specs=(
            pl.BlockSpec((tm, n), lambda i: (i, 0)),
            pl.BlockSpec((k, slab), lambda i: (0, jnp.minimum(i, last))),
            pl.BlockSpec((slab, k), lambda i: (jnp.minimum(i, last), 0)),
        ),
        compiler_params=_cparams(("arbitrary",)),
        name="in_proj",
    )(*x_sources, nw, w, w_up, w_down)


def _resident_weight_spec(w, layer):
    return pl.BlockSpec((None,) + w.shape[1:], lambda i: (layer, 0, 0), pipeline_mode=pl.Buffered(1))


def _matmul_resident_kernel(a_ref, w_ref, o_ref, wb_ref):
    _cast_weight_once(w_ref, wb_ref)
    o_ref[...] = jnp.dot(a_ref[...], wb_ref[...], preferred_element_type=F32).astype(o_ref.dtype)


def matmul_resident(a, w, layer, *, tm):
    m, k = a.shape
    n = w.shape[-1]
    return pl.pallas_call(
        _matmul_resident_kernel,
        out_shape=jax.ShapeDtypeStruct((m, n), BF16),
        grid=(m // tm,),
        in_specs=[pl.BlockSpec((tm, k), lambda i: (i, 0)), _resident_weight_spec(w, layer)],
        out_specs=pl.BlockSpec((tm, n), lambda i: (i, 0)),
        scratch_shapes=[pltpu.VMEM((k, n), BF16)],
        compiler_params=_cparams(("arbitrary",)),
        name="matmul_resident",
    )(a, w)


def _matmul_res_resident_kernel(*refs, n_parts, res_ends):
    a_refs = refs[:n_parts]
    w_ref = refs[n_parts]
    r_refs = refs[n_parts + 1:n_parts + 1 + len(res_ends)]
    nw_ref, o_ref, h_ref, wb_ref = refs[n_parts + 1 + len(res_ends):]
    _cast_weight_once(w_ref, wb_ref)
    half = o_ref.shape[0] // 2
    for rows in (slice(0, half), slice(half, 2 * half)):
        acc = _read_stacked_rows(r_refs, res_ends, rows)
        k0 = 0
        for a_ref in a_refs:
            k1 = k0 + a_ref.shape[1]
            acc = acc + jnp.dot(a_ref[rows, :], wb_ref[k0:k1, :], preferred_element_type=F32)
            k0 = k1
        o_ref[rows, :] = acc
        h_ref[rows, :] = _rms_rows(acc, nw_ref[...]).astype(BF16)


def matmul_residual_resident(a_parts, w, res_sources, nw, layer, *, tm):
    res_specs, res_ends = _stacked_row_specs(res_sources, tm)
    m, n = res_ends[-1] * tm, w.shape[-1]
    k = sum(a.shape[1] for a in a_parts)
    row_tile = lambda width: pl.BlockSpec((tm, width), lambda i: (i, 0))
    return pl.pallas_call(
        functools.partial(_matmul_res_resident_kernel, n_parts=len(a_parts), res_ends=res_ends),
        out_shape=(jax.ShapeDtypeStruct((m, n), F32), jax.ShapeDtypeStruct((m, n), BF16)),
        grid=(m // tm,),
        in_specs=[row_tile(a.shape[1]) for a in a_parts] + [_resident_weight_spec(w, layer)]
        + res_specs + [pl.BlockSpec((None, 1, n), lambda i: (layer, 0, 0))],
        out_specs=(row_tile(n), row_tile(n)),
        scratch_shapes=[pltpu.VMEM((k, n), BF16)],
        compiler_params=_cparams(("arbitrary",)),
        name="matmul_residual_resident",
    )(*a_parts, w, *res_sources, nw)


MLP_TF = 1024
MLP_TN = 512
MLP_KC = 4096
MLP_UP_STEPS = D_FF // MLP_TF
MLP_K_STEPS = D_FF // MLP_KC
MLP_DOWN_STEPS = (D_MODEL // MLP_TN) * MLP_K_STEPS


def _mlp_kernel(h_ref, x_ref, wu_ref, wd_ref, o_ref, hid_ref):
    s = pl.program_id(1)

    @pl.when(s < MLP_UP_STEPS)
    def _():
        hid = jnp.dot(h_ref[...], wu_ref[...], preferred_element_type=F32)
        per_chunk = MLP_KC // MLP_TF
        c0 = pl.multiple_of((s % per_chunk) * MLP_TF, MLP_TF)
        hid_ref[s // per_chunk, :, pl.ds(c0, MLP_TF)] = jnp.square(jnp.maximum(hid, 0.0)).astype(BF16)

    @pl.when(s >= MLP_UP_STEPS)
    def _():
        kc = (s - MLP_UP_STEPS) % MLP_K_STEPS
        part = jnp.dot(hid_ref[kc], wd_ref[...], preferred_element_type=F32)

        @pl.when(kc == 0)
        def _():
            o_ref[...] = x_ref[...] + part

        @pl.when(kc != 0)
        def _():
            o_ref[...] += part


def mlp_residual(h, x, w_up, w_down, *, tm):
    m, k = x.shape
    down = lambda s: jnp.maximum(s - MLP_UP_STEPS, 0)
    out_tile = pl.BlockSpec((tm, MLP_TN), lambda i, s: (i, down(s) // MLP_K_STEPS))
    return pl.pallas_call(
        _mlp_kernel,
        out_shape=jax.ShapeDtypeStruct((m, k), F32),
        grid=(m // tm, MLP_UP_STEPS + MLP_DOWN_STEPS),
        in_specs=[
            pl.BlockSpec((tm, k), lambda i, s: (i, 0)),
            out_tile,
            pl.BlockSpec((k, MLP_TF), lambda i, s: (0, jnp.minimum(s, MLP_UP_STEPS - 1))),
            pl.BlockSpec((MLP_KC, MLP_TN), lambda i, s: (down(s) % MLP_K_STEPS, down(s) // MLP_K_STEPS)),
        ],
        out_specs=out_tile,
        scratch_shapes=[pltpu.VMEM((MLP_K_STEPS, tm, MLP_KC), BF16)],
        compiler_params=_cparams(("parallel", "arbitrary")),
        name="mlp_residual",
    )(h, x, w_up, w_down)


def _rotate(x, cos, sin):
    half = x.shape[-1] // 2
    x1, x2 = x[:, :half], x[:, half:]
    return jnp.concatenate([x1 * cos - x2 * sin, x2 * cos + x1 * sin], axis=-1)


def _head_norm_gate(o, g, rnw):
    o = o * lax.rsqrt(jnp.mean(o * o, axis=-1, keepdims=True) + EPS)
    return (g * jax.nn.sigmoid(g)) * (o * rnw)


def _dot_t_lhs(a, b):
    return lax.dot_general(a, b, (((0,), (0,)), ((), ())), preferred_element_type=F32)


def _dot_t_rhs(a, b):
    return lax.dot_general(a, b, (((1,), (1,)), ((), ())), preferred_element_type=F32)


MIX_P_ROWS = 512


def _ret_prompt_compute(q_ref, k_ref, v_ref, g_ref, cos_ref, sin_ref, din_ref, dq_ref, dk_ref, dc_ref,
                        rnw_ref, o_ref, s_ref):
    for c in range(MIX_P_ROWS // RET_CHUNK):
        rows = slice(c * RET_CHUNK, (c + 1) * RET_CHUNK)
        cos, sin = cos_ref[rows, :], sin_ref[rows, :]
        for h in range(RET_HEADS):
            cols = slice(h * RET_HEAD_DIM, (h + 1) * RET_HEAD_DIM)
            dq = jnp.concatenate([dq_ref[h], dq_ref[h]], axis=-1)
            dk = jnp.concatenate([dk_ref[h], dk_ref[h]], axis=-1)
            q = _rotate(q_ref[rows, cols], cos, sin)
            k = _rotate(k_ref[rows, cols], cos, sin) * (RET_HEAD_DIM ** -0.5)
            vb = v_ref[rows, cols].astype(BF16)
            s = s_ref[h]
            scores = _dot_t_rhs(q.astype(BF16), k.astype(BF16)) * din_ref[h]
            o = jnp.dot(scores.astype(BF16), vb, preferred_element_type=F32)
            o = o + jnp.dot((q * dq).astype(BF16), s.astype(BF16), preferred_element_type=F32)
            s_ref[h] = s * dc_ref[h] + _dot_t_lhs((k * dk).astype(BF16), vb)
            o_ref[rows, cols] = _head_norm_gate(o, g_ref[rows, cols], rnw_ref[:, cols]).astype(o_ref.dtype)


def _ret_sample_head(h, dec_ref, z_ref, cos, sin, st_ref, rnw_ref, o_ref, st_out_ref):
    cols = slice(h * RET_HEAD_DIM, (h + 1) * RET_HEAD_DIM)
    part = lambda p: z_ref[:, p * D_RET + h * RET_HEAD_DIM:p * D_RET + (h + 1) * RET_HEAD_DIM]
    q = _rotate(part(0), cos, sin)
    k = _rotate(part(1), cos, sin) * (RET_HEAD_DIM ** -0.5)
    v = part(2)
    g = part(3)
    slab = lambda a, t: a[t * S_TILE_B:(t + 1) * S_TILE_B, :]

    intra = []
    for t in range(DEC_SEQ):
        acc = None
        for j in range(t + 1):
            w = jnp.sum(slab(q, t) * slab(k, j), axis=-1, keepdims=True) * dec_ref[h, t * DEC_SEQ + j]
            acc = w * slab(v, j) if acc is None else acc + w * slab(v, j)
        intra.append(acc)
    o = jnp.concatenate(intra, axis=0)

    dq_rows = jnp.concatenate(
        [jnp.full((S_TILE_B, 1), 1.0, F32) * dec_ref[h, 16 + t] for t in range(DEC_SEQ)], axis=0)
    dk_rows = jnp.concatenate(
        [jnp.full((S_TILE_B, 1), 1.0, F32) * dec_ref[h, 20 + t] for t in range(DEC_SEQ)], axis=0)
    qd = (q * dq_rows).astype(BF16)
    kd = k * dk_rows
    vb = v.astype(BF16)
    dchunk = dec_ref[h, 24]
    row_b = lax.broadcasted_iota(jnp.int32, (S_TILE_ROWS, 1), 0) % S_TILE_B
    for b in range(S_TILE_B):
        mine = row_b == b
        s = st_ref[b, h]
        o = o + jnp.where(mine, jnp.dot(qd, s.astype(BF16), preferred_element_type=F32), 0.0)
        st_out_ref[b, h] = s * dchunk + _dot_t_lhs(jnp.where(mine, kd, 0.0).astype(BF16), vb)
    o_ref[:, cols] = _head_norm_gate(o, g, rnw_ref[:, cols]).astype(o_ref.dtype)


def _ret_sample_kernel(dec_ref, z_ref, cos_ref, sin_ref, st_ref, rnw_ref, o_ref, st_out_ref):
    cos, sin = cos_ref[...], sin_ref[...]
    for h in range(RET_HEADS):
        _ret_sample_head(h, dec_ref, z_ref, cos, sin, st_ref, rnw_ref, o_ref, st_out_ref)


def retention_sample(dec, z, cos_s, sin_s, state_ret, rnw, mix, layer, state_prev):
    extra = [mix] + _carried(state_prev)
    state_spec = pl.BlockSpec((None, S_TILE_B, RET_HEADS, RET_HEAD_DIM, RET_HEAD_DIM),
                              lambda i: (layer, i, 0, 0, 0))
    return pl.pallas_call(
        _skip_refs(_ret_sample_kernel, 6, len(extra)),
        out_shape=(jax.ShapeDtypeStruct(mix.shape, mix.dtype),
                   jax.ShapeDtypeStruct(state_ret.shape, F32)),
        grid=(N_S_TILES,),
        in_specs=[
            pl.BlockSpec(memory_space=pltpu.SMEM),
            pl.BlockSpec((S_TILE_ROWS, D_IN), lambda i: (S_BLK0 + i, 0)),
            pl.BlockSpec((S_TILE_ROWS, RET_HEAD_DIM // 2), lambda i: (0, 0)),
            pl.BlockSpec((S_TILE_ROWS, RET_HEAD_DIM // 2), lambda i: (0, 0)),
            state_spec,
            pl.BlockSpec((None, 1, D_RET), lambda i: (layer, 0, 0)),
        ] + [_ANY] * len(extra),
        out_specs=(pl.BlockSpec((S_TILE_ROWS, D_RET), lambda i: (S_BLK0 + i, 0)), state_spec),
        input_output_aliases={6 + n: n for n in range(len(extra))},
        compiler_params=_cparams(("arbitrary",)),
        name="retention_sample",
    )(dec, z, cos_s, sin_s, state_ret, rnw, *extra)


POOL_HALO = 16


def _shift_rows(a, s):
    return pltpu.roll(a, s, axis=0)


def _pool_prompt_compute(u_ref, pw_ref, ps_ref, o_ref, halo_ref):
    pos = pl.program_id(1) * MIX_P_ROWS + lax.broadcasted_iota(jnp.int32, (MIX_P_ROWS, 1), 0)
    for g, w in enumerate(POOL_WINDOWS):
        cols = slice(g * POOL_GROUP_DIM, (g + 1) * POOL_GROUP_DIM)
        u = u_ref[:, cols]
        a = jnp.concatenate([halo_ref[:, cols], u], axis=0)
        s = 1
        while s < w:
            a = a + _shift_rows(a, s)
            s *= 2
        cnt = jnp.minimum(pos + 1, w).astype(F32)
        pooled = a[POOL_HALO:, :] / cnt - u
        pm = jnp.dot(pooled.astype(BF16), pw_ref[g].astype(BF16), preferred_element_type=F32)
        o_ref[:, cols] = (pm * ps_ref[:, cols]).astype(o_ref.dtype)
    halo_ref[...] = u_ref[MIX_P_ROWS - POOL_HALO:, :]


def _mix_prompt_kernel(q_ref, k_ref, v_ref, g_ref, u_ref, cos_ref, sin_ref, din_ref, dq_ref, dk_ref,
                       dc_ref, rnw_ref, pw_ref, ps_ref, ret_ref, s_out_ref, pool_ref, buf_ref,
                       s_ref, halo_ref):
    t = pl.program_id(1)

    @pl.when(t == 0)
    def _():
        s_ref[...] = jnp.zeros_like(s_ref)
        halo_ref[...] = jnp.zeros_like(halo_ref)

    _ret_prompt_compute(q_ref, k_ref, v_ref, g_ref, cos_ref, sin_ref, din_ref, dq_ref, dk_ref, dc_ref,
                        rnw_ref, ret_ref, s_ref)
    _pool_prompt_compute(u_ref, pw_ref, ps_ref, pool_ref, halo_ref)

    @pl.when(t == pl.num_programs(1) - 1)
    def _():
        s_out_ref[...] = s_ref[...]
        buf_ref[...] = u_ref[MIX_P_ROWS - POOL_BUF:, :]


def mix_prompt(z, cos, sin, din, dq, dk, dc, rnw, pool_w, pool_scale, layer, state_prev, buf_prev):
    nt = SEQ // MIX_P_ROWS
    extra = _carried(state_prev) + _carried(buf_prev)
    zspec = lambda part: pl.BlockSpec((MIX_P_ROWS, D_RET), lambda b, t: (b * nt + t, part))
    whole = lambda a: pl.BlockSpec(a.shape, lambda b, t: (0,) * a.ndim)
    rope = pl.BlockSpec((MIX_P_ROWS, RET_HEAD_DIM // 2), lambda b, t: (t, 0))
    state_block = (RET_HEADS, RET_HEAD_DIM, RET_HEAD_DIM)
    mix_spec = pl.BlockSpec((MIX_P_ROWS, D_RET), lambda b, t: (b * nt + t, 0))
    n_in = 14
    return pl.pallas_call(
        _skip_refs(_mix_prompt_kernel, n_in, len(extra)),
        out_shape=(jax.ShapeDtypeStruct((M_ROWS, D_RET), BF16),
                   jax.ShapeDtypeStruct((DEPTH, BATCH) + state_block, F32),
                   jax.ShapeDtypeStruct((M_ROWS, D_POOL), BF16),
                   jax.ShapeDtypeStruct((DEPTH, BATCH, POOL_BUF, D_POOL), F32)),
        grid=(BATCH, nt),
        in_specs=[
            zspec(0), zspec(1), zspec(2), zspec(3), zspec(4), rope, rope,
            whole(din), whole(dq), whole(dk), whole(dc),
            pl.BlockSpec((None, 1, D_RET), lambda b, t: (layer, 0, 0)),
            pl.BlockSpec((None, 4, POOL_GROUP_DIM, POOL_GROUP_DIM), lambda b, t: (layer, 0, 0, 0)),
            pl.BlockSpec((None, 1, D_POOL), lambda b, t: (layer, 0, 0)),
        ] + [_ANY] * len(extra),
        out_specs=(
            mix_spec,
            pl.BlockSpec((None, None) + state_block, lambda b, t: (layer, b, 0, 0, 0)),
            mix_spec,
            pl.BlockSpec((None, None, POOL_BUF, D_POOL), lambda b, t: (layer, b, 0, 0)),
        ),
        scratch_shapes=[pltpu.VMEM(state_block, F32), pltpu.VMEM((POOL_HALO, D_POOL), F32)],
        input_output_aliases={n_in: 1, n_in + 1: 3} if extra else {},
        compiler_params=_cparams(("parallel", "arbitrary")),
        name="mix_prompt",
    )(z, z, z, z, z, cos, sin, din, dq, dk, dc, rnw, pool_w, pool_scale, *extra)


def _pool_sample_kernel(u_ref, buf_ref, pw_ref, ps_ref, o_ref, nbuf_ref):

    def ext(r, cols):
        if r < POOL_BUF:
            return buf_ref[:, r, cols]
        return u_ref[(r - POOL_BUF) * S_TILE_B:(r - POOL_BUF + 1) * S_TILE_B, cols]

    for g, w in enumerate(POOL_WINDOWS):
        cols = slice(g * POOL_GROUP_DIM, (g + 1) * POOL_GROUP_DIM)
        pooled = []
        for t in range(DEC_SEQ):
            win = ext(POOL_BUF + t, cols)
            for r in range(POOL_BUF + t - w + 1, POOL_BUF + t):
                win = win + ext(r, cols)
            cnt = float(min(t + PAST_LEN + 1, w))
            pooled.append(win / cnt - ext(POOL_BUF + t, cols))
        pooled = jnp.concatenate(pooled, axis=0)
        pm = jnp.dot(pooled.astype(BF16), pw_ref[g].astype(BF16), preferred_element_type=F32)
        o_ref[:, cols] = (pm * ps_ref[:, cols]).astype(o_ref.dtype)

    full = slice(0, D_POOL)
    for r in range(POOL_BUF):
        nbuf_ref[:, r, :] = ext(r + DEC_SEQ, full)


def pool_sample(z, state_pool, pool_w, pool_scale, mix, layer, buf_prev):
    extra = [mix] + _carried(buf_prev)
    return pl.pallas_call(
        _skip_refs(_pool_sample_kernel, 4, len(extra)),
        out_shape=(jax.ShapeDtypeStruct(mix.shape, mix.dtype),
                   jax.ShapeDtypeStruct(state_pool.shape, F32)),
        grid=(N_S_TILES,),
        in_specs=[
            pl.BlockSpec((S_TILE_ROWS, D_POOL), lambda i: (S_BLK0 + i, 4)),
            pl.BlockSpec((None, S_TILE_B, POOL_BUF, D_POOL), lambda i: (layer, i, 0, 0)),
            pl.BlockSpec((None, 4, POOL_GROUP_DIM, POOL_GROUP_DIM), lambda i: (layer, 0, 0, 0)),
            pl.BlockSpec((None, 1, D_POOL), lambda i: (layer, 0, 0)),
        ] + [_ANY] * len(extra),
        out_specs=(
            pl.BlockSpec((S_TILE_ROWS, D_POOL), lambda i: (S_BLK0 + i, 0)),
            pl.BlockSpec((None, S_TILE_B, POOL_BUF, D_POOL), lambda i: (layer, i, 0, 0)),
        ),
        input_output_aliases={4 + n: n for n in range(len(extra))},
        compiler_params=_cparams(("arbitrary",)),
        name="pool_sample",
    )(z, state_pool, pool_w, pool_scale, *extra)


def _softmax_rows(s):
    m = jnp.max(s, axis=-1, keepdims=True)
    e = jnp.exp(s - m)
    return e / jnp.sum(e, axis=-1, keepdims=True)


XATTN_P_ROWS = 1024


def _xattn_prompt_kernel(q_ref, mk_ref, mv_ref, o_ref):
    for h in range(MEM_HEADS):
        cols = slice(h * MEM_HEAD_DIM, (h + 1) * MEM_HEAD_DIM)
        s = _dot_t_rhs(q_ref[:, cols], mk_ref[:, cols].astype(BF16)) * (MEM_HEAD_DIM ** -0.5)
        p = _softmax_rows(s)
        o_ref[:, cols] = jnp.dot(p.astype(BF16), mv_ref[:, cols].astype(BF16),
                                 preferred_element_type=F32).astype(o_ref.dtype)


def xattn_prompt(qx, mk, mv, layer):
    nt = SEQ // XATTN_P_ROWS
    return pl.pallas_call(
        _xattn_prompt_kernel,
        out_shape=jax.ShapeDtypeStruct((M_ROWS, D_MODEL), BF16),
        grid=(BATCH, nt),
        in_specs=[
            pl.BlockSpec((XATTN_P_ROWS, D_MODEL), lambda b, t: (b * nt + t, 0)),
            pl.BlockSpec((None, N_MEM, D_MODEL), lambda b, t: (layer, b, 0)),
            pl.BlockSpec((None, N_MEM, D_MODEL), lambda b, t: (layer, b, 0)),
        ],
        out_specs=pl.BlockSpec((XATTN_P_ROWS, D_MODEL), lambda b, t: (b * nt + t, 0)),
        compiler_params=_cparams(("parallel", "arbitrary")),
        name="xattn_prompt",
    )(qx, mk, mv)


XATTN_S_B = 4
XATTN_S_ROWS = MEM_HEADS * DEC_SEQ
XATTN_S_KEYS = N_MEM * MEM_HEADS


def _xattn_sample_kernel(q_ref, k_ref, v_ref, o_ref):
    row_h = lax.broadcasted_iota(jnp.int32, (XATTN_S_ROWS, XATTN_S_KEYS), 0) // DEC_SEQ
    col_h = lax.broadcasted_iota(jnp.int32, (XATTN_S_ROWS, XATTN_S_KEYS), 1) % MEM_HEADS
    same_head = row_h == col_h
    for bb in range(XATTN_S_B):
        k2 = k_ref[bb].reshape(XATTN_S_KEYS, MEM_HEAD_DIM).astype(BF16)
        v2 = v_ref[bb].reshape(XATTN_S_KEYS, MEM_HEAD_DIM).astype(BF16)
        s = _dot_t_rhs(q_ref[bb], k2) * (MEM_HEAD_DIM ** -0.5)
        s = jnp.where(same_head, s, -jnp.inf)
        p = _softmax_rows(s)
        o_ref[bb] = jnp.dot(p.astype(BF16), v2, preferred_element_type=F32).astype(o_ref.dtype)


def xattn_sample(qs, cache_k, cache_v, layer):
    kv_spec = pl.BlockSpec((None, XATTN_S_B, N_MEM, MEM_HEADS, MEM_HEAD_DIM),
                           lambda i: (layer, i, 0, 0, 0))
    qo_spec = pl.BlockSpec((XATTN_S_B, XATTN_S_ROWS, MEM_HEAD_DIM), lambda i: (i, 0, 0))
    return pl.pallas_call(
        _xattn_sample_kernel,
        out_shape=jax.ShapeDtypeStruct(qs.shape, BF16),
        grid=(DEC_BATCH // XATTN_S_B,),
        in_specs=[qo_spec, kv_spec, kv_spec],
        out_specs=qo_spec,
        compiler_params=_cparams(("parallel",)),
        name="xattn_sample",
    )(qs, cache_k, cache_v)


def _sample_rows_to_heads(a):
    a = a.reshape(N_S_TILES, DEC_SEQ, S_TILE_B, MEM_HEADS, MEM_HEAD_DIM)
    return a.transpose(0, 2, 3, 1, 4).reshape(DEC_BATCH, XATTN_S_ROWS, MEM_HEAD_DIM)


def _heads_to_sample_rows(a):
    a = a.reshape(N_S_TILES, S_TILE_B, MEM_HEADS, DEC_SEQ, MEM_HEAD_DIM)
    return a.transpose(0, 3, 1, 2, 4).reshape(S_ROWS, D_MODEL)


FINAL_ROWS = 512


def _final_norm_kernel(x_ref, nw_ref, yp_ref, ys_ref):
    def run(dst_ref):
        def body(rows):
            dst_ref[rows, :] = _rms_rows(x_ref[rows, :], nw_ref[...])
        _for_row_chunks(FINAL_ROWS, _norm_chunk(FINAL_ROWS), body)

    is_prompt = pl.program_id(0) < P_ROWS // FINAL_ROWS
    pl.when(is_prompt)(lambda: run(yp_ref))
    pl.when(jnp.logical_not(is_prompt))(lambda: run(ys_ref))


def final_norm(x, nw):
    n_p = P_ROWS // FINAL_ROWS
    return pl.pallas_call(
        _final_norm_kernel,
        out_shape=(jax.ShapeDtypeStruct((P_ROWS, D_MODEL), F32),
                   jax.ShapeDtypeStruct((S_ROWS, D_MODEL), F32)),
        grid=(M_ROWS // FINAL_ROWS,),
        in_specs=[
            pl.BlockSpec((FINAL_ROWS, D_MODEL), lambda i: (i, 0)),
            pl.BlockSpec((1, D_MODEL), lambda i: (0, 0)),
        ],
        out_specs=(
            pl.BlockSpec((FINAL_ROWS, D_MODEL), lambda i: (jnp.minimum(i, n_p - 1), 0)),
            pl.BlockSpec((S_ROWS, D_MODEL), lambda i: (0, 0)),
        ),
        compiler_params=_cparams(("arbitrary",)),
        name="final_norm",
    )(x, nw)


def _rope_tables(pos):
    half = RET_HEAD_DIM // 2
    inv = ROPE_BASE ** (-jnp.arange(half, dtype=F32) / half)
    ang = pos.astype(F32)[:, None] * inv[None, :]
    return jnp.cos(ang), jnp.sin(ang)


def _log_gamma():
    return jnp.log1p(-jnp.exp2(-5.0 - jnp.arange(RET_HEADS, dtype=F32)))


def _decay_tables(chunk):
    lg = _log_gamma()
    idx = jnp.arange(chunk, dtype=F32)
    diff = idx[:, None] - idx[None, :]
    decay_in = jnp.where(diff[None] >= 0.0,
                         jnp.exp(lg[:, None, None] * jnp.maximum(diff, 0.0)[None]), 0.0)
    decay_q = jnp.exp(lg[:, None] * (idx[None, :] + 1.0))
    decay_k = jnp.exp(lg[:, None] * (chunk - 1.0 - idx[None, :]))
    decay_chunk = jnp.exp(lg * chunk)
    return decay_in, decay_q, decay_k, decay_chunk


def _to_sample_rows(a):
    d = a.shape[-1]
    return a.reshape(N_S_TILES, S_TILE_B, DEC_SEQ, d).transpose(0, 2, 1, 3).reshape(S_ROWS, d)


def _from_sample_rows(a):
    d = a.shape[-1]
    return a.reshape(N_S_TILES, DEC_SEQ, S_TILE_B, d).transpose(0, 2, 1, 3).reshape(DEC_BATCH, DEC_SEQ, d)


def kernel(x_prompt, x_sample, mem_prompt, state_ret, state_pool, cache_mem_k, cache_mem_v,
           attn_norm_w, w_in, ret_norm_w, pool_w, pool_scale, w_out, xattn_norm_w, mem_norm_w,
           w_xq, w_mk, w_mv, w_xo, mlp_norm_w, w_up, w_down, final_norm_w):
    cos_p, sin_p = _rope_tables(jnp.arange(SEQ))
    cos_s, sin_s = _rope_tables(jnp.arange(DEC_SEQ) + PAST_LEN)
    cos_s = jnp.repeat(cos_s, S_TILE_B, axis=0)
    sin_s = jnp.repeat(sin_s, S_TILE_B, axis=0)
    din_p, dq_p, dk_p, dc_p = _decay_tables(RET_CHUNK)
    half = RET_HEAD_DIM // 2
    dq_p = jnp.broadcast_to(dq_p[:, :, None], (RET_HEADS, RET_CHUNK, half))
    dk_p = jnp.broadcast_to(dk_p[:, :, None], (RET_HEADS, RET_CHUNK, half))
    dc_p = jnp.broadcast_to(dc_p[:, None, None], (RET_HEADS, 1, RET_HEAD_DIM))
    din_s, dq_s, dk_s, dc_s = _decay_tables(DEC_SEQ)
    dec_s = jnp.concatenate([din_s.reshape(RET_HEADS, DEC_SEQ * DEC_SEQ), dq_s, dk_s,
                             dc_s[:, None], jnp.zeros((RET_HEADS, 7), F32)], axis=1)

    row3 = lambda a: a.reshape(DEPTH, 1, a.shape[-1])
    attn_nw, xattn_nw, mem_nw, mlp_nw = map(row3, (attn_norm_w, xattn_norm_w, mem_norm_w, mlp_norm_w))
    ret_nw, pool_sc = row3(ret_norm_w), row3(pool_scale)
    w_in_b = w_in.astype(BF16)

    x_sources = [x_prompt.reshape(P_ROWS, D_MODEL), _to_sample_rows(x_sample)]
    mem = mem_prompt.reshape(BATCH * N_MEM, D_MODEL)

    ret_p = buf_p = mk_p = mv_p = ret_s = buf_s = None
    for l in range(DEPTH):
        z, w_up_b, w_down_b = in_proj_and_mlp_weight_cast(x_sources, attn_nw, w_in_b, w_up, w_down, l,
                                                          tm=TM_IN_PROJ)
        mix_r, ret_p, mix_p, buf_p = mix_prompt(z, cos_p, sin_p, din_p, dq_p, dk_p, dc_p, ret_nw,
                                                pool_w, pool_sc, l, ret_p, buf_p)
        mix_r, ret_s = retention_sample(dec_s, z, cos_s, sin_s, state_ret, ret_nw, mix_r, l, ret_s)
        mix_p, buf_s = pool_sample(z, state_pool, pool_w, pool_sc, mix_p, l, buf_s)
        x, h = matmul_residual_resident([mix_r, mix_p], w_out, x_sources, xattn_nw, l, tm=TM_RESIDENT)

        mk_p, mv_p = mem_kv_stacked(mem, mem_nw, w_mk, w_mv, l, mk_p, mv_p, tm=1024, tn=512)
        qx = matmul_resident(h, w_xq, l, tm=TM)
        att = xattn_prompt(qx, mk_p, mv_p, l)
        att_s = xattn_sample(_sample_rows_to_heads(qx[P_ROWS:]), cache_mem_k, cache_mem_v, l)
        att = lax.dynamic_update_slice(att, _heads_to_sample_rows(att_s), (P_ROWS, 0))
        x, h = matmul_residual_resident([att], w_xo, [x], mlp_nw, l, tm=TM_RESIDENT)
        x = mlp_residual(h, x, w_up_b, w_down_b, tm=TM)
        x_sources = [x]

    y_p, y_s = final_norm(x, final_norm_w.reshape(1, D_MODEL))
    mem_shape = (DEPTH, BATCH, N_MEM, MEM_HEADS, MEM_HEAD_DIM)
    return (y_p.reshape(BATCH, SEQ, D_MODEL), _from_sample_rows(y_s),
            ret_p, buf_p, mk_p.reshape(mem_shape), mv_p.reshape(mem_shape), ret_s, buf_s)
```

```python
import functools

import jax
import jax.numpy as jnp
from jax import lax
from jax.experimental import pallas as pl
from jax.experimental.pallas import tpu as pltpu

D_MODEL = 2048
BATCH = 4
SEQ = 2048
DEPTH = 2
DEC_BATCH = 128
DEC_SEQ = 4
PAST_LEN = 16384
D_RET = 1024
RET_HEADS = 4
RET_HEAD_DIM = 256
D_POOL = 1024
POOL_WINDOWS = (2, 4, 8, 16)
POOL_GROUP_DIM = 256
POOL_BUF = 15
D_IN = 5120
N_MEM = 256
MEM_HEADS = 4
MEM_HEAD_DIM = 512
D_FF = 8192
RET_CHUNK = 128
ROPE_BASE = 10000.0
EPS = 1e-6

F32 = jnp.float32
BF16 = jnp.bfloat16

P_ROWS = BATCH * SEQ
S_ROWS = DEC_BATCH * DEC_SEQ
M_ROWS = P_ROWS + S_ROWS
S_TILE_B = 8
S_TILE_ROWS = S_TILE_B * DEC_SEQ
N_S_TILES = DEC_BATCH // S_TILE_B
S_BLK0 = P_ROWS // S_TILE_ROWS

TM = 1088
TM_RESIDENT = 512
TM_IN_PROJ = 256
VMEM_LIMIT = 58 * 1024 * 1024


def _cparams(sem):
    return pltpu.CompilerParams(dimension_semantics=sem, vmem_limit_bytes=VMEM_LIMIT)


_ANY = pl.BlockSpec(memory_space=pl.ANY)


def _skip_refs(kernel_fn, start, count):
    def wrapped(*refs):
        return kernel_fn(*refs[:start], *refs[start + count:])
    return wrapped


def _carried(prev):
    return [] if prev is None else [prev]


def _stacked_row_specs(sources, tm):
    specs, ends, start = [], [], 0
    for a in sources:
        n = a.shape[0] // tm
        specs.append(pl.BlockSpec((tm, a.shape[1]),
                                  lambda i, start=start, n=n: (jnp.clip(i - start, 0, n - 1), 0)))
        start += n
        ends.append(start)
    return specs, tuple(ends)


def _read_stacked_rows(refs, ends, rows=slice(None)):
    i = pl.program_id(0)
    x = refs[-1][rows, :]
    for ref, end in zip(reversed(refs[:-1]), reversed(ends[:-1])):
        x = jnp.where(i < end, ref[rows, :], x)
    return x


def _rms_rows(x, nw):
    ms = jnp.mean(x * x, axis=-1, keepdims=True)
    return x * lax.rsqrt(ms + EPS) * nw


def _for_row_chunks(n_rows, chunk, body):
    def step(c, carry):
        body(pl.ds(pl.multiple_of(c * chunk, chunk), chunk))
        return carry
    lax.fori_loop(0, n_rows // chunk, step, 0)


def _norm_chunk(n_rows):
    return 272 if n_rows % 272 == 0 else 128


def _cast_weight_once(w_ref, wb_ref):
    @pl.when(pl.program_id(0) == 0)
    def _():
        def body(rows):
            wb_ref[rows, :] = w_ref[rows, :].astype(BF16)
        _for_row_chunks(w_ref.shape[0], 256, body)


def _mem_kv_kernel(x_ref, nw_ref, wk_ref, wv_ref, k_ref, v_ref, h_ref, *, tm):
    @pl.when(pl.program_id(1) == 0)
    def _():
        def body(rows):
            h_ref[rows, :] = _rms_rows(x_ref[rows, :], nw_ref[...]).astype(BF16)
        _for_row_chunks(tm, _norm_chunk(tm), body)

    k_ref[...] = jnp.dot(h_ref[...], wk_ref[...].astype(BF16), preferred_element_type=F32)
    v_ref[...] = jnp.dot(h_ref[...], wv_ref[...].astype(BF16), preferred_element_type=F32)


def mem_kv_stacked(x, nw, w_k, w_v, layer, prev_k, prev_v, *, tm, tn):
    m, k = x.shape
    n = w_k.shape[-1]
    extra = _carried(prev_k) + _carried(prev_v)
    w_spec = pl.BlockSpec((None, k, tn), lambda i, j: (layer, 0, j))
    o_spec = pl.BlockSpec((None, tm, tn), lambda i, j: (layer, i, j))
    stacked = jax.ShapeDtypeStruct((DEPTH, m, n), F32)
    return pl.pallas_call(
        _skip_refs(functools.partial(_mem_kv_kernel, tm=tm), 4, len(extra)),
        out_shape=(stacked, stacked),
        grid=(m // tm, n // tn),
        in_specs=[
            pl.BlockSpec((tm, k), lambda i, j: (i, 0)),
            pl.BlockSpec((None, 1, k), lambda i, j: (layer, 0, 0)),
            w_spec, w_spec,
        ] + [_ANY] * len(extra),
        out_specs=(o_spec, o_spec),
        scratch_shapes=[pltpu.VMEM((tm, k), BF16)],
        input_output_aliases={4: 0, 5: 1} if extra else {},
        compiler_params=_cparams(("parallel", "arbitrary")),
        name="mem_kv",
    )(x, nw, w_k, w_v, *extra)


def _in_proj_kernel(*refs, src_ends):
    x_refs = refs[:len(src_ends)]
    nw_ref, w_ref, wu_ref, wd_ref, o_ref, wub_ref, wdb_ref = refs[len(src_ends):]
    h = _rms_rows(_read_stacked_rows(x_refs, src_ends), nw_ref[...]).astype(BF16)
    o_ref[...] = jnp.dot(h, w_ref[...], preferred_element_type=F32)
    wub_ref[...] = wu_ref[...].astype(BF16)
    wdb_ref[...] = wd_ref[...].astype(BF16)


def in_proj_and_mlp_weight_cast(x_sources, nw, w, w_up, w_down, layer, *, tm):
    x_specs, src_ends = _stacked_row_specs(x_sources, tm)
    m = src_ends[-1] * tm
    k, n = w.shape[1:]
    ff = w_up.shape[-1]
    slab = 256
    last = ff // slab - 1
    assert src_ends[-1] > last
    return pl.pallas_call(
        functools.partial(_in_proj_kernel, src_ends=src_ends),
        out_shape=(jax.ShapeDtypeStruct((m, n), F32),
                   jax.ShapeDtypeStruct((k, ff), BF16),
                   jax.ShapeDtypeStruct((ff, k), BF16)),
        grid=(src_ends[-1],),
        in_specs=x_specs + [
            pl.BlockSpec((None, 1, k), lambda i: (layer, 0, 0)),
            pl.BlockSpec((None, k, n), lambda i: (layer, 0, 0), pipeline_mode=pl.Buffered(1)),
            pl.BlockSpec((None, k, slab), lambda i: (layer, 0, jnp.minimum(i, last))),
            pl.BlockSpec((None, slab, k), lambda i: (layer, jnp.minimum(i, last), 0)),
        ],
        out_specs=(
            pl.BlockSpec((tm, n), lambda i: (i, 0)),
            pl.BlockSpec((k, slab), lambda i: (0, jnp.minimum(i, last))),
            pl.BlockSpec((slab, k), lambda i: (jnp.minimum(i, last), 0)),
        ),
        compiler_params=_cparams(("arbitrary",)),
        name="in_proj",
    )(*x_sources, nw, w, w_up, w_down)


def _resident_weight_spec(w, layer):
    return pl.BlockSpec((None,) + w.shape[1:], lambda i: (layer, 0, 0), pipeline_mode=pl.Buffered(1))


def _matmul_res_resident_kernel(*refs, n_parts, res_ends):
    a_refs = refs[:n_parts]
    w_ref = refs[n_parts]
    r_refs = refs[n_parts + 1:n_parts + 1 + len(res_ends)]
    nw_ref, o_ref, h_ref, wb_ref = refs[n_parts + 1 + len(res_ends):]
    _cast_weight_once(w_ref, wb_ref)
    half = o_ref.shape[0] // 2
    for rows in (slice(0, half), slice(half, 2 * half)):
        acc = _read_stacked_rows(r_refs, res_ends, rows)
        k0 = 0
        for a_ref in a_refs:
            k1 = k0 + a_ref.shape[1]
            acc = acc + jnp.dot(a_ref[rows, :], wb_ref[k0:k1, :], preferred_element_type=F32)
            k0 = k1
        o_ref[rows, :] = acc
        h_ref[rows, :] = _rms_rows(acc, nw_ref[...]).astype(BF16)


def matmul_residual_resident(a_parts, w, res_sources, nw, layer, *, tm):
    res_specs, res_ends = _stacked_row_specs(res_sources, tm)
    m, n = res_ends[-1] * tm, w.shape[-1]
    k = sum(a.shape[1] for a in a_parts)
    row_tile = lambda width: pl.BlockSpec((tm, width), lambda i: (i, 0))
    return pl.pallas_call(
        functools.partial(_matmul_res_resident_kernel, n_parts=len(a_parts), res_ends=res_ends),
        out_shape=(jax.ShapeDtypeStruct((m, n), F32), jax.ShapeDtypeStruct((m, n), BF16)),
        grid=(m // tm,),
        in_specs=[row_tile(a.shape[1]) for a in a_parts] + [_resident_weight_spec(w, layer)]
        + res_specs + [pl.BlockSpec((None, 1, n), lambda i: (layer, 0, 0))],
        out_specs=(row_tile(n), row_tile(n)),
        scratch_shapes=[pltpu.VMEM((k, n), BF16)],
        compiler_params=_cparams(("arbitrary",)),
        name="matmul_residual_resident",
    )(*a_parts, w, *res_sources, nw)


MLP_TF = 1024
MLP_TN = 512
MLP_KC = 4096
MLP_UP_STEPS = D_FF // MLP_TF
MLP_K_STEPS = D_FF // MLP_KC
MLP_DOWN_STEPS = (D_MODEL // MLP_TN) * MLP_K_STEPS


def _mlp_kernel(h_ref, x_ref, wu_ref, wd_ref, o_ref, hid_ref):
    s = pl.program_id(1)

    @pl.when(s < MLP_UP_STEPS)
    def _():
        hid = jnp.dot(h_ref[...], wu_ref[...], preferred_element_type=F32)
        per_chunk = MLP_KC // MLP_TF
        c0 = pl.multiple_of((s % per_chunk) * MLP_TF, MLP_TF)
        hid_ref[s // per_chunk, :, pl.ds(c0, MLP_TF)] = jnp.square(jnp.maximum(hid, 0.0)).astype(BF16)

    @pl.when(s >= MLP_UP_STEPS)
    def _():
        kc = (s - MLP_UP_STEPS) % MLP_K_STEPS
        part = jnp.dot(hid_ref[kc], wd_ref[...], preferred_element_type=F32)

        @pl.when(kc == 0)
        def _():
            o_ref[...] = x_ref[...] + part

        @pl.when(kc != 0)
        def _():
            o_ref[...] += part


def mlp_residual(h, x, w_up, w_down, *, tm):
    m, k = x.shape
    down = lambda s: jnp.maximum(s - MLP_UP_STEPS, 0)
    out_tile = pl.BlockSpec((tm, MLP_TN), lambda i, s: (i, down(s) // MLP_K_STEPS))
    return pl.pallas_call(
        _mlp_kernel,
        out_shape=jax.ShapeDtypeStruct((m, k), F32),
        grid=(m // tm, MLP_UP_STEPS + MLP_DOWN_STEPS),
        in_specs=[
            pl.BlockSpec((tm, k), lambda i, s: (i, 0)),
            out_tile,
            pl.BlockSpec((k, MLP_TF), lambda i, s: (0, jnp.minimum(s, MLP_UP_STEPS - 1))),
            pl.BlockSpec((MLP_KC, MLP_TN), lambda i, s: (down(s) % MLP_K_STEPS, down(s) // MLP_K_STEPS)),
        ],
        out_specs=out_tile,
        scratch_shapes=[pltpu.VMEM((MLP_K_STEPS, tm, MLP_KC), BF16)],
        compiler_params=_cparams(("parallel", "arbitrary")),
        name="mlp_residual",
    )(h, x, w_up, w_down)


def _rotate(x, cos, sin):
    half = x.shape[-1] // 2
    x1, x2 = x[:, :half], x[:, half:]
    return jnp.concatenate([x1 * cos - x2 * sin, x2 * cos + x1 * sin], axis=-1)


def _head_norm_gate(o, g, rnw):
    o = o * lax.rsqrt(jnp.mean(o * o, axis=-1, keepdims=True) + EPS)
    return (g * jax.nn.sigmoid(g)) * (o * rnw)


def _dot_t_lhs(a, b):
    return lax.dot_general(a, b, (((0,), (0,)), ((), ())), preferred_element_type=F32)


def _dot_t_rhs(a, b):
    return lax.dot_general(a, b, (((1,), (1,)), ((), ())), preferred_element_type=F32)


MIX_P_ROWS = 512


def _ret_prompt_compute(q_ref, k_ref, v_ref, g_ref, cos_ref, sin_ref, din_ref, dq_ref, dk_ref, dc_ref,
                        rnw_ref, o_ref, s_ref):
    for c in range(MIX_P_ROWS // RET_CHUNK):
        rows = slice(c * RET_CHUNK, (c + 1) * RET_CHUNK)
        cos, sin = cos_ref[rows, :], sin_ref[rows, :]
        for h in range(RET_HEADS):
            cols = slice(h * RET_HEAD_DIM, (h + 1) * RET_HEAD_DIM)
            dq = jnp.concatenate([dq_ref[h], dq_ref[h]], axis=-1)
            dk = jnp.concatenate([dk_ref[h], dk_ref[h]], axis=-1)
            q = _rotate(q_ref[rows, cols], cos, sin)
            k = _rotate(k_ref[rows, cols], cos, sin) * (RET_HEAD_DIM ** -0.5)
            vb = v_ref[rows, cols].astype(BF16)
            s = s_ref[h]
            scores = _dot_t_rhs(q.astype(BF16), k.astype(BF16)) * din_ref[h]
            o = jnp.dot(scores.astype(BF16), vb, preferred_element_type=F32)
            o = o + jnp.dot((q * dq).astype(BF16), s.astype(BF16), preferred_element_type=F32)
            s_ref[h] = s * dc_ref[h] + _dot_t_lhs((k * dk).astype(BF16), vb)
            o_ref[rows, cols] = _head_norm_gate(o, g_ref[rows, cols], rnw_ref[:, cols]).astype(o_ref.dtype)


def _ret_sample_head(h, dec_ref, z_ref, cos, sin, st_ref, rnw_ref, o_ref, st_out_ref):
    cols = slice(h * RET_HEAD_DIM, (h + 1) * RET_HEAD_DIM)
    part = lambda p: z_ref[:, p * D_RET + h * RET_HEAD_DIM:p * D_RET + (h + 1) * RET_HEAD_DIM]
    q = _rotate(part(0), cos, sin)
    k = _rotate(part(1), cos, sin) * (RET_HEAD_DIM ** -0.5)
    v = part(2)
    g = part(3)
    slab = lambda a, t: a[t * S_TILE_B:(t + 1) * S_TILE_B, :]

    intra = []
    for t in range(DEC_SEQ):
        acc = None
        for j in range(t + 1):
            w = jnp.sum(slab(q, t) * slab(k, j), axis=-1, keepdims=True) * dec_ref[h, t * DEC_SEQ + j]
            acc = w * slab(v, j) if acc is None else acc + w * slab(v, j)
        intra.append(acc)
    o = jnp.concatenate(intra, axis=0)

    dq_rows = jnp.concatenate(
        [jnp.full((S_TILE_B, 1), 1.0, F32) * dec_ref[h, 16 + t] for t in range(DEC_SEQ)], axis=0)
    dk_rows = jnp.concatenate(
        [jnp.full((S_TILE_B, 1), 1.0, F32) * dec_ref[h, 20 + t] for t in range(DEC_SEQ)], axis=0)
    qd = (q * dq_rows).astype(BF16)
    kd = k * dk_rows
    vb = v.astype(BF16)
    dchunk = dec_ref[h, 24]
    row_b = lax.broadcasted_iota(jnp.int32, (S_TILE_ROWS, 1), 0) % S_TILE_B
    for b in range(S_TILE_B):
        mine = row_b == b
        s = st_ref[b, h]
        o = o + jnp.where(mine, jnp.dot(qd, s.astype(BF16), preferred_element_type=F32), 0.0)
        st_out_ref[b, h] = s * dchunk + _dot_t_lhs(jnp.where(mine, kd, 0.0).astype(BF16), vb)
    o_ref[:, cols] = _head_norm_gate(o, g, rnw_ref[:, cols]).astype(o_ref.dtype)


def _ret_sample_kernel(dec_ref, z_ref, cos_ref, sin_ref, st_ref, rnw_ref, o_ref, st_out_ref):
    cos, sin = cos_ref[...], sin_ref[...]
    for h in range(RET_HEADS):
        _ret_sample_head(h, dec_ref, z_ref, cos, sin, st_ref, rnw_ref, o_ref, st_out_ref)


def retention_sample(dec, z, cos_s, sin_s, state_ret, rnw, mix, layer, state_prev):
    extra = [mix] + _carried(state_prev)
    state_spec = pl.BlockSpec((None, S_TILE_B, RET_HEADS, RET_HEAD_DIM, RET_HEAD_DIM),
                              lambda i: (layer, i, 0, 0, 0))
    return pl.pallas_call(
        _skip_refs(_ret_sample_kernel, 6, len(extra)),
        out_shape=(jax.ShapeDtypeStruct(mix.shape, mix.dtype),
                   jax.ShapeDtypeStruct(state_ret.shape, F32)),
        grid=(N_S_TILES,),
        in_specs=[
            pl.BlockSpec(memory_space=pltpu.SMEM),
            pl.BlockSpec((S_TILE_ROWS, D_IN), lambda i: (S_BLK0 + i, 0)),
            pl.BlockSpec((S_TILE_ROWS, RET_HEAD_DIM // 2), lambda i: (0, 0)),
            pl.BlockSpec((S_TILE_ROWS, RET_HEAD_DIM // 2), lambda i: (0, 0)),
            state_spec,
            pl.BlockSpec((None, 1, D_RET), lambda i: (layer, 0, 0)),
        ] + [_ANY] * len(extra),
        out_specs=(pl.BlockSpec((S_TILE_ROWS, D_RET), lambda i: (S_BLK0 + i, 0)), state_spec),
        input_output_aliases={6 + n: n for n in range(len(extra))},
        compiler_params=_cparams(("arbitrary",)),
        name="retention_sample",
    )(dec, z, cos_s, sin_s, state_ret, rnw, *extra)


POOL_HALO = 16


def _shift_rows(a, s):
    return pltpu.roll(a, s, axis=0)


def _pool_prompt_compute(u_ref, pw_ref, ps_ref, o_ref, halo_ref):
    pos = pl.program_id(1) * MIX_P_ROWS + lax.broadcasted_iota(jnp.int32, (MIX_P_ROWS, 1), 0)
    for g, w in enumerate(POOL_WINDOWS):
        cols = slice(g * POOL_GROUP_DIM, (g + 1) * POOL_GROUP_DIM)
        u = u_ref[:, cols]
        a = jnp.concatenate([halo_ref[:, cols], u], axis=0)
        s = 1
        while s < w:
            a = a + _shift_rows(a, s)
            s *= 2
        cnt = jnp.minimum(pos + 1, w).astype(F32)
        pooled = a[POOL_HALO:, :] / cnt - u
        pm = jnp.dot(pooled.astype(BF16), pw_ref[g].astype(BF16), preferred_element_type=F32)
        o_ref[:, cols] = (pm * ps_ref[:, cols]).astype(o_ref.dtype)
    halo_ref[...] = u_ref[MIX_P_ROWS - POOL_HALO:, :]


def _mix_prompt_kernel(q_ref, k_ref, v_ref, g_ref, u_ref, cos_ref, sin_ref, din_ref, dq_ref, dk_ref,
                       dc_ref, rnw_ref, pw_ref, ps_ref, ret_ref, s_out_ref, pool_ref, buf_ref,
                       s_ref, halo_ref):
    t = pl.program_id(1)

    @pl.when(t == 0)
    def _():
        s_ref[...] = jnp.zeros_like(s_ref)
        halo_ref[...] = jnp.zeros_like(halo_ref)

    _ret_prompt_compute(q_ref, k_ref, v_ref, g_ref, cos_ref, sin_ref, din_ref, dq_ref, dk_ref, dc_ref,
                        rnw_ref, ret_ref, s_ref)
    _pool_prompt_compute(u_ref, pw_ref, ps_ref, pool_ref, halo_ref)

    @pl.when(t == pl.num_programs(1) - 1)
    def _():
        s_out_ref[...] = s_ref[...]
        buf_ref[...] = u_ref[MIX_P_ROWS - POOL_BUF:, :]


def mix_prompt(z, cos, sin, din, dq, dk, dc, rnw, pool_w, pool_scale, layer, state_prev, buf_prev):
    nt = SEQ // MIX_P_ROWS
    extra = _carried(state_prev) + _carried(buf_prev)
    zspec = lambda part: pl.BlockSpec((MIX_P_ROWS, D_RET), lambda b, t: (b * nt + t, part))
    whole = lambda a: pl.BlockSpec(a.shape, lambda b, t: (0,) * a.ndim)
    rope = pl.BlockSpec((MIX_P_ROWS, RET_HEAD_DIM // 2), lambda b, t: (t, 0))
    state_block = (RET_HEADS, RET_HEAD_DIM, RET_HEAD_DIM)
    mix_spec = pl.BlockSpec((MIX_P_ROWS, D_RET), lambda b, t: (b * nt + t, 0))
    n_in = 14
    return pl.pallas_call(
        _skip_refs(_mix_prompt_kernel, n_in, len(extra)),
        out_shape=(jax.ShapeDtypeStruct((M_ROWS, D_RET), BF16),
                   jax.ShapeDtypeStruct((DEPTH, BATCH) + state_block, F32),
                   jax.ShapeDtypeStruct((M_ROWS, D_POOL), BF16),
                   jax.ShapeDtypeStruct((DEPTH, BATCH, POOL_BUF, D_POOL), F32)),
        grid=(BATCH, nt),
        in_specs=[
            zspec(0), zspec(1), zspec(2), zspec(3), zspec(4), rope, rope,
            whole(din), whole(dq), whole(dk), whole(dc),
            pl.BlockSpec((None, 1, D_RET), lambda b, t: (layer, 0, 0)),
            pl.BlockSpec((None, 4, POOL_GROUP_DIM, POOL_GROUP_DIM), lambda b, t: (layer, 0, 0, 0)),
            pl.BlockSpec((None, 1, D_POOL), lambda b, t: (layer, 0, 0)),
        ] + [_ANY] * len(extra),
        out_specs=(
            mix_spec,
            pl.BlockSpec((None, None) + state_block, lambda b, t: (layer, b, 0, 0, 0)),
            mix_spec,
            pl.BlockSpec((None, None, POOL_BUF, D_POOL), lambda b, t: (layer, b, 0, 0)),
        ),
        scratch_shapes=[pltpu.VMEM(state_block, F32), pltpu.VMEM((POOL_HALO, D_POOL), F32)],
        input_output_aliases={n_in: 1, n_in + 1: 3} if extra else {},
        compiler_params=_cparams(("parallel", "arbitrary")),
        name="mix_prompt",
    )(z, z, z, z, z, cos, sin, din, dq, dk, dc, rnw, pool_w, pool_scale, *extra)


def _pool_sample_kernel(u_ref, buf_ref, pw_ref, ps_ref, o_ref, nbuf_ref):

    def ext(r, cols):
        if r < POOL_BUF:
            return buf_ref[:, r, cols]
        return u_ref[(r - POOL_BUF) * S_TILE_B:(r - POOL_BUF + 1) * S_TILE_B, cols]

    for g, w in enumerate(POOL_WINDOWS):
        cols = slice(g * POOL_GROUP_DIM, (g + 1) * POOL_GROUP_DIM)
        pooled = []
        for t in range(DEC_SEQ):
            win = ext(POOL_BUF + t, cols)
            for r in range(POOL_BUF + t - w + 1, POOL_BUF + t):
                win = win + ext(r, cols)
            cnt = float(min(t + PAST_LEN + 1, w))
            pooled.append(win / cnt - ext(POOL_BUF + t, cols))
        pooled = jnp.concatenate(pooled, axis=0)
        pm = jnp.dot(pooled.astype(BF16), pw_ref[g].astype(BF16), preferred_element_type=F32)
        o_ref[:, cols] = (pm * ps_ref[:, cols]).astype(o_ref.dtype)

    full = slice(0, D_POOL)
    for r in range(POOL_BUF):
        nbuf_ref[:, r, :] = ext(r + DEC_SEQ, full)


def pool_sample(z, state_pool, pool_w, pool_scale, mix, layer, buf_prev):
    extra = [mix] + _carried(buf_prev)
    return pl.pallas_call(
        _skip_refs(_pool_sample_kernel, 4, len(extra)),
        out_shape=(jax.ShapeDtypeStruct(mix.shape, mix.dtype),
                   jax.ShapeDtypeStruct(state_pool.shape, F32)),
        grid=(N_S_TILES,),
        in_specs=[
            pl.BlockSpec((S_TILE_ROWS, D_POOL), lambda i: (S_BLK0 + i, 4)),
            pl.BlockSpec((None, S_TILE_B, POOL_BUF, D_POOL), lambda i: (layer, i, 0, 0)),
            pl.BlockSpec((None, 4, POOL_GROUP_DIM, POOL_GROUP_DIM), lambda i: (layer, 0, 0, 0)),
            pl.BlockSpec((None, 1, D_POOL), lambda i: (layer, 0, 0)),
        ] + [_ANY] * len(extra),
        out_specs=(
            pl.BlockSpec((S_TILE_ROWS, D_POOL), lambda i: (S_BLK0 + i, 0)),
            pl.BlockSpec((None, S_TILE_B, POOL_BUF, D_POOL), lambda i: (layer, i, 0, 0)),
        ),
        input_output_aliases={4 + n: n for n in range(len(extra))},
        compiler_params=_cparams(("arbitrary",)),
        name="pool_sample",
    )(z, state_pool, pool_w, pool_scale, *extra)


def _softmax_rows(s):
    m = jnp.max(s, axis=-1, keepdims=True)
    e = jnp.exp(s - m)
    return e / jnp.sum(e, axis=-1, keepdims=True)


def _xq_prompt_attention_kernel(h_ref, w_ref, mk_ref, mv_ref, o_ref, wb_ref, *, prompt_tiles):
    _cast_weight_once(w_ref, wb_ref)

    @pl.when(pl.program_id(0) < prompt_tiles)
    def _():
        half = o_ref.shape[0] // 2
        for rows in (slice(0, half), slice(half, 2 * half)):
            q = jnp.dot(h_ref[rows, :], wb_ref[...], preferred_element_type=F32).astype(BF16)
            for hd in range(MEM_HEADS):
                cols = slice(hd * MEM_HEAD_DIM, (hd + 1) * MEM_HEAD_DIM)
                s = _dot_t_rhs(q[:, cols], mk_ref[:, cols].astype(BF16)) * (MEM_HEAD_DIM ** -0.5)
                p = _softmax_rows(s)
                o_ref[rows, cols] = jnp.dot(p.astype(BF16), mv_ref[:, cols].astype(BF16),
                                            preferred_element_type=F32).astype(o_ref.dtype)

    @pl.when(pl.program_id(0) >= prompt_tiles)
    def _():
        o_ref[...] = jnp.dot(h_ref[...], wb_ref[...], preferred_element_type=F32).astype(o_ref.dtype)


def xq_and_prompt_attention(h, w, mk, mv, layer, *, tm):
    m, k = h.shape
    tiles_per_batch = SEQ // tm
    prompt_tiles = P_ROWS // tm
    kv_spec = pl.BlockSpec((None, N_MEM, k),
                           lambda i: (layer, jnp.minimum(i // tiles_per_batch, BATCH - 1), 0))
    return pl.pallas_call(
        functools.partial(_xq_prompt_attention_kernel, prompt_tiles=prompt_tiles),
        out_shape=jax.ShapeDtypeStruct((m, k), BF16),
        grid=(m // tm,),
        in_specs=[pl.BlockSpec((tm, k), lambda i: (i, 0)), _resident_weight_spec(w, layer),
                  kv_spec, kv_spec],
        out_specs=pl.BlockSpec((tm, k), lambda i: (i, 0)),
        scratch_shapes=[pltpu.VMEM((k, k), BF16)],
        compiler_params=_cparams(("arbitrary",)),
        name="xq_prompt_attention",
    )(h, w, mk, mv)


XATTN_S_B = 4
XATTN_S_ROWS = MEM_HEADS * DEC_SEQ
XATTN_S_KEYS = N_MEM * MEM_HEADS


def _xattn_sample_kernel(q_ref, k_ref, v_ref, o_ref):
    row_h = lax.broadcasted_iota(jnp.int32, (XATTN_S_ROWS, XATTN_S_KEYS), 0) // DEC_SEQ
    col_h = lax.broadcasted_iota(jnp.int32, (XATTN_S_ROWS, XATTN_S_KEYS), 1) % MEM_HEADS
    same_head = row_h == col_h
    for bb in range(XATTN_S_B):
        k2 = k_ref[bb].reshape(XATTN_S_KEYS, MEM_HEAD_DIM).astype(BF16)
        v2 = v_ref[bb].reshape(XATTN_S_KEYS, MEM_HEAD_DIM).astype(BF16)
        s = _dot_t_rhs(q_ref[bb], k2) * (MEM_HEAD_DIM ** -0.5)
        s = jnp.where(same_head, s, -jnp.inf)
        p = _softmax_rows(s)
        o_ref[bb] = jnp.dot(p.astype(BF16), v2, preferred_element_type=F32).astype(o_ref.dtype)


def xattn_sample(qs, cache_k, cache_v, layer):
    kv_spec = pl.BlockSpec((None, XATTN_S_B, N_MEM, MEM_HEADS, MEM_HEAD_DIM),
                           lambda i: (layer, i, 0, 0, 0))
    qo_spec = pl.BlockSpec((XATTN_S_B, XATTN_S_ROWS, MEM_HEAD_DIM), lambda i: (i, 0, 0))
    return pl.pallas_call(
        _xattn_sample_kernel,
        out_shape=jax.ShapeDtypeStruct(qs.shape, BF16),
        grid=(DEC_BATCH // XATTN_S_B,),
        in_specs=[qo_spec, kv_spec, kv_spec],
        out_specs=qo_spec,
        compiler_params=_cparams(("parallel",)),
        name="xattn_sample",
    )(qs, cache_k, cache_v)


def _sample_rows_to_heads(a):
    a = a.reshape(N_S_TILES, DEC_SEQ, S_TILE_B, MEM_HEADS, MEM_HEAD_DIM)
    return a.transpose(0, 2, 3, 1, 4).reshape(DEC_BATCH, XATTN_S_ROWS, MEM_HEAD_DIM)


def _heads_to_sample_rows(a):
    a = a.reshape(N_S_TILES, S_TILE_B, MEM_HEADS, DEC_SEQ, MEM_HEAD_DIM)
    return a.transpose(0, 3, 1, 2, 4).reshape(S_ROWS, D_MODEL)


FINAL_ROWS = 512


def _final_norm_kernel(x_ref, nw_ref, yp_ref, ys_ref):
    def run(dst_ref):
        def body(rows):
            dst_ref[rows, :] = _rms_rows(x_ref[rows, :], nw_ref[...])
        _for_row_chunks(FINAL_ROWS, _norm_chunk(FINAL_ROWS), body)

    is_prompt = pl.program_id(0) < P_ROWS // FINAL_ROWS
    pl.when(is_prompt)(lambda: run(yp_ref))
    pl.when(jnp.logical_not(is_prompt))(lambda: run(ys_ref))


def final_norm(x, nw):
    n_p = P_ROWS // FINAL_ROWS
    return pl.pallas_call(
        _final_norm_kernel,
        out_shape=(jax.ShapeDtypeStruct((P_ROWS, D_MODEL), F32),
                   jax.ShapeDtypeStruct((S_ROWS, D_MODEL), F32)),
        grid=(M_ROWS // FINAL_ROWS,),
        in_specs=[
            pl.BlockSpec((FINAL_ROWS, D_MODEL), lambda i: (i, 0)),
            pl.BlockSpec((1, D_MODEL), lambda i: (0, 0)),
        ],
        out_specs=(
            pl.BlockSpec((FINAL_ROWS, D_MODEL), lambda i: (jnp.minimum(i, n_p - 1), 0)),
            pl.BlockSpec((S_ROWS, D_MODEL), lambda i: (0, 0)),
        ),
        compiler_params=_cparams(("arbitrary",)),
        name="final_norm",
    )(x, nw)


def _rope_tables(pos):
    half = RET_HEAD_DIM // 2
    inv = ROPE_BASE ** (-jnp.arange(half, dtype=F32) / half)
    ang = pos.astype(F32)[:, None] * inv[None, :]
    return jnp.cos(ang), jnp.sin(ang)


def _log_gamma():
    return jnp.log1p(-jnp.exp2(-5.0 - jnp.arange(RET_HEADS, dtype=F32)))


def _decay_tables(chunk):
    lg = _log_gamma()
    idx = jnp.arange(chunk, dtype=F32)
    diff = idx[:, None] - idx[None, :]
    decay_in = jnp.where(diff[None] >= 0.0,
                         jnp.exp(lg[:, None, None] * jnp.maximum(diff, 0.0)[None]), 0.0)
    decay_q = jnp.exp(lg[:, None] * (idx[None, :] + 1.0))
    decay_k = jnp.exp(lg[:, None] * (chunk - 1.0 - idx[None, :]))
    decay_chunk = jnp.exp(lg * chunk)
    return decay_in, decay_q, decay_k, decay_chunk


def _to_sample_rows(a):
    d = a.shape[-1]
    return a.reshape(N_S_TILES, S_TILE_B, DEC_SEQ, d).transpose(0, 2, 1, 3).reshape(S_ROWS, d)


def _from_sample_rows(a):
    d = a.shape[-1]
    return a.reshape(N_S_TILES, DEC_SEQ, S_TILE_B, d).transpose(0, 2, 1, 3).reshape(DEC_BATCH, DEC_SEQ, d)


def kernel(x_prompt, x_sample, mem_prompt, state_ret, state_pool, cache_mem_k, cache_mem_v,
           attn_norm_w, w_in, ret_norm_w, pool_w, pool_scale, w_out, xattn_norm_w, mem_norm_w,
           w_xq, w_mk, w_mv, w_xo, mlp_norm_w, w_up, w_down, final_norm_w):
    cos_p, sin_p = _rope_tables(jnp.arange(SEQ))
    cos_s, sin_s = _rope_tables(jnp.arange(DEC_SEQ) + PAST_LEN)
    cos_s = jnp.repeat(cos_s, S_TILE_B, axis=0)
    sin_s = jnp.repeat(sin_s, S_TILE_B, axis=0)
    din_p, dq_p, dk_p, dc_p = _decay_tables(RET_CHUNK)
    half = RET_HEAD_DIM // 2
    dq_p = jnp.broadcast_to(dq_p[:, :, None], (RET_HEADS, RET_CHUNK, half))
    dk_p = jnp.broadcast_to(dk_p[:, :, None], (RET_HEADS, RET_CHUNK, half))
    dc_p = jnp.broadcast_to(dc_p[:, None, None], (RET_HEADS, 1, RET_HEAD_DIM))
    din_s, dq_s, dk_s, dc_s = _decay_tables(DEC_SEQ)
    dec_s = jnp.concatenate([din_s.reshape(RET_HEADS, DEC_SEQ * DEC_SEQ), dq_s, dk_s,
                             dc_s[:, None], jnp.zeros((RET_HEADS, 7), F32)], axis=1)

    row3 = lambda a: a.reshape(DEPTH, 1, a.shape[-1])
    attn_nw, xattn_nw, mem_nw, mlp_nw = map(row3, (attn_norm_w, xattn_norm_w, mem_norm_w, mlp_norm_w))
    ret_nw, pool_sc = row3(ret_norm_w), row3(pool_scale)
    w_in_b = w_in.astype(BF16)

    x_sources = [x_prompt.reshape(P_ROWS, D_MODEL), _to_sample_rows(x_sample)]
    mem = mem_prompt.reshape(BATCH * N_MEM, D_MODEL)

    ret_p = buf_p = mk_p = mv_p = ret_s = buf_s = None
    for l in range(DEPTH):
        z, w_up_b, w_down_b = in_proj_and_mlp_weight_cast(x_sources, attn_nw, w_in_b, w_up, w_down, l,
                                                          tm=TM_IN_PROJ)
        mix_r, ret_p, mix_p, buf_p = mix_prompt(z, cos_p, sin_p, din_p, dq_p, dk_p, dc_p, ret_nw,
                                                pool_w, pool_sc, l, ret_p, buf_p)
        mix_r, ret_s = retention_sample(dec_s, z, cos_s, sin_s, state_ret, ret_nw, mix_r, l, ret_s)
        mix_p, buf_s = pool_sample(z, state_pool, pool_w, pool_sc, mix_p, l, buf_s)
        x, h = matmul_residual_resident([mix_r, mix_p], w_out, x_sources, xattn_nw, l, tm=TM_RESIDENT)

        mk_p, mv_p = mem_kv_stacked(mem, mem_nw, w_mk, w_mv, l, mk_p, mv_p, tm=1024, tn=512)
        att = xq_and_prompt_attention(h, w_xq, mk_p, mv_p, l, tm=TM_RESIDENT)
        att_s = xattn_sample(_sample_rows_to_heads(att[P_ROWS:]), cache_mem_k, cache_mem_v, l)
        att = lax.dynamic_update_slice(att, _heads_to_sample_rows(att_s), (P_ROWS, 0))
        x, h = matmul_residual_resident([att], w_xo, [x], mlp_nw, l, tm=TM_RESIDENT)
        x = mlp_residual(h, x, w_up_b, w_down_b, tm=TM)
        x_sources = [x]

    y_p, y_s = final_norm(x, final_norm_w.reshape(1, D_MODEL))
    mem_shape = (DEPTH, BATCH, N_MEM, MEM_HEADS, MEM_HEAD_DIM)
    return (y_p.reshape(BATCH, SEQ, D_MODEL), _from_sample_rows(y_s),
            ret_p, buf_p, mk_p.reshape(mem_shape), mv_p.reshape(mem_shape), ret_s, buf_s)
```

```python
import functools

import jax
import jax.numpy as jnp
from jax import lax
from jax.experimental import pallas as pl
from jax.experimental.pallas import tpu as pltpu

D_MODEL = 2048
BATCH = 4
SEQ = 2048
DEPTH = 2
DEC_BATCH = 128
DEC_SEQ = 4
PAST_LEN = 16384
D_RET = 1024
RET_HEADS = 4
RET_HEAD_DIM = 256
D_POOL = 1024
POOL_WINDOWS = (2, 4, 8, 16)
POOL_GROUP_DIM = 256
POOL_BUF = 15
D_IN = 5120
N_MEM = 256
MEM_HEADS = 4
MEM_HEAD_DIM = 512
D_FF = 8192
RET_CHUNK = 128
ROPE_BASE = 10000.0
EPS = 1e-6

F32 = jnp.float32
BF16 = jnp.bfloat16

P_ROWS = BATCH * SEQ
S_ROWS = DEC_BATCH * DEC_SEQ
S_TILE_B = 8
S_TILE_ROWS = S_TILE_B * DEC_SEQ
N_S_TILES = DEC_BATCH // S_TILE_B

TM_MLP = 1024
TM_RESIDENT = 512
TM_IN_PROJ = 256
VMEM_LIMIT = 58 * 1024 * 1024


def _cparams(sem):
    return pltpu.CompilerParams(dimension_semantics=sem, vmem_limit_bytes=VMEM_LIMIT)


_ANY = pl.BlockSpec(memory_space=pl.ANY)


def _skip_refs(kernel_fn, start, count):
    def wrapped(*refs):
        return kernel_fn(*refs[:start], *refs[start + count:])
    return wrapped


def _carried(prev):
    return [] if prev is None else [prev]


def _rms_rows(x, nw):
    ms = jnp.mean(x * x, axis=-1, keepdims=True)
    return x * lax.rsqrt(ms + EPS) * nw


def _for_row_chunks(n_rows, chunk, body):
    def step(c, carry):
        body(pl.ds(pl.multiple_of(c * chunk, chunk), chunk))
        return carry
    lax.fori_loop(0, n_rows // chunk, step, 0)


def _cast_weight_once(w_ref, wb_ref):
    @pl.when(pl.program_id(0) == 0)
    def _():
        def body(rows):
            wb_ref[rows, :] = w_ref[rows, :].astype(BF16)
        _for_row_chunks(w_ref.shape[0], 256, body)


def _dot_t_lhs(a, b):
    return lax.dot_general(a, b, (((0,), (0,)), ((), ())), preferred_element_type=F32)


def _dot_t_rhs(a, b):
    return lax.dot_general(a, b, (((1,), (1,)), ((), ())), preferred_element_type=F32)


def _softmax_rows(s):
    m = jnp.max(s, axis=-1, keepdims=True)
    e = jnp.exp(s - m)
    return e / jnp.sum(e, axis=-1, keepdims=True)


def _mem_kv_kernel(x_ref, nw_ref, wk_ref, wv_ref, k_ref, v_ref, h_ref, *, tm):
    @pl.when(pl.program_id(1) == 0)
    def _():
        def body(rows):
            h_ref[rows, :] = _rms_rows(x_ref[rows, :], nw_ref[...]).astype(BF16)
        _for_row_chunks(tm, 128, body)

    k_ref[...] = jnp.dot(h_ref[...], wk_ref[...].astype(BF16), preferred_element_type=F32)
    v_ref[...] = jnp.dot(h_ref[...], wv_ref[...].astype(BF16), preferred_element_type=F32)


def mem_kv_stacked(x, nw, w_k, w_v, layer, prev_k, prev_v, *, tm, tn):
    m, k = x.shape
    n = w_k.shape[-1]
    extra = _carried(prev_k) + _carried(prev_v)
    w_spec = pl.BlockSpec((None, k, tn), lambda i, j: (layer, 0, j))
    o_spec = pl.BlockSpec((None, tm, tn), lambda i, j: (layer, i, j))
    stacked = jax.ShapeDtypeStruct((DEPTH, m, n), F32)
    return pl.pallas_call(
        _skip_refs(functools.partial(_mem_kv_kernel, tm=tm), 4, len(extra)),
        out_shape=(stacked, stacked),
        grid=(m // tm, n // tn),
        in_specs=[
            pl.BlockSpec((tm, k), lambda i, j: (i, 0)),
            pl.BlockSpec((None, 1, k), lambda i, j: (layer, 0, 0)),
            w_spec, w_spec,
        ] + [_ANY] * len(extra),
        out_specs=(o_spec, o_spec),
        scratch_shapes=[pltpu.VMEM((tm, k), BF16)],
        input_output_aliases={4: 0, 5: 1} if extra else {},
        compiler_params=_cparams(("parallel", "arbitrary")),
        name="mem_kv",
    )(x, nw, w_k, w_v, *extra)


def _in_proj_kernel(x_ref, nw_ref, w_ref, *refs):
    h = _rms_rows(x_ref[...], nw_ref[...]).astype(BF16)
    if len(refs) == 1:
        (o_ref,) = refs
    else:
        wu_ref, wd_ref, o_ref, wub_ref, wdb_ref = refs
        wub_ref[...] = wu_ref[...].astype(BF16)
        wdb_ref[...] = wd_ref[...].astype(BF16)
    o_ref[...] = jnp.dot(h, w_ref[...], preferred_element_type=F32)


def in_proj(x, nw, w, layer, *, tm, mlp_weights=None):
    m, k = x.shape
    n = w.shape[-1]
    steps = m // tm
    in_specs = [
        pl.BlockSpec((tm, k), lambda i: (i, 0)),
        pl.BlockSpec((None, 1, k), lambda i: (layer, 0, 0)),
        pl.BlockSpec((None, k, n), lambda i: (layer, 0, 0), pipeline_mode=pl.Buffered(1)),
    ]
    out_shape = [jax.ShapeDtypeStruct((m, n), F32)]
    out_specs = [pl.BlockSpec((tm, n), lambda i: (i, 0))]
    operands = [x, nw, w]
    if mlp_weights is not None:
        w_up, w_down = mlp_weights
        ff = w_up.shape[-1]
        slab = ff // steps
        assert slab * steps == ff and slab % 128 == 0
        in_specs += [pl.BlockSpec((None, k, slab), lambda i: (layer, 0, i)),
                     pl.BlockSpec((None, slab, k), lambda i: (layer, i, 0))]
        out_shape += [jax.ShapeDtypeStruct((k, ff), BF16), jax.ShapeDtypeStruct((ff, k), BF16)]
        out_specs += [pl.BlockSpec((k, slab), lambda i: (0, i)), pl.BlockSpec((slab, k), lambda i: (i, 0))]
        operands += [w_up, w_down]
    out = pl.pallas_call(
        _in_proj_kernel,
        out_shape=tuple(out_shape),
        grid=(steps,),
        in_specs=in_specs,
        out_specs=tuple(out_specs),
        compiler_params=_cparams(("arbitrary",)),
        name="in_proj",
    )(*operands)
    return out[0] if mlp_weights is None else out


def _resident_weight_spec(w, layer):
    return pl.BlockSpec((None,) + w.shape[1:], lambda i: (layer, 0, 0), pipeline_mode=pl.Buffered(1))


def _matmul_res_resident_kernel(*refs, n_parts):
    a_refs = refs[:n_parts]
    w_ref, r_ref, nw_ref, o_ref, h_ref, wb_ref = refs[n_parts:]
    _cast_weight_once(w_ref, wb_ref)
    half = o_ref.shape[0] // 2
    for rows in (slice(0, half), slice(half, 2 * half)):
        acc = r_ref[rows, :]
        k0 = 0
        for a_ref in a_refs:
            k1 = k0 + a_ref.shape[1]
            acc = acc + jnp.dot(a_ref[rows, :], wb_ref[k0:k1, :], preferred_element_type=F32)
            k0 = k1
        o_ref[rows, :] = acc
        h_ref[rows, :] = _rms_rows(acc, nw_ref[...]).astype(BF16)


def matmul_residual_resident(a_parts, w, res, nw, layer, *, tm):
    m, n = res.shape
    k = sum(a.shape[1] for a in a_parts)
    row_tile = lambda width: pl.BlockSpec((tm, width), lambda i: (i, 0))
    return pl.pallas_call(
        functools.partial(_matmul_res_resident_kernel, n_parts=len(a_parts)),
        out_shape=(jax.ShapeDtypeStruct((m, n), F32), jax.ShapeDtypeStruct((m, n), BF16)),
        grid=(m // tm,),
        in_specs=[row_tile(a.shape[1]) for a in a_parts] + [
            _resident_weight_spec(w, layer),
            row_tile(n),
            pl.BlockSpec((None, 1, n), lambda i: (layer, 0, 0)),
        ],
        out_specs=(row_tile(n), row_tile(n)),
        scratch_shapes=[pltpu.VMEM((k, n), BF16)],
        compiler_params=_cparams(("arbitrary",)),
        name="matmul_residual_resident",
    )(*a_parts, w, res, nw)


def _xq_kernel(h_ref, w_ref, *refs):
    if len(refs) == 2:
        o_ref, wb_ref = refs
        _cast_weight_once(w_ref, wb_ref)
        o_ref[...] = jnp.dot(h_ref[...], wb_ref[...], preferred_element_type=F32).astype(o_ref.dtype)
        return
    mk_ref, mv_ref, o_ref, wb_ref = refs
    _cast_weight_once(w_ref, wb_ref)
    half = o_ref.shape[0] // 2
    for rows in (slice(0, half), slice(half, 2 * half)):
        q = jnp.dot(h_ref[rows, :], wb_ref[...], preferred_element_type=F32).astype(BF16)
        for hd in range(MEM_HEADS):
            cols = slice(hd * MEM_HEAD_DIM, (hd + 1) * MEM_HEAD_DIM)
            s = _dot_t_rhs(q[:, cols], mk_ref[:, cols].astype(BF16)) * (MEM_HEAD_DIM ** -0.5)
            p = _softmax_rows(s)
            o_ref[rows, cols] = jnp.dot(p.astype(BF16), mv_ref[:, cols].astype(BF16),
                                        preferred_element_type=F32).astype(o_ref.dtype)


def xq_projection(h, w, layer, *, tm, memory_kv=None):
    m, k = h.shape
    in_specs = [pl.BlockSpec((tm, k), lambda i: (i, 0)), _resident_weight_spec(w, layer)]
    operands = [h, w]
    if memory_kv is not None:
        tiles_per_batch = SEQ // tm
        kv_spec = pl.BlockSpec((None, N_MEM, k), lambda i: (layer, i // tiles_per_batch, 0))
        in_specs += [kv_spec, kv_spec]
        operands += list(memory_kv)
    return pl.pallas_call(
        _xq_kernel,
        out_shape=jax.ShapeDtypeStruct((m, k), BF16),
        grid=(m // tm,),
        in_specs=in_specs,
        out_specs=pl.BlockSpec((tm, k), lambda i: (i, 0)),
        scratch_shapes=[pltpu.VMEM((k, k), BF16)],
        compiler_params=_cparams(("arbitrary",)),
        name="xq_projection",
    )(*operands)


XATTN_S_ROWS = MEM_HEADS * DEC_SEQ
XATTN_S_KEYS = N_MEM * MEM_HEADS


def _sample_attention(q_ref, k_ref, v_ref, o_ref):
    row_h = lax.broadcasted_iota(jnp.int32, (XATTN_S_ROWS, XATTN_S_KEYS), 0) // DEC_SEQ
    col_h = lax.broadcasted_iota(jnp.int32, (XATTN_S_ROWS, XATTN_S_KEYS), 1) % MEM_HEADS
    k2 = k_ref[...].reshape(XATTN_S_KEYS, MEM_HEAD_DIM).astype(BF16)
    v2 = v_ref[...].reshape(XATTN_S_KEYS, MEM_HEAD_DIM).astype(BF16)
    s = _dot_t_rhs(q_ref[...], k2) * (MEM_HEAD_DIM ** -0.5)
    s = jnp.where(row_h == col_h, s, -jnp.inf)
    p = _softmax_rows(s)
    o_ref[...] = jnp.dot(p.astype(BF16), v2, preferred_element_type=F32).astype(o_ref.dtype)


def _sample_rows_to_heads(a):
    a = a.reshape(N_S_TILES, DEC_SEQ, S_TILE_B, MEM_HEADS, MEM_HEAD_DIM)
    return a.transpose(0, 2, 3, 1, 4).reshape(DEC_BATCH, XATTN_S_ROWS, MEM_HEAD_DIM)


def _heads_to_sample_rows(a):
    a = a.reshape(N_S_TILES, S_TILE_B, MEM_HEADS, DEC_SEQ, MEM_HEAD_DIM)
    return a.transpose(0, 3, 1, 2, 4).reshape(S_ROWS, D_MODEL)


MLP_TN = 512
MLP_KC = 4096
MLP_K_STEPS = D_FF // MLP_KC
MLP_DOWN_STEPS = (D_MODEL // MLP_TN) * MLP_K_STEPS


def _mlp_kernel(h_ref, x_ref, wu_ref, wd_ref, *refs, tf):
    if len(refs) == 2:
        o_ref, hid_ref = refs
        attention_job = None
    else:
        q_ref, k_ref, v_ref, o_ref, att_ref, hid_ref = refs
        attention_job = (q_ref, k_ref, v_ref, att_ref)
    up_steps = D_FF // tf
    s = pl.program_id(1)

    @pl.when(s < up_steps)
    def _():
        hid = jnp.dot(h_ref[...], wu_ref[...], preferred_element_type=F32)
        per_chunk = MLP_KC // tf
        c0 = pl.multiple_of((s % per_chunk) * tf, tf)
        hid_ref[s // per_chunk, :, pl.ds(c0, tf)] = jnp.square(jnp.maximum(hid, 0.0)).astype(BF16)
        if attention_job is not None:
            _sample_attention(*attention_job)

    @pl.when(s >= up_steps)
    def _():
        kc = (s - up_steps) % MLP_K_STEPS
        part = jnp.dot(hid_ref[kc], wd_ref[...], preferred_element_type=F32)

        @pl.when(kc == 0)
        def _():
            o_ref[...] = x_ref[...] + part

        @pl.when(kc != 0)
        def _():
            o_ref[...] += part


def mlp_residual(h, x, w_up, w_down, *, tm, tf, sample_attention=None):
    m, k = x.shape
    up_steps = D_FF // tf
    up = lambda s: jnp.minimum(s, up_steps - 1)
    down = lambda s: jnp.maximum(s - up_steps, 0)
    out_tile = pl.BlockSpec((tm, MLP_TN), lambda i, s: (i, down(s) // MLP_K_STEPS))
    in_specs = [
        pl.BlockSpec((tm, k), lambda i, s: (i, 0)),
        out_tile,
        pl.BlockSpec((k, tf), lambda i, s: (0, up(s))),
        pl.BlockSpec((MLP_KC, MLP_TN), lambda i, s: (down(s) % MLP_K_STEPS, down(s) // MLP_K_STEPS)),
    ]
    out_shape = [jax.ShapeDtypeStruct((m, k), F32)]
    out_specs = [out_tile]
    operands = [h, x, w_up, w_down]
    if sample_attention is not None:
        qs, cache_k, cache_v, layer = sample_attention
        assert (m // tm) * up_steps == qs.shape[0]
        batch = lambda i, s: i * up_steps + up(s)
        q_spec = pl.BlockSpec((None,) + qs.shape[1:], lambda i, s: (batch(i, s), 0, 0))
        kv_spec = pl.BlockSpec((None, None) + cache_k.shape[2:], lambda i, s: (layer, batch(i, s), 0, 0, 0))
        in_specs += [q_spec, kv_spec, kv_spec]
        out_shape += [jax.ShapeDtypeStruct(qs.shape, BF16)]
        out_specs += [q_spec]
        operands += [qs, cache_k, cache_v]
    out = pl.pallas_call(
        functools.partial(_mlp_kernel, tf=tf),
        out_shape=tuple(out_shape),
        grid=(m // tm, up_steps + MLP_DOWN_STEPS),
        in_specs=in_specs,
        out_specs=tuple(out_specs),
        scratch_shapes=[pltpu.VMEM((MLP_K_STEPS, tm, MLP_KC), BF16)],
        compiler_params=_cparams(("arbitrary", "arbitrary")),
        name="mlp_residual",
    )(*operands)
    return out[0] if sample_attention is None else out


def _rotate(x, cos, sin):
    half = x.shape[-1] // 2
    x1, x2 = x[:, :half], x[:, half:]
    return jnp.concatenate([x1 * cos - x2 * sin, x2 * cos + x1 * sin], axis=-1)


def _head_norm_gate(o, g, rnw):
    o = o * lax.rsqrt(jnp.mean(o * o, axis=-1, keepdims=True) + EPS)
    return (g * jax.nn.sigmoid(g)) * (o * rnw)


MIX_P_ROWS = 512


def _ret_prompt_compute(q_ref, k_ref, v_ref, g_ref, cos_ref, sin_ref, din_ref, dq_ref, dk_ref, dc_ref,
                        rnw_ref, o_ref, s_ref):
    for c in range(MIX_P_ROWS // RET_CHUNK):
        rows = slice(c * RET_CHUNK, (c + 1) * RET_CHUNK)
        cos, sin = cos_ref[rows, :], sin_ref[rows, :]
        for h in range(RET_HEADS):
            cols = slice(h * RET_HEAD_DIM, (h + 1) * RET_HEAD_DIM)
            dq = jnp.concatenate([dq_ref[h], dq_ref[h]], axis=-1)
            dk = jnp.concatenate([dk_ref[h], dk_ref[h]], axis=-1)
            q = _rotate(q_ref[rows, cols], cos, sin)
            k = _rotate(k_ref[rows, cols], cos, sin) * (RET_HEAD_DIM ** -0.5)
            vb = v_ref[rows, cols].astype(BF16)
            s = s_ref[h]
            scores = _dot_t_rhs(q.astype(BF16), k.astype(BF16)) * din_ref[h]
            o = jnp.dot(scores.astype(BF16), vb, preferred_element_type=F32)
            o = o + jnp.dot((q * dq).astype(BF16), s.astype(BF16), preferred_element_type=F32)
            s_ref[h] = s * dc_ref[h] + _dot_t_lhs((k * dk).astype(BF16), vb)
            o_ref[rows, cols] = _head_norm_gate(o, g_ref[rows, cols], rnw_ref[:, cols]).astype(o_ref.dtype)


def _ret_sample_head(h, dec_ref, z_ref, cos, sin, st_ref, rnw_ref, o_ref, st_out_ref):
    cols = slice(h * RET_HEAD_DIM, (h + 1) * RET_HEAD_DIM)
    part = lambda p: z_ref[:, p * D_RET + h * RET_HEAD_DIM:p * D_RET + (h + 1) * RET_HEAD_DIM]
    q = _rotate(part(0), cos, sin)
    k = _rotate(part(1), cos, sin) * (RET_HEAD_DIM ** -0.5)
    v = part(2)
    g = part(3)
    slab = lambda a, t: a[t * S_TILE_B:(t + 1) * S_TILE_B, :]

    intra = []
    for t in range(DEC_SEQ):
        acc = None
        for j in range(t + 1):
            w = jnp.sum(slab(q, t) * slab(k, j), axis=-1, keepdims=True) * dec_ref[h, t * DEC_SEQ + j]
            acc = w * slab(v, j) if acc is None else acc + w * slab(v, j)
        intra.append(acc)
    o = jnp.concatenate(intra, axis=0)

    dq_rows = jnp.concatenate(
        [jnp.full((S_TILE_B, 1), 1.0, F32) * dec_ref[h, 16 + t] for t in range(DEC_SEQ)], axis=0)
    dk_rows = jnp.concatenate(
        [jnp.full((S_TILE_B, 1), 1.0, F32) * dec_ref[h, 20 + t] for t in range(DEC_SEQ)], axis=0)
    qd = (q * dq_rows).astype(BF16)
    kd = k * dk_rows
    vb = v.astype(BF16)
    dchunk = dec_ref[h, 24]
    row_b = lax.broadcasted_iota(jnp.int32, (S_TILE_ROWS, 1), 0) % S_TILE_B
    for b in range(S_TILE_B):
        mine = row_b == b
        s = st_ref[b, h]
        o = o + jnp.where(mine, jnp.dot(qd, s.astype(BF16), preferred_element_type=F32), 0.0)
        st_out_ref[b, h] = s * dchunk + _dot_t_lhs(jnp.where(mine, kd, 0.0).astype(BF16), vb)
    o_ref[:, cols] = _head_norm_gate(o, g, rnw_ref[:, cols]).astype(o_ref.dtype)


def _ret_sample_kernel(dec_ref, z_ref, cos_ref, sin_ref, st_ref, rnw_ref, o_ref, st_out_ref):
    cos, sin = cos_ref[...], sin_ref[...]
    for h in range(RET_HEADS):
        _ret_sample_head(h, dec_ref, z_ref, cos, sin, st_ref, rnw_ref, o_ref, st_out_ref)


def retention_sample(dec, z, cos_s, sin_s, state_ret, rnw, layer, state_prev):
    extra = _carried(state_prev)
    state_spec = pl.BlockSpec((None, S_TILE_B, RET_HEADS, RET_HEAD_DIM, RET_HEAD_DIM),
                              lambda i: (layer, i, 0, 0, 0))
    return pl.pallas_call(
        _skip_refs(_ret_sample_kernel, 6, len(extra)),
        out_shape=(jax.ShapeDtypeStruct((S_ROWS, D_RET), BF16),
                   jax.ShapeDtypeStruct(state_ret.shape, F32)),
        grid=(N_S_TILES,),
        in_specs=[
            pl.BlockSpec(memory_space=pltpu.SMEM),
            pl.BlockSpec((S_TILE_ROWS, D_IN), lambda i: (i, 0)),
            pl.BlockSpec((S_TILE_ROWS, RET_HEAD_DIM // 2), lambda i: (0, 0)),
            pl.BlockSpec((S_TILE_ROWS, RET_HEAD_DIM // 2), lambda i: (0, 0)),
            state_spec,
            pl.BlockSpec((None, 1, D_RET), lambda i: (layer, 0, 0)),
        ] + [_ANY] * len(extra),
        out_specs=(pl.BlockSpec((S_TILE_ROWS, D_RET), lambda i: (i, 0)), state_spec),
        input_output_aliases={6: 1} if extra else {},
        compiler_params=_cparams(("arbitrary",)),
        name="retention_sample",
    )(dec, z, cos_s, sin_s, state_ret, rnw, *extra)


POOL_HALO = 16


def _shift_rows(a, s):
    return pltpu.roll(a, s, axis=0)


def _pool_prompt_compute(u_ref, pw_ref, ps_ref, o_ref, halo_ref):
    pos = pl.program_id(1) * MIX_P_ROWS + lax.broadcasted_iota(jnp.int32, (MIX_P_ROWS, 1), 0)
    for g, w in enumerate(POOL_WINDOWS):
        cols = slice(g * POOL_GROUP_DIM, (g + 1) * POOL_GROUP_DIM)
        u = u_ref[:, cols]
        a = jnp.concatenate([halo_ref[:, cols], u], axis=0)
        s = 1
        while s < w:
            a = a + _shift_rows(a, s)
            s *= 2
        cnt = jnp.minimum(pos + 1, w).astype(F32)
        pooled = a[POOL_HALO:, :] / cnt - u
        pm = jnp.dot(pooled.astype(BF16), pw_ref[g].astype(BF16), preferred_element_type=F32)
        o_ref[:, cols] = (pm * ps_ref[:, cols]).astype(o_ref.dtype)
    halo_ref[...] = u_ref[MIX_P_ROWS - POOL_HALO:, :]


def _mix_prompt_kernel(q_ref, k_ref, v_ref, g_ref, u_ref, cos_ref, sin_ref, din_ref, dq_ref, dk_ref,
                       dc_ref, rnw_ref, pw_ref, ps_ref, ret_ref, s_out_ref, pool_ref, buf_ref,
                       s_ref, halo_ref):
    t = pl.program_id(1)

    @pl.when(t == 0)
    def _():
        s_ref[...] = jnp.zeros_like(s_ref)
        halo_ref[...] = jnp.zeros_like(halo_ref)

    _ret_prompt_compute(q_ref, k_ref, v_ref, g_ref, cos_ref, sin_ref, din_ref, dq_ref, dk_ref, dc_ref,
                        rnw_ref, ret_ref, s_ref)
    _pool_prompt_compute(u_ref, pw_ref, ps_ref, pool_ref, halo_ref)

    @pl.when(t == pl.num_programs(1) - 1)
    def _():
        s_out_ref[...] = s_ref[...]
        buf_ref[...] = u_ref[MIX_P_ROWS - POOL_BUF:, :]


def mix_prompt(z, cos, sin, din, dq, dk, dc, rnw, pool_w, pool_scale, layer, state_prev, buf_prev):
    nt = SEQ // MIX_P_ROWS
    extra = _carried(state_prev) + _carried(buf_prev)
    zspec = lambda part: pl.BlockSpec((MIX_P_ROWS, D_RET), lambda b, t: (b * nt + t, part))
    whole = lambda a: pl.BlockSpec(a.shape, lambda b, t: (0,) * a.ndim)
    rope = pl.BlockSpec((MIX_P_ROWS, RET_HEAD_DIM // 2), lambda b, t: (t, 0))
    state_block = (RET_HEADS, RET_HEAD_DIM, RET_HEAD_DIM)
    mix_spec = pl.BlockSpec((MIX_P_ROWS, D_RET), lambda b, t: (b * nt + t, 0))
    n_in = 14
    return pl.pallas_call(
        _skip_refs(_mix_prompt_kernel, n_in, len(extra)),
        out_shape=(jax.ShapeDtypeStruct((P_ROWS, D_RET), BF16),
                   jax.ShapeDtypeStruct((DEPTH, BATCH) + state_block, F32),
                   jax.ShapeDtypeStruct((P_ROWS, D_POOL), BF16),
                   jax.ShapeDtypeStruct((DEPTH, BATCH, POOL_BUF, D_POOL), F32)),
        grid=(BATCH, nt),
        in_specs=[
            zspec(0), zspec(1), zspec(2), zspec(3), zspec(4), rope, rope,
            whole(din), whole(dq), whole(dk), whole(dc),
            pl.BlockSpec((None, 1, D_RET), lambda b, t: (layer, 0, 0)),
            pl.BlockSpec((None, 4, POOL_GROUP_DIM, POOL_GROUP_DIM), lambda b, t: (layer, 0, 0, 0)),
            pl.BlockSpec((None, 1, D_POOL), lambda b, t: (layer, 0, 0)),
        ] + [_ANY] * len(extra),
        out_specs=(
            mix_spec,
            pl.BlockSpec((None, None) + state_block, lambda b, t: (layer, b, 0, 0, 0)),
            mix_spec,
            pl.BlockSpec((None, None, POOL_BUF, D_POOL), lambda b, t: (layer, b, 0, 0)),
        ),
        scratch_shapes=[pltpu.VMEM(state_block, F32), pltpu.VMEM((POOL_HALO, D_POOL), F32)],
        input_output_aliases={n_in: 1, n_in + 1: 3} if extra else {},
        compiler_params=_cparams(("parallel", "arbitrary")),
        name="mix_prompt",
    )(z, z, z, z, z, cos, sin, din, dq, dk, dc, rnw, pool_w, pool_scale, *extra)


def _pool_sample_kernel(u_ref, buf_ref, pw_ref, ps_ref, o_ref, nbuf_ref):

    def ext(r, cols):
        if r < POOL_BUF:
            return buf_ref[:, r, cols]
        return u_ref[(r - POOL_BUF) * S_TILE_B:(r - POOL_BUF + 1) * S_TILE_B, cols]

    for g, w in enumerate(POOL_WINDOWS):
        cols = slice(g * POOL_GROUP_DIM, (g + 1) * POOL_GROUP_DIM)
        pooled = []
        for t in range(DEC_SEQ):
            win = ext(POOL_BUF + t, cols)
            for r in range(POOL_BUF + t - w + 1, POOL_BUF + t):
                win = win + ext(r, cols)
            cnt = float(min(t + PAST_LEN + 1, w))
            pooled.append(win / cnt - ext(POOL_BUF + t, cols))
        pooled = jnp.concatenate(pooled, axis=0)
        pm = jnp.dot(pooled.astype(BF16), pw_ref[g].astype(BF16), preferred_element_type=F32)
        o_ref[:, cols] = (pm * ps_ref[:, cols]).astype(o_ref.dtype)

    full = slice(0, D_POOL)
    for r in range(POOL_BUF):
        nbuf_ref[:, r, :] = ext(r + DEC_SEQ, full)


def pool_sample(z, state_pool, pool_w, pool_scale, layer, buf_prev):
    extra = _carried(buf_prev)
    buf_spec = pl.BlockSpec((None, S_TILE_B, POOL_BUF, D_POOL), lambda i: (layer, i, 0, 0))
    return pl.pallas_call(
        _skip_refs(_pool_sample_kernel, 4, len(extra)),
        out_shape=(jax.ShapeDtypeStruct((S_ROWS, D_POOL), BF16),
                   jax.ShapeDtypeStruct(state_pool.shape, F32)),
        grid=(N_S_TILES,),
        in_specs=[
            pl.BlockSpec((S_TILE_ROWS, D_POOL), lambda i: (i, 4)),
            buf_spec,
            pl.BlockSpec((None, 4, POOL_GROUP_DIM, POOL_GROUP_DIM), lambda i: (layer, 0, 0, 0)),
            pl.BlockSpec((None, 1, D_POOL), lambda i: (layer, 0, 0)),
        ] + [_ANY] * len(extra),
        out_specs=(pl.BlockSpec((S_TILE_ROWS, D_POOL), lambda i: (i, 0)), buf_spec),
        input_output_aliases={4: 1} if extra else {},
        compiler_params=_cparams(("arbitrary",)),
        name="pool_sample",
    )(z, state_pool, pool_w, pool_scale, *extra)


FINAL_ROWS = 512


def _final_norm_kernel(x_ref, nw_ref, y_ref):
    def body(rows):
        y_ref[rows, :] = _rms_rows(x_ref[rows, :], nw_ref[...])
    _for_row_chunks(FINAL_ROWS, 128, body)


def final_norm(x, nw):
    m, k = x.shape
    tile = pl.BlockSpec((FINAL_ROWS, k), lambda i: (i, 0))
    return pl.pallas_call(
        _final_norm_kernel,
        out_shape=jax.ShapeDtypeStruct((m, k), F32),
        grid=(m // FINAL_ROWS,),
        in_specs=[tile, pl.BlockSpec((1, k), lambda i: (0, 0))],
        out_specs=tile,
        compiler_params=_cparams(("parallel",)),
        name="final_norm",
    )(x, nw)


def _rope_tables(pos):
    half = RET_HEAD_DIM // 2
    inv = ROPE_BASE ** (-jnp.arange(half, dtype=F32) / half)
    ang = pos.astype(F32)[:, None] * inv[None, :]
    return jnp.cos(ang), jnp.sin(ang)


def _log_gamma():
    return jnp.log1p(-jnp.exp2(-5.0 - jnp.arange(RET_HEADS, dtype=F32)))


def _decay_tables(chunk):
    lg = _log_gamma()
    idx = jnp.arange(chunk, dtype=F32)
    diff = idx[:, None] - idx[None, :]
    decay_in = jnp.where(diff[None] >= 0.0,
                         jnp.exp(lg[:, None, None] * jnp.maximum(diff, 0.0)[None]), 0.0)
    decay_q = jnp.exp(lg[:, None] * (idx[None, :] + 1.0))
    decay_k = jnp.exp(lg[:, None] * (chunk - 1.0 - idx[None, :]))
    decay_chunk = jnp.exp(lg * chunk)
    return decay_in, decay_q, decay_k, decay_chunk


def _to_sample_rows(a):
    d = a.shape[-1]
    return a.reshape(N_S_TILES, S_TILE_B, DEC_SEQ, d).transpose(0, 2, 1, 3).reshape(S_ROWS, d)


def _from_sample_rows(a):
    d = a.shape[-1]
    return a.reshape(N_S_TILES, DEC_SEQ, S_TILE_B, d).transpose(0, 2, 1, 3).reshape(DEC_BATCH, DEC_SEQ, d)


def kernel(x_prompt, x_sample, mem_prompt, state_ret, state_pool, cache_mem_k, cache_mem_v,
           attn_norm_w, w_in, ret_norm_w, pool_w, pool_scale, w_out, xattn_norm_w, mem_norm_w,
           w_xq, w_mk, w_mv, w_xo, mlp_norm_w, w_up, w_down, final_norm_w):
    cos_p, sin_p = _rope_tables(jnp.arange(SEQ))
    cos_s, sin_s = _rope_tables(jnp.arange(DEC_SEQ) + PAST_LEN)
    cos_s = jnp.repeat(cos_s, S_TILE_B, axis=0)
    sin_s = jnp.repeat(sin_s, S_TILE_B, axis=0)
    din_p, dq_p, dk_p, dc_p = _decay_tables(RET_CHUNK)
    half = RET_HEAD_DIM // 2
    dq_p = jnp.broadcast_to(dq_p[:, :, None], (RET_HEADS, RET_CHUNK, half))
    dk_p = jnp.broadcast_to(dk_p[:, :, None], (RET_HEADS, RET_CHUNK, half))
    dc_p = jnp.broadcast_to(dc_p[:, None, None], (RET_HEADS, 1, RET_HEAD_DIM))
    din_s, dq_s, dk_s, dc_s = _decay_tables(DEC_SEQ)
    dec_s = jnp.concatenate([din_s.reshape(RET_HEADS, DEC_SEQ * DEC_SEQ), dq_s, dk_s,
                             dc_s[:, None], jnp.zeros((RET_HEADS, 7), F32)], axis=1)

    row3 = lambda a: a.reshape(DEPTH, 1, a.shape[-1])
    attn_nw, xattn_nw, mem_nw, mlp_nw = map(row3, (attn_norm_w, xattn_norm_w, mem_norm_w, mlp_norm_w))
    ret_nw, pool_sc = row3(ret_norm_w), row3(pool_scale)
    w_in_b = w_in.astype(BF16)

    xp = x_prompt.reshape(P_ROWS, D_MODEL)
    xs = _to_sample_rows(x_sample)
    mem = mem_prompt.reshape(BATCH * N_MEM, D_MODEL)

    ret_p = buf_p = mk_p = mv_p = ret_s = buf_s = None
    for l in range(DEPTH):
        zp, w_up_b, w_down_b = in_proj(xp, attn_nw, w_in_b, l, tm=TM_IN_PROJ, mlp_weights=(w_up, w_down))
        mix_rp, ret_p, mix_pp, buf_p = mix_prompt(zp, cos_p, sin_p, din_p, dq_p, dk_p, dc_p, ret_nw,
                                                  pool_w, pool_sc, l, ret_p, buf_p)
        xp, hp = matmul_residual_resident([mix_rp, mix_pp], w_out, xp, xattn_nw, l, tm=TM_RESIDENT)

        zs = in_proj(xs, attn_nw, w_in_b, l, tm=TM_IN_PROJ)
        mix_rs, ret_s = retention_sample(dec_s, zs, cos_s, sin_s, state_ret, ret_nw, l, ret_s)
        mix_ps, buf_s = pool_sample(zs, state_pool, pool_w, pool_sc, l, buf_s)
        xs, hs = matmul_residual_resident([mix_rs, mix_ps], w_out, xs, xattn_nw, l, tm=TM_RESIDENT)

        mk_p, mv_p = mem_kv_stacked(mem, mem_nw, w_mk, w_mv, l, mk_p, mv_p, tm=1024, tn=512)
        att_p = xq_projection(hp, w_xq, l, tm=TM_RESIDENT, memory_kv=(mk_p, mv_p))
        xp, hp = matmul_residual_resident([att_p], w_xo, xp, mlp_nw, l, tm=TM_RESIDENT)
        qs = _sample_rows_to_heads(xq_projection(hs, w_xq, l, tm=TM_RESIDENT))
        xp, att_s = mlp_residual(hp, xp, w_up_b, w_down_b, tm=TM_MLP, tf=512,
                                 sample_attention=(qs, cache_mem_k, cache_mem_v, l))

        xs, hs = matmul_residual_resident([_heads_to_sample_rows(att_s)], w_xo, xs, mlp_nw, l,
                                          tm=TM_RESIDENT)
        xs = mlp_residual(hs, xs, w_up_b, w_down_b, tm=S_ROWS, tf=1024)

    nw = final_norm_w.reshape(1, D_MODEL)
    mem_shape = (DEPTH, BATCH, N_MEM, MEM_HEADS, MEM_HEAD_DIM)
    return (final_norm(xp, nw).reshape(BATCH, SEQ, D_MODEL), _from_sample_rows(final_norm(xs, nw)),
            ret_p, buf_p, mk_p.reshape(mem_shape), mv_p.reshape(mem_shape), ret_s, buf_s)
```

```python
import functools

import jax
import jax.numpy as jnp
from jax import lax
from jax.experimental import pallas as pl
from jax.experimental.pallas import tpu as pltpu

D_MODEL = 2048
BATCH = 4
SEQ = 2048
DEPTH = 2
DEC_BATCH = 128
DEC_SEQ = 4
PAST_LEN = 16384
D_RET = 1024
RET_HEADS = 4
RET_HEAD_DIM = 256
D_POOL = 1024
POOL_WINDOWS = (2, 4, 8, 16)
POOL_GROUP_DIM = 256
POOL_BUF = 15
D_IN = 5120
N_MEM = 256
MEM_HEADS = 4
MEM_HEAD_DIM = 512
D_FF = 8192
RET_CHUNK = 128
ROPE_BASE = 10000.0
EPS = 1e-6

F32 = jnp.float32
BF16 = jnp.bfloat16

P_ROWS = BATCH * SEQ
S_ROWS = DEC_BATCH * DEC_SEQ
M_ROWS = P_ROWS + S_ROWS
S_TILE_B = 8
S_TILE_ROWS = S_TILE_B * DEC_SEQ
N_S_TILES = DEC_BATCH // S_TILE_B
S_BLK0 = P_ROWS // S_TILE_ROWS

TM_MLP = 1024
TM_RESIDENT = 512
TM_IN_PROJ = 256
VMEM_LIMIT = 58 * 1024 * 1024


def _cparams(sem):
    return pltpu.CompilerParams(dimension_semantics=sem, vmem_limit_bytes=VMEM_LIMIT)


_ANY = pl.BlockSpec(memory_space=pl.ANY)


def _skip_refs(kernel_fn, start, count):
    def wrapped(*refs):
        return kernel_fn(*refs[:start], *refs[start + count:])
    return wrapped


def _carried(prev):
    return [] if prev is None else [prev]


def _stacked_row_specs(sources, tm):
    specs, ends, start = [], [], 0
    for a in sources:
        n = a.shape[0] // tm
        specs.append(pl.BlockSpec((tm, a.shape[1]),
                                  lambda i, start=start, n=n: (jnp.clip(i - start, 0, n - 1), 0)))
        start += n
        ends.append(start)
    return specs, tuple(ends)


def _read_stacked_rows(refs, ends, rows=slice(None)):
    i = pl.program_id(0)
    x = refs[-1][rows, :]
    for ref, end in zip(reversed(refs[:-1]), reversed(ends[:-1])):
        x = jnp.where(i < end, ref[rows, :], x)
    return x


def _rms_rows(x, nw):
    ms = jnp.mean(x * x, axis=-1, keepdims=True)
    return x * lax.rsqrt(ms + EPS) * nw


def _for_row_chunks(n_rows, chunk, body):
    def step(c, carry):
        body(pl.ds(pl.multiple_of(c * chunk, chunk), chunk))
        return carry
    lax.fori_loop(0, n_rows // chunk, step, 0)


def _norm_chunk(n_rows):
    return 272 if n_rows % 272 == 0 else 128


def _cast_weight_once(w_ref, wb_ref):
    @pl.when(pl.program_id(0) == 0)
    def _():
        def body(rows):
            wb_ref[rows, :] = w_ref[rows, :].astype(BF16)
        _for_row_chunks(w_ref.shape[0], 256, body)


def _mem_kv_kernel(x_ref, nw_ref, wk_ref, wv_ref, k_ref, v_ref, h_ref, *, tm):
    @pl.when(pl.program_id(1) == 0)
    def _():
        def body(rows):
            h_ref[rows, :] = _rms_rows(x_ref[rows, :], nw_ref[...]).astype(BF16)
        _for_row_chunks(tm, _norm_chunk(tm), body)

    k_ref[...] = jnp.dot(h_ref[...], wk_ref[...].astype(BF16), preferred_element_type=F32)
    v_ref[...] = jnp.dot(h_ref[...], wv_ref[...].astype(BF16), preferred_element_type=F32)


def mem_kv_stacked(x, nw, w_k, w_v, layer, prev_k, prev_v, *, tm, tn):
    m, k = x.shape
    n = w_k.shape[-1]
    extra = _carried(prev_k) + _carried(prev_v)
    w_spec = pl.BlockSpec((None, k, tn), lambda i, j: (layer, 0, j))
    o_spec = pl.BlockSpec((None, tm, tn), lambda i, j: (layer, i, j))
    stacked = jax.ShapeDtypeStruct((DEPTH, m, n), F32)
    return pl.pallas_call(
        _skip_refs(functools.partial(_mem_kv_kernel, tm=tm), 4, len(extra)),
        out_shape=(stacked, stacked),
        grid=(m // tm, n // tn),
        in_specs=[
            pl.BlockSpec((tm, k), lambda i, j: (i, 0)),
            pl.BlockSpec((None, 1, k), lambda i, j: (layer, 0, 0)),
            w_spec, w_spec,
        ] + [_ANY] * len(extra),
        out_specs=(o_spec, o_spec),
        scratch_shapes=[pltpu.VMEM((tm, k), BF16)],
        input_output_aliases={4: 0, 5: 1} if extra else {},
        compiler_params=_cparams(("parallel", "arbitrary")),
        name="mem_kv",
    )(x, nw, w_k, w_v, *extra)


def _in_proj_kernel(*refs, src_ends):
    x_refs = refs[:len(src_ends)]
    nw_ref, w_ref, wu_ref, wd_ref, o_ref, wub_ref, wdb_ref = refs[len(src_ends):]
    h = _rms_rows(_read_stacked_rows(x_refs, src_ends), nw_ref[...]).astype(BF16)
    o_ref[...] = jnp.dot(h, w_ref[...], preferred_element_type=F32)
    wub_ref[...] = wu_ref[...].astype(BF16)
    wdb_ref[...] = wd_ref[...].astype(BF16)


def in_proj_and_mlp_weight_cast(x_sources, nw, w, w_up, w_down, layer, *, tm):
    x_specs, src_ends = _stacked_row_specs(x_sources, tm)
    m = src_ends[-1] * tm
    k, n = w.shape[1:]
    ff = w_up.shape[-1]
    slab = 256
    last = ff // slab - 1
    assert src_ends[-1] > last
    return pl.pallas_call(
        functools.partial(_in_proj_kernel, src_ends=src_ends),
        out_shape=(jax.ShapeDtypeStruct((m, n), F32),
                   jax.ShapeDtypeStruct((k, ff), BF16),
                   jax.ShapeDtypeStruct((ff, k), BF16)),
        grid=(src_ends[-1],),
        in_specs=x_specs + [
            pl.BlockSpec((None, 1, k), lambda i: (layer, 0, 0)),
            pl.BlockSpec((None, k, n), lambda i: (layer, 0, 0), pipeline_mode=pl.Buffered(1)),
            pl.BlockSpec((None, k, slab), lambda i: (layer, 0, jnp.minimum(i, last))),
            pl.BlockSpec((None, slab, k), lambda i: (layer, jnp.minimum(i, last), 0)),
        ],
        out_specs=(
            pl.BlockSpec((tm, n), lambda i: (i, 0)),
            pl.BlockSpec((k, slab), lambda i: (0, jnp.minimum(i, last))),
            pl.BlockSpec((slab, k), lambda i: (jnp.minimum(i, last), 0)),
        ),
        compiler_params=_cparams(("arbitrary",)),
        name="in_proj",
    )(*x_sources, nw, w, w_up, w_down)


def _resident_weight_spec(w, layer):
    return pl.BlockSpec((None,) + w.shape[1:], lambda i: (layer, 0, 0), pipeline_mode=pl.Buffered(1))


def _matmul_res_resident_kernel(*refs, n_parts, res_ends):
    a_refs = refs[:n_parts]
    w_ref = refs[n_parts]
    r_refs = refs[n_parts + 1:n_parts + 1 + len(res_ends)]
    nw_ref, o_ref, h_ref, wb_ref = refs[n_parts + 1 + len(res_ends):]
    _cast_weight_once(w_ref, wb_ref)
    half = o_ref.shape[0] // 2
    for rows in (slice(0, half), slice(half, 2 * half)):
        acc = _read_stacked_rows(r_refs, res_ends, rows)
        k0 = 0
        for a_ref in a_refs:
            k1 = k0 + a_ref.shape[1]
            acc = acc + jnp.dot(a_ref[rows, :], wb_ref[k0:k1, :], preferred_element_type=F32)
            k0 = k1
        o_ref[rows, :] = acc
        h_ref[rows, :] = _rms_rows(acc, nw_ref[...]).astype(BF16)


def matmul_residual_resident(a_parts, w, res_sources, nw, layer, *, tm, n_tiles=None):
    res_specs, res_ends = _stacked_row_specs(res_sources, tm)
    m, n = (n_tiles or res_ends[-1]) * tm, w.shape[-1]
    k = sum(a.shape[1] for a in a_parts)
    row_tile = lambda width: pl.BlockSpec((tm, width), lambda i: (i, 0))
    return pl.pallas_call(
        functools.partial(_matmul_res_resident_kernel, n_parts=len(a_parts), res_ends=res_ends),
        out_shape=(jax.ShapeDtypeStruct((m, n), F32), jax.ShapeDtypeStruct((m, n), BF16)),
        grid=(m // tm,),
        in_specs=[row_tile(a.shape[1]) for a in a_parts] + [_resident_weight_spec(w, layer)]
        + res_specs + [pl.BlockSpec((None, 1, n), lambda i: (layer, 0, 0))],
        out_specs=(row_tile(n), row_tile(n)),
        scratch_shapes=[pltpu.VMEM((k, n), BF16)],
        compiler_params=_cparams(("arbitrary",)),
        name="matmul_residual_resident",
    )(*a_parts, w, *res_sources, nw)


MLP_TN = 512
MLP_KC = 4096
MLP_K_STEPS = D_FF // MLP_KC
MLP_DOWN_STEPS = (D_MODEL // MLP_TN) * MLP_K_STEPS


def _mlp_kernel(h_ref, x_ref, wu_ref, wd_ref, *refs, tf):
    if len(refs) == 2:
        o_ref, hid_ref = refs
        attention_job = None
    else:
        q_ref, k_ref, v_ref, o_ref, att_ref, hid_ref = refs
        attention_job = (q_ref, k_ref, v_ref, att_ref)
    up_steps = D_FF // tf
    s = pl.program_id(1)

    @pl.when(s < up_steps)
    def _():
        hid = jnp.dot(h_ref[...], wu_ref[...], preferred_element_type=F32)
        per_chunk = MLP_KC // tf
        c0 = pl.multiple_of((s % per_chunk) * tf, tf)
        hid_ref[s // per_chunk, :, pl.ds(c0, tf)] = jnp.square(jnp.maximum(hid, 0.0)).astype(BF16)
        if attention_job is not None:
            _sample_attention(*attention_job)

    @pl.when(s >= up_steps)
    def _():
        kc = (s - up_steps) % MLP_K_STEPS
        part = jnp.dot(hid_ref[kc], wd_ref[...], preferred_element_type=F32)

        @pl.when(kc == 0)
        def _():
            o_ref[...] = x_ref[...] + part

        @pl.when(kc != 0)
        def _():
            o_ref[...] += part


def mlp_residual(h, x, w_up, w_down, *, tm, tf, sample_attention=None):
    m, k = x.shape
    up_steps = D_FF // tf
    up = lambda s: jnp.minimum(s, up_steps - 1)
    down = lambda s: jnp.maximum(s - up_steps, 0)
    out_tile = pl.BlockSpec((tm, MLP_TN), lambda i, s: (i, down(s) // MLP_K_STEPS))
    in_specs = [
        pl.BlockSpec((tm, k), lambda i, s: (i, 0)),
        out_tile,
        pl.BlockSpec((k, tf), lambda i, s: (0, up(s))),
        pl.BlockSpec((MLP_KC, MLP_TN), lambda i, s: (down(s) % MLP_K_STEPS, down(s) // MLP_K_STEPS)),
    ]
    out_shape = [jax.ShapeDtypeStruct((m, k), F32)]
    out_specs = [out_tile]
    operands = [h, x, w_up, w_down]
    if sample_attention is not None:
        qs, cache_k, cache_v, layer = sample_attention
        assert (m // tm) * up_steps == qs.shape[0]
        batch = lambda i, s: i * up_steps + up(s)
        q_spec = pl.BlockSpec((None,) + qs.shape[1:], lambda i, s: (batch(i, s), 0, 0))
        kv_spec = pl.BlockSpec((None, None) + cache_k.shape[2:], lambda i, s: (layer, batch(i, s), 0, 0, 0))
        in_specs += [q_spec, kv_spec, kv_spec]
        out_shape += [jax.ShapeDtypeStruct(qs.shape, BF16)]
        out_specs += [q_spec]
        operands += [qs, cache_k, cache_v]
    out = pl.pallas_call(
        functools.partial(_mlp_kernel, tf=tf),
        out_shape=tuple(out_shape),
        grid=(m // tm, up_steps + MLP_DOWN_STEPS),
        in_specs=in_specs,
        out_specs=tuple(out_specs),
        scratch_shapes=[pltpu.VMEM((MLP_K_STEPS, tm, MLP_KC), BF16)],
        compiler_params=_cparams(("arbitrary", "arbitrary")),
        name="mlp_residual",
    )(*operands)
    return out[0] if sample_attention is None else out


def _rotate(x, cos, sin):
    half = x.shape[-1] // 2
    x1, x2 = x[:, :half], x[:, half:]
    return jnp.concatenate([x1 * cos - x2 * sin, x2 * cos + x1 * sin], axis=-1)


def _head_norm_gate(o, g, rnw):
    o = o * lax.rsqrt(jnp.mean(o * o, axis=-1, keepdims=True) + EPS)
    return (g * jax.nn.sigmoid(g)) * (o * rnw)


def _dot_t_lhs(a, b):
    return lax.dot_general(a, b, (((0,), (0,)), ((), ())), preferred_element_type=F32)


def _dot_t_rhs(a, b):
    return lax.dot_general(a, b, (((1,), (1,)), ((), ())), preferred_element_type=F32)


MIX_P_ROWS = 512


def _ret_prompt_compute(q_ref, k_ref, v_ref, g_ref, cos_ref, sin_ref, din_ref, dq_ref, dk_ref, dc_ref,
                        rnw_ref, o_ref, s_ref):
    for c in range(MIX_P_ROWS // RET_CHUNK):
        rows = slice(c * RET_CHUNK, (c + 1) * RET_CHUNK)
        cos, sin = cos_ref[rows, :], sin_ref[rows, :]
        for h in range(RET_HEADS):
            cols = slice(h * RET_HEAD_DIM, (h + 1) * RET_HEAD_DIM)
            dq = jnp.concatenate([dq_ref[h], dq_ref[h]], axis=-1)
            dk = jnp.concatenate([dk_ref[h], dk_ref[h]], axis=-1)
            q = _rotate(q_ref[rows, cols], cos, sin)
            k = _rotate(k_ref[rows, cols], cos, sin) * (RET_HEAD_DIM ** -0.5)
            vb = v_ref[rows, cols].astype(BF16)
            s = s_ref[h]
            scores = _dot_t_rhs(q.astype(BF16), k.astype(BF16)) * din_ref[h]
            o = jnp.dot(scores.astype(BF16), vb, preferred_element_type=F32)
            o = o + jnp.dot((q * dq).astype(BF16), s.astype(BF16), preferred_element_type=F32)
            s_ref[h] = s * dc_ref[h] + _dot_t_lhs((k * dk).astype(BF16), vb)
            o_ref[rows, cols] = _head_norm_gate(o, g_ref[rows, cols], rnw_ref[:, cols]).astype(o_ref.dtype)


def _ret_sample_head(h, dec_ref, z_ref, cos, sin, st_ref, rnw_ref, o_ref, st_out_ref):
    cols = slice(h * RET_HEAD_DIM, (h + 1) * RET_HEAD_DIM)
    part = lambda p: z_ref[:, p * D_RET + h * RET_HEAD_DIM:p * D_RET + (h + 1) * RET_HEAD_DIM]
    q = _rotate(part(0), cos, sin)
    k = _rotate(part(1), cos, sin) * (RET_HEAD_DIM ** -0.5)
    v = part(2)
    g = part(3)
    slab = lambda a, t: a[t * S_TILE_B:(t + 1) * S_TILE_B, :]

    intra = []
    for t in range(DEC_SEQ):
        acc = None
        for j in range(t + 1):
            w = jnp.sum(slab(q, t) * slab(k, j), axis=-1, keepdims=True) * dec_ref[h, t * DEC_SEQ + j]
            acc = w * slab(v, j) if acc is None else acc + w * slab(v, j)
        intra.append(acc)
    o = jnp.concatenate(intra, axis=0)

    dq_rows = jnp.concatenate(
        [jnp.full((S_TILE_B, 1), 1.0, F32) * dec_ref[h, 16 + t] for t in range(DEC_SEQ)], axis=0)
    dk_rows = jnp.concatenate(
        [jnp.full((S_TILE_B, 1), 1.0, F32) * dec_ref[h, 20 + t] for t in range(DEC_SEQ)], axis=0)
    qd = (q * dq_rows).astype(BF16)
    kd = k * dk_rows
    vb = v.astype(BF16)
    dchunk = dec_ref[h, 24]
    row_b = lax.broadcasted_iota(jnp.int32, (S_TILE_ROWS, 1), 0) % S_TILE_B
    for b in range(S_TILE_B):
        mine = row_b == b
        s = st_ref[b, h]
        o = o + jnp.where(mine, jnp.dot(qd, s.astype(BF16), preferred_element_type=F32), 0.0)
        st_out_ref[b, h] = s * dchunk + _dot_t_lhs(jnp.where(mine, kd, 0.0).astype(BF16), vb)
    o_ref[:, cols] = _head_norm_gate(o, g, rnw_ref[:, cols]).astype(o_ref.dtype)


def _ret_sample_kernel(dec_ref, z_ref, cos_ref, sin_ref, st_ref, rnw_ref, o_ref, st_out_ref):
    cos, sin = cos_ref[...], sin_ref[...]
    for h in range(RET_HEADS):
        _ret_sample_head(h, dec_ref, z_ref, cos, sin, st_ref, rnw_ref, o_ref, st_out_ref)


def retention_sample(dec, z, cos_s, sin_s, state_ret, rnw, mix, layer, state_prev):
    extra = [mix] + _carried(state_prev)
    state_spec = pl.BlockSpec((None, S_TILE_B, RET_HEADS, RET_HEAD_DIM, RET_HEAD_DIM),
                              lambda i: (layer, i, 0, 0, 0))
    return pl.pallas_call(
        _skip_refs(_ret_sample_kernel, 6, len(extra)),
        out_shape=(jax.ShapeDtypeStruct(mix.shape, mix.dtype),
                   jax.ShapeDtypeStruct(state_ret.shape, F32)),
        grid=(N_S_TILES,),
        in_specs=[
            pl.BlockSpec(memory_space=pltpu.SMEM),
            pl.BlockSpec((S_TILE_ROWS, D_IN), lambda i: (S_BLK0 + i, 0)),
            pl.BlockSpec((S_TILE_ROWS, RET_HEAD_DIM // 2), lambda i: (0, 0)),
            pl.BlockSpec((S_TILE_ROWS, RET_HEAD_DIM // 2), lambda i: (0, 0)),
            state_spec,
            pl.BlockSpec((None, 1, D_RET), lambda i: (layer, 0, 0)),
        ] + [_ANY] * len(extra),
        out_specs=(pl.BlockSpec((S_TILE_ROWS, D_RET), lambda i: (S_BLK0 + i, 0)), state_spec),
        input_output_aliases={6 + n: n for n in range(len(extra))},
        compiler_params=_cparams(("arbitrary",)),
        name="retention_sample",
    )(dec, z, cos_s, sin_s, state_ret, rnw, *extra)


POOL_HALO = 16


def _shift_rows(a, s):
    return pltpu.roll(a, s, axis=0)


def _pool_prompt_compute(u_ref, pw_ref, ps_ref, o_ref, halo_ref):
    pos = pl.program_id(1) * MIX_P_ROWS + lax.broadcasted_iota(jnp.int32, (MIX_P_ROWS, 1), 0)
    for g, w in enumerate(POOL_WINDOWS):
        cols = slice(g * POOL_GROUP_DIM, (g + 1) * POOL_GROUP_DIM)
        u = u_ref[:, cols]
        a = jnp.concatenate([halo_ref[:, cols], u], axis=0)
        s = 1
        while s < w:
            a = a + _shift_rows(a, s)
            s *= 2
        cnt = jnp.minimum(pos + 1, w).astype(F32)
        pooled = a[POOL_HALO:, :] / cnt - u
        pm = jnp.dot(pooled.astype(BF16), pw_ref[g].astype(BF16), preferred_element_type=F32)
        o_ref[:, cols] = (pm * ps_ref[:, cols]).astype(o_ref.dtype)
    halo_ref[...] = u_ref[MIX_P_ROWS - POOL_HALO:, :]


def _mix_prompt_kernel(q_ref, k_ref, v_ref, g_ref, u_ref, cos_ref, sin_ref, din_ref, dq_ref, dk_ref,
                       dc_ref, rnw_ref, pw_ref, ps_ref, ret_ref, s_out_ref, pool_ref, buf_ref,
                       s_ref, halo_ref):
    t = pl.program_id(1)

    @pl.when(t == 0)
    def _():
        s_ref[...] = jnp.zeros_like(s_ref)
        halo_ref[...] = jnp.zeros_like(halo_ref)

    _ret_prompt_compute(q_ref, k_ref, v_ref, g_ref, cos_ref, sin_ref, din_ref, dq_ref, dk_ref, dc_ref,
                        rnw_ref, ret_ref, s_ref)
    _pool_prompt_compute(u_ref, pw_ref, ps_ref, pool_ref, halo_ref)

    @pl.when(t == pl.num_programs(1) - 1)
    def _():
        s_out_ref[...] = s_ref[...]
        buf_ref[...] = u_ref[MIX_P_ROWS - POOL_BUF:, :]


def mix_prompt(z, cos, sin, din, dq, dk, dc, rnw, pool_w, pool_scale, layer, state_prev, buf_prev):
    nt = SEQ // MIX_P_ROWS
    extra = _carried(state_prev) + _carried(buf_prev)
    zspec = lambda part: pl.BlockSpec((MIX_P_ROWS, D_RET), lambda b, t: (b * nt + t, part))
    whole = lambda a: pl.BlockSpec(a.shape, lambda b, t: (0,) * a.ndim)
    rope = pl.BlockSpec((MIX_P_ROWS, RET_HEAD_DIM // 2), lambda b, t: (t, 0))
    state_block = (RET_HEADS, RET_HEAD_DIM, RET_HEAD_DIM)
    mix_spec = pl.BlockSpec((MIX_P_ROWS, D_RET), lambda b, t: (b * nt + t, 0))
    n_in = 14
    return pl.pallas_call(
        _skip_refs(_mix_prompt_kernel, n_in, len(extra)),
        out_shape=(jax.ShapeDtypeStruct((M_ROWS, D_RET), BF16),
                   jax.ShapeDtypeStruct((DEPTH, BATCH) + state_block, F32),
                   jax.ShapeDtypeStruct((M_ROWS, D_POOL), BF16),
                   jax.ShapeDtypeStruct((DEPTH, BATCH, POOL_BUF, D_POOL), F32)),
        grid=(BATCH, nt),
        in_specs=[
            zspec(0), zspec(1), zspec(2), zspec(3), zspec(4), rope, rope,
            whole(din), whole(dq), whole(dk), whole(dc),
            pl.BlockSpec((None, 1, D_RET), lambda b, t: (layer, 0, 0)),
            pl.BlockSpec((None, 4, POOL_GROUP_DIM, POOL_GROUP_DIM), lambda b, t: (layer, 0, 0, 0)),
            pl.BlockSpec((None, 1, D_POOL), lambda b, t: (layer, 0, 0)),
        ] + [_ANY] * len(extra),
        out_specs=(
            mix_spec,
            pl.BlockSpec((None, None) + state_block, lambda b, t: (layer, b, 0, 0, 0)),
            mix_spec,
            pl.BlockSpec((None, None, POOL_BUF, D_POOL), lambda b, t: (layer, b, 0, 0)),
        ),
        scratch_shapes=[pltpu.VMEM(state_block, F32), pltpu.VMEM((POOL_HALO, D_POOL), F32)],
        input_output_aliases={n_in: 1, n_in + 1: 3} if extra else {},
        compiler_params=_cparams(("parallel", "arbitrary")),
        name="mix_prompt",
    )(z, z, z, z, z, cos, sin, din, dq, dk, dc, rnw, pool_w, pool_scale, *extra)


def _pool_sample_kernel(u_ref, buf_ref, pw_ref, ps_ref, o_ref, nbuf_ref):

    def ext(r, cols):
        if r < POOL_BUF:
            return buf_ref[:, r, cols]
        return u_ref[(r - POOL_BUF) * S_TILE_B:(r - POOL_BUF + 1) * S_TILE_B, cols]

    for g, w in enumerate(POOL_WINDOWS):
        cols = slice(g * POOL_GROUP_DIM, (g + 1) * POOL_GROUP_DIM)
        pooled = []
        for t in range(DEC_SEQ):
            win = ext(POOL_BUF + t, cols)
            for r in range(POOL_BUF + t - w + 1, POOL_BUF + t):
                win = win + ext(r, cols)
            cnt = float(min(t + PAST_LEN + 1, w))
            pooled.append(win / cnt - ext(POOL_BUF + t, cols))
        pooled = jnp.concatenate(pooled, axis=0)
        pm = jnp.dot(pooled.astype(BF16), pw_ref[g].astype(BF16), preferred_element_type=F32)
        o_ref[:, cols] = (pm * ps_ref[:, cols]).astype(o_ref.dtype)

    full = slice(0, D_POOL)
    for r in range(POOL_BUF):
        nbuf_ref[:, r, :] = ext(r + DEC_SEQ, full)


def pool_sample(z, state_pool, pool_w, pool_scale, mix, layer, buf_prev):
    extra = [mix] + _carried(buf_prev)
    return pl.pallas_call(
        _skip_refs(_pool_sample_kernel, 4, len(extra)),
        out_shape=(jax.ShapeDtypeStruct(mix.shape, mix.dtype),
                   jax.ShapeDtypeStruct(state_pool.shape, F32)),
        grid=(N_S_TILES,),
        in_specs=[
            pl.BlockSpec((S_TILE_ROWS, D_POOL), lambda i: (S_BLK0 + i, 4)),
            pl.BlockSpec((None, S_TILE_B, POOL_BUF, D_POOL), lambda i: (layer, i, 0, 0)),
            pl.BlockSpec((None, 4, POOL_GROUP_DIM, POOL_GROUP_DIM), lambda i: (layer, 0, 0, 0)),
            pl.BlockSpec((None, 1, D_POOL), lambda i: (layer, 0, 0)),
        ] + [_ANY] * len(extra),
        out_specs=(
            pl.BlockSpec((S_TILE_ROWS, D_POOL), lambda i: (S_BLK0 + i, 0)),
            pl.BlockSpec((None, S_TILE_B, POOL_BUF, D_POOL), lambda i: (layer, i, 0, 0)),
        ),
        input_output_aliases={4 + n: n for n in range(len(extra))},
        compiler_params=_cparams(("arbitrary",)),
        name="pool_sample",
    )(z, state_pool, pool_w, pool_scale, *extra)


def _softmax_rows(s):
    m = jnp.max(s, axis=-1, keepdims=True)
    e = jnp.exp(s - m)
    return e / jnp.sum(e, axis=-1, keepdims=True)


def _xq_prompt_attention_kernel(h_ref, w_ref, mk_ref, mv_ref, o_ref, wb_ref, *, prompt_tiles):
    _cast_weight_once(w_ref, wb_ref)

    @pl.when(pl.program_id(0) < prompt_tiles)
    def _():
        half = o_ref.shape[0] // 2
        for rows in (slice(0, half), slice(half, 2 * half)):
            q = jnp.dot(h_ref[rows, :], wb_ref[...], preferred_element_type=F32).astype(BF16)
            for hd in range(MEM_HEADS):
                cols = slice(hd * MEM_HEAD_DIM, (hd + 1) * MEM_HEAD_DIM)
                s = _dot_t_rhs(q[:, cols], mk_ref[:, cols].astype(BF16)) * (MEM_HEAD_DIM ** -0.5)
                p = _softmax_rows(s)
                o_ref[rows, cols] = jnp.dot(p.astype(BF16), mv_ref[:, cols].astype(BF16),
                                            preferred_element_type=F32).astype(o_ref.dtype)

    @pl.when(pl.program_id(0) >= prompt_tiles)
    def _():
        o_ref[...] = jnp.dot(h_ref[...], wb_ref[...], preferred_element_type=F32).astype(o_ref.dtype)


def xq_and_prompt_attention(h, w, mk, mv, layer, *, tm):
    m, k = h.shape
    tiles_per_batch = SEQ // tm
    prompt_tiles = P_ROWS // tm
    kv_spec = pl.BlockSpec((None, N_MEM, k),
                           lambda i: (layer, jnp.minimum(i // tiles_per_batch, BATCH - 1), 0))
    return pl.pallas_call(
        functools.partial(_xq_prompt_attention_kernel, prompt_tiles=prompt_tiles),
        out_shape=jax.ShapeDtypeStruct((m, k), BF16),
        grid=(m // tm,),
        in_specs=[pl.BlockSpec((tm, k), lambda i: (i, 0)), _resident_weight_spec(w, layer),
                  kv_spec, kv_spec],
        out_specs=pl.BlockSpec((tm, k), lambda i: (i, 0)),
        scratch_shapes=[pltpu.VMEM((k, k), BF16)],
        compiler_params=_cparams(("arbitrary",)),
        name="xq_prompt_attention",
    )(h, w, mk, mv)


XATTN_S_ROWS = MEM_HEADS * DEC_SEQ
XATTN_S_KEYS = N_MEM * MEM_HEADS


def _sample_attention(q_ref, k_ref, v_ref, o_ref):
    row_h = lax.broadcasted_iota(jnp.int32, (XATTN_S_ROWS, XATTN_S_KEYS), 0) // DEC_SEQ
    col_h = lax.broadcasted_iota(jnp.int32, (XATTN_S_ROWS, XATTN_S_KEYS), 1) % MEM_HEADS
    k2 = k_ref[...].reshape(XATTN_S_KEYS, MEM_HEAD_DIM).astype(BF16)
    v2 = v_ref[...].reshape(XATTN_S_KEYS, MEM_HEAD_DIM).astype(BF16)
    s = _dot_t_rhs(q_ref[...], k2) * (MEM_HEAD_DIM ** -0.5)
    s = jnp.where(row_h == col_h, s, -jnp.inf)
    p = _softmax_rows(s)
    o_ref[...] = jnp.dot(p.astype(BF16), v2, preferred_element_type=F32).astype(o_ref.dtype)


def _sample_rows_to_heads(a):
    a = a.reshape(N_S_TILES, DEC_SEQ, S_TILE_B, MEM_HEADS, MEM_HEAD_DIM)
    return a.transpose(0, 2, 3, 1, 4).reshape(DEC_BATCH, XATTN_S_ROWS, MEM_HEAD_DIM)


def _heads_to_sample_rows(a):
    a = a.reshape(N_S_TILES, S_TILE_B, MEM_HEADS, DEC_SEQ, MEM_HEAD_DIM)
    return a.transpose(0, 3, 1, 2, 4).reshape(S_ROWS, D_MODEL)


FINAL_ROWS = 512


def _final_norm_kernel(x_ref, nw_ref, y_ref):
    def body(rows):
        y_ref[rows, :] = _rms_rows(x_ref[rows, :], nw_ref[...])
    _for_row_chunks(FINAL_ROWS, _norm_chunk(FINAL_ROWS), body)


def final_norm(x, nw):
    m, k = x.shape
    tile = pl.BlockSpec((FINAL_ROWS, k), lambda i: (i, 0))
    return pl.pallas_call(
        _final_norm_kernel,
        out_shape=jax.ShapeDtypeStruct((m, k), F32),
        grid=(m // FINAL_ROWS,),
        in_specs=[tile, pl.BlockSpec((1, k), lambda i: (0, 0))],
        out_specs=tile,
        compiler_params=_cparams(("parallel",)),
        name="final_norm",
    )(x, nw)


def _rope_tables(pos):
    half = RET_HEAD_DIM // 2
    inv = ROPE_BASE ** (-jnp.arange(half, dtype=F32) / half)
    ang = pos.astype(F32)[:, None] * inv[None, :]
    return jnp.cos(ang), jnp.sin(ang)


def _log_gamma():
    return jnp.log1p(-jnp.exp2(-5.0 - jnp.arange(RET_HEADS, dtype=F32)))


def _decay_tables(chunk):
    lg = _log_gamma()
    idx = jnp.arange(chunk, dtype=F32)
    diff = idx[:, None] - idx[None, :]
    decay_in = jnp.where(diff[None] >= 0.0,
                         jnp.exp(lg[:, None, None] * jnp.maximum(diff, 0.0)[None]), 0.0)
    decay_q = jnp.exp(lg[:, None] * (idx[None, :] + 1.0))
    decay_k = jnp.exp(lg[:, None] * (chunk - 1.0 - idx[None, :]))
    decay_chunk = jnp.exp(lg * chunk)
    return decay_in, decay_q, decay_k, decay_chunk


def _to_sample_rows(a):
    d = a.shape[-1]
    return a.reshape(N_S_TILES, S_TILE_B, DEC_SEQ, d).transpose(0, 2, 1, 3).reshape(S_ROWS, d)


def _from_sample_rows(a):
    d = a.shape[-1]
    return a.reshape(N_S_TILES, DEC_SEQ, S_TILE_B, d).transpose(0, 2, 1, 3).reshape(DEC_BATCH, DEC_SEQ, d)


def kernel(x_prompt, x_sample, mem_prompt, state_ret, state_pool, cache_mem_k, cache_mem_v,
           attn_norm_w, w_in, ret_norm_w, pool_w, pool_scale, w_out, xattn_norm_w, mem_norm_w,
           w_xq, w_mk, w_mv, w_xo, mlp_norm_w, w_up, w_down, final_norm_w):
    cos_p, sin_p = _rope_tables(jnp.arange(SEQ))
    cos_s, sin_s = _rope_tables(jnp.arange(DEC_SEQ) + PAST_LEN)
    cos_s = jnp.repeat(cos_s, S_TILE_B, axis=0)
    sin_s = jnp.repeat(sin_s, S_TILE_B, axis=0)
    din_p, dq_p, dk_p, dc_p = _decay_tables(RET_CHUNK)
    half = RET_HEAD_DIM // 2
    dq_p = jnp.broadcast_to(dq_p[:, :, None], (RET_HEADS, RET_CHUNK, half))
    dk_p = jnp.broadcast_to(dk_p[:, :, None], (RET_HEADS, RET_CHUNK, half))
    dc_p = jnp.broadcast_to(dc_p[:, None, None], (RET_HEADS, 1, RET_HEAD_DIM))
    din_s, dq_s, dk_s, dc_s = _decay_tables(DEC_SEQ)
    dec_s = jnp.concatenate([din_s.reshape(RET_HEADS, DEC_SEQ * DEC_SEQ), dq_s, dk_s,
                             dc_s[:, None], jnp.zeros((RET_HEADS, 7), F32)], axis=1)

    row3 = lambda a: a.reshape(DEPTH, 1, a.shape[-1])
    attn_nw, xattn_nw, mem_nw, mlp_nw = map(row3, (attn_norm_w, xattn_norm_w, mem_norm_w, mlp_norm_w))
    ret_nw, pool_sc = row3(ret_norm_w), row3(pool_scale)
    w_in_b = w_in.astype(BF16)

    x_sources = [x_prompt.reshape(P_ROWS, D_MODEL), _to_sample_rows(x_sample)]
    mem = mem_prompt.reshape(BATCH * N_MEM, D_MODEL)

    ret_p = buf_p = mk_p = mv_p = ret_s = buf_s = None
    for l in range(DEPTH):
        z, w_up_b, w_down_b = in_proj_and_mlp_weight_cast(x_sources, attn_nw, w_in_b, w_up, w_down, l,
                                                          tm=TM_IN_PROJ)
        mix_r, ret_p, mix_p, buf_p = mix_prompt(z, cos_p, sin_p, din_p, dq_p, dk_p, dc_p, ret_nw,
                                                pool_w, pool_sc, l, ret_p, buf_p)
        mix_r, ret_s = retention_sample(dec_s, z, cos_s, sin_s, state_ret, ret_nw, mix_r, l, ret_s)
        mix_p, buf_s = pool_sample(z, state_pool, pool_w, pool_sc, mix_p, l, buf_s)
        x, h = matmul_residual_resident([mix_r, mix_p], w_out, x_sources, xattn_nw, l, tm=TM_RESIDENT)

        mk_p, mv_p = mem_kv_stacked(mem, mem_nw, w_mk, w_mv, l, mk_p, mv_p, tm=1024, tn=512)
        att = xq_and_prompt_attention(h, w_xq, mk_p, mv_p, l, tm=TM_RESIDENT)
        xp, hp = matmul_residual_resident([att], w_xo, [x], mlp_nw, l, tm=TM_RESIDENT,
                                          n_tiles=P_ROWS // TM_RESIDENT)
        qs = _sample_rows_to_heads(att[P_ROWS:])
        xp, att_s = mlp_residual(hp, xp, w_up_b, w_down_b, tm=TM_MLP, tf=512,
                                 sample_attention=(qs, cache_mem_k, cache_mem_v, l))
        xs, hs = matmul_residual_resident([_heads_to_sample_rows(att_s)], w_xo, [x[P_ROWS:]], mlp_nw, l,
                                          tm=TM_RESIDENT)
        xs = mlp_residual(hs, xs, w_up_b, w_down_b, tm=S_ROWS, tf=1024)
        x_sources = [xp, xs]

    nw = final_norm_w.reshape(1, D_MODEL)
    y_p, y_s = final_norm(xp, nw), final_norm(xs, nw)
    mem_shape = (DEPTH, BATCH, N_MEM, MEM_HEADS, MEM_HEAD_DIM)
    return (y_p.reshape(BATCH, SEQ, D_MODEL), _from_sample_rows(y_s),
            ret_p, buf_p, mk_p.reshape(mem_shape), mv_p.reshape(mem_shape), ret_s, buf_s)
```

```python
import functools

import jax
import jax.numpy as jnp
from jax import lax
from jax.experimental import pallas as pl
from jax.experimental.pallas import tpu as pltpu

D_MODEL = 2048
BATCH = 4
SEQ = 2048
DEPTH = 2
DEC_BATCH = 128
DEC_SEQ = 4
PAST_LEN = 16384
D_RET = 1024
RET_HEADS = 4
RET_HEAD_DIM = 256
D_POOL = 1024
POOL_WINDOWS = (2, 4, 8, 16)
POOL_GROUP_DIM = 256
POOL_BUF = 15
D_IN = 5120
N_MEM = 256
MEM_HEADS = 4
MEM_HEAD_DIM = 512
D_FF = 8192
RET_CHUNK = 128
ROPE_BASE = 10000.0
EPS = 1e-6

F32 = jnp.float32
BF16 = jnp.bfloat16

P_ROWS = BATCH * SEQ
S_ROWS = DEC_BATCH * DEC_SEQ
M_ROWS = P_ROWS + S_ROWS
S_TILE_B = 8
S_TILE_ROWS = S_TILE_B * DEC_SEQ
N_S_TILES = DEC_BATCH // S_TILE_B
S_BLK0 = P_ROWS // S_TILE_ROWS

TM_MLP = 1024
TM_RESIDENT = 512
TM_IN_PROJ = 256
VMEM_LIMIT = 58 * 1024 * 1024


def _cparams(sem):
    return pltpu.CompilerParams(dimension_semantics=sem, vmem_limit_bytes=VMEM_LIMIT)


_ANY = pl.BlockSpec(memory_space=pl.ANY)


def _skip_refs(kernel_fn, start, count):
    def wrapped(*refs):
        return kernel_fn(*refs[:start], *refs[start + count:])
    return wrapped


def _carried(prev):
    return [] if prev is None else [prev]


def _stacked_row_specs(sources, tm):
    specs, ends, start = [], [], 0
    for a in sources:
        n = a.shape[0] // tm
        specs.append(pl.BlockSpec((tm, a.shape[1]),
                                  lambda i, start=start, n=n: (jnp.clip(i - start, 0, n - 1), 0)))
        start += n
        ends.append(start)
    return specs, tuple(ends)


def _read_stacked_rows(refs, ends, rows=slice(None)):
    i = pl.program_id(0)
    x = refs[-1][rows, :]
    for ref, end in zip(reversed(refs[:-1]), reversed(ends[:-1])):
        x = jnp.where(i < end, ref[rows, :], x)
    return x


def _rms_rows(x, nw):
    ms = jnp.mean(x * x, axis=-1, keepdims=True)
    return x * lax.rsqrt(ms + EPS) * nw


def _for_row_chunks(n_rows, chunk, body):
    def step(c, carry):
        body(pl.ds(pl.multiple_of(c * chunk, chunk), chunk))
        return carry
    lax.fori_loop(0, n_rows // chunk, step, 0)


def _norm_chunk(n_rows):
    return 272 if n_rows % 272 == 0 else 128


def _cast_weight_once(w_ref, wb_ref):
    @pl.when(pl.program_id(0) == 0)
    def _():
        def body(rows):
            wb_ref[rows, :] = w_ref[rows, :].astype(BF16)
        _for_row_chunks(w_ref.shape[0], 256, body)


def _mem_kv_kernel(x_ref, nw_ref, wk_ref, wv_ref, k_ref, v_ref, h_ref, *, tm):
    @pl.when(pl.program_id(1) == 0)
    def _():
        def body(rows):
            h_ref[rows, :] = _rms_rows(x_ref[rows, :], nw_ref[...]).astype(BF16)
        _for_row_chunks(tm, _norm_chunk(tm), body)

    k_ref[...] = jnp.dot(h_ref[...], wk_ref[...].astype(BF16), preferred_element_type=F32)
    v_ref[...] = jnp.dot(h_ref[...], wv_ref[...].astype(BF16), preferred_element_type=F32)


def mem_kv_stacked(x, nw, w_k, w_v, layer, prev_k, prev_v, *, tm, tn):
    m, k = x.shape
    n = w_k.shape[-1]
    extra = _carried(prev_k) + _carried(prev_v)
    w_spec = pl.BlockSpec((None, k, tn), lambda i, j: (layer, 0, j))
    o_spec = pl.BlockSpec((None, tm, tn), lambda i, j: (layer, i, j))
    stacked = jax.ShapeDtypeStruct((DEPTH, m, n), F32)
    return pl.pallas_call(
        _skip_refs(functools.partial(_mem_kv_kernel, tm=tm), 4, len(extra)),
        out_shape=(stacked, stacked),
        grid=(m // tm, n // tn),
        in_specs=[
            pl.BlockSpec((tm, k), lambda i, j: (i, 0)),
            pl.BlockSpec((None, 1, k), lambda i, j: (layer, 0, 0)),
            w_spec, w_spec,
        ] + [_ANY] * len(extra),
        out_specs=(o_spec, o_spec),
        scratch_shapes=[pltpu.VMEM((tm, k), BF16)],
        input_output_aliases={4: 0, 5: 1} if extra else {},
        compiler_params=_cparams(("parallel", "arbitrary")),
        name="mem_kv",
    )(x, nw, w_k, w_v, *extra)


def _in_proj_kernel(*refs, src_ends):
    x_refs = refs[:len(src_ends)]
    nw_ref, w_ref, wu_ref, wd_ref, o_ref, wub_ref, wdb_ref = refs[len(src_ends):]
    h = _rms_rows(_read_stacked_rows(x_refs, src_ends), nw_ref[...]).astype(BF16)
    o_ref[...] = jnp.dot(h, w_ref[...], preferred_element_type=F32)
    wub_ref[...] = wu_ref[...].astype(BF16)
    wdb_ref[...] = wd_ref[...].astype(BF16)


def in_proj_and_mlp_weight_cast(x_sources, nw, w, w_up, w_down, layer, *, tm):
    x_specs, src_ends = _stacked_row_specs(x_sources, tm)
    m = src_ends[-1] * tm
    k, n = w.shape[1:]
    ff = w_up.shape[-1]
    slab = 256
    last = ff // slab - 1
    assert src_ends[-1] > last
    return pl.pallas_call(
        functools.partial(_in_proj_kernel, src_ends=src_ends),
        out_shape=(jax.ShapeDtypeStruct((m, n), F32),
                   jax.ShapeDtypeStruct((k, ff), BF16),
                   jax.ShapeDtypeStruct((ff, k), BF16)),
        grid=(src_ends[-1],),
        in_specs=x_specs + [
            pl.BlockSpec((None, 1, k), lambda i: (layer, 0, 0)),
            pl.BlockSpec((None, k, n), lambda i: (layer, 0, 0), pipeline_mode=pl.Buffered(1)),
            pl.BlockSpec((None, k, slab), lambda i: (layer, 0, jnp.minimum(i, last))),
            pl.BlockSpec((None, slab, k), lambda i: (layer, jnp.minimum(i, last), 0)),
        ],
        out_specs=(
            pl.BlockSpec((tm, n), lambda i: (i, 0)),
            pl.BlockSpec((k, slab), lambda i: (0, jnp.minimum(i, last))),
            pl.BlockSpec((slab, k), lambda i: (jnp.minimum(i, last), 0)),
        ),
        compiler_params=_cparams(("arbitrary",)),
        name="in_proj",
    )(*x_sources, nw, w, w_up, w_down)


def _resident_weight_spec(w, layer):
    return pl.BlockSpec((None,) + w.shape[1:], lambda i: (layer, 0, 0), pipeline_mode=pl.Buffered(1))


def _matmul_res_resident_kernel(*refs, n_parts, res_ends):
    a_refs = refs[:n_parts]
    w_ref = refs[n_parts]
    r_refs = refs[n_parts + 1:n_parts + 1 + len(res_ends)]
    nw_ref, o_ref, h_ref, wb_ref = refs[n_parts + 1 + len(res_ends):]
    _cast_weight_once(w_ref, wb_ref)
    half = o_ref.shape[0] // 2
    for rows in (slice(0, half), slice(half, 2 * half)):
        acc = _read_stacked_rows(r_refs, res_ends, rows)
        k0 = 0
        for a_ref in a_refs:
            k1 = k0 + a_ref.shape[1]
            acc = acc + jnp.dot(a_ref[rows, :], wb_ref[k0:k1, :], preferred_element_type=F32)
            k0 = k1
        o_ref[rows, :] = acc
        h_ref[rows, :] = _rms_rows(acc, nw_ref[...]).astype(BF16)


def matmul_residual_resident(a_parts, w, res_sources, nw, layer, *, tm, n_tiles=None):
    res_specs, res_ends = _stacked_row_specs(res_sources, tm)
    m, n = (n_tiles or res_ends[-1]) * tm, w.shape[-1]
    k = sum(a.shape[1] for a in a_parts)
    row_tile = lambda width: pl.BlockSpec((tm, width), lambda i: (i, 0))
    return pl.pallas_call(
        functools.partial(_matmul_res_resident_kernel, n_parts=len(a_parts), res_ends=res_ends),
        out_shape=(jax.ShapeDtypeStruct((m, n), F32), jax.ShapeDtypeStruct((m, n), BF16)),
        grid=(m // tm,),
        in_specs=[row_tile(a.shape[1]) for a in a_parts] + [_resident_weight_spec(w, layer)]
        + res_specs + [pl.BlockSpec((None, 1, n), lambda i: (layer, 0, 0))],
        out_specs=(row_tile(n), row_tile(n)),
        scratch_shapes=[pltpu.VMEM((k, n), BF16)],
        compiler_params=_cparams(("arbitrary",)),
        name="matmul_residual_resident",
    )(*a_parts, w, *res_sources, nw)


MLP_TF = 1024
MLP_TN = 256
MLP_UP_STEPS = D_FF // MLP_TF
MLP_DOWN_STEPS = D_MODEL // MLP_TN
MLP_STEPS = MLP_UP_STEPS + MLP_DOWN_STEPS


def _mlp_kernel(h_ref, x_ref, wu_ref, wd_ref, *refs):
    if len(refs) == 2:
        o_ref, hid_ref = refs
        attention_job = None
    else:
        q_ref, k_ref, v_ref, o_ref, att_ref, hid_ref = refs
        attention_job = (q_ref, k_ref, v_ref, att_ref)
    s = pl.program_id(1)

    @pl.when(s < MLP_UP_STEPS)
    def _():
        hid = jnp.dot(h_ref[...], wu_ref[...], preferred_element_type=F32)
        c0 = pl.multiple_of(s * MLP_TF, MLP_TF)
        hid_ref[:, pl.ds(c0, MLP_TF)] = jnp.square(jnp.maximum(hid, 0.0)).astype(BF16)
        if attention_job is not None:
            _sample_attention(*attention_job)

    @pl.when(s >= MLP_UP_STEPS)
    def _():
        o_ref[...] = x_ref[...] + jnp.dot(hid_ref[...], wd_ref[...], preferred_element_type=F32)
        if attention_job is not None:
            _sample_attention(*attention_job)


def mlp_residual(h, x, w_up, w_down, *, tm, sample_attention=None):
    m, k = x.shape
    up = lambda s: jnp.minimum(s, MLP_UP_STEPS - 1)
    down = lambda s: jnp.maximum(s - MLP_UP_STEPS, 0)
    out_tile = pl.BlockSpec((tm, MLP_TN), lambda i, s: (i, down(s)))
    in_specs = [
        pl.BlockSpec((tm, k), lambda i, s: (i, 0)),
        out_tile,
        pl.BlockSpec((k, MLP_TF), lambda i, s: (0, up(s))),
        pl.BlockSpec((D_FF, MLP_TN), lambda i, s: (0, down(s))),
    ]
    out_shape = [jax.ShapeDtypeStruct((m, k), F32)]
    out_specs = [out_tile]
    operands = [h, x, w_up, w_down]
    if sample_attention is not None:
        qs, cache_k, cache_v, layer = sample_attention
        assert (m // tm) * MLP_STEPS == qs.shape[0]
        batch = lambda i, s: i * MLP_STEPS + s
        q_spec = pl.BlockSpec((None,) + qs.shape[1:], lambda i, s: (batch(i, s), 0, 0))
        kv_spec = pl.BlockSpec((None, None) + cache_k.shape[2:], lambda i, s: (layer, batch(i, s), 0, 0, 0))
        in_specs += [q_spec, kv_spec, kv_spec]
        out_shape += [jax.ShapeDtypeStruct(qs.shape, BF16)]
        out_specs += [q_spec]
        operands += [qs, cache_k, cache_v]
    out = pl.pallas_call(
        _mlp_kernel,
        out_shape=tuple(out_shape),
        grid=(m // tm, MLP_STEPS),
        in_specs=in_specs,
        out_specs=tuple(out_specs),
        scratch_shapes=[pltpu.VMEM((tm, D_FF), BF16)],
        compiler_params=_cparams(("arbitrary", "arbitrary")),
        name="mlp_residual",
    )(*operands)
    return out[0] if sample_attention is None else out


def _rotate(x, cos, sin):
    half = x.shape[-1] // 2
    x1, x2 = x[:, :half], x[:, half:]
    return jnp.concatenate([x1 * cos - x2 * sin, x2 * cos + x1 * sin], axis=-1)


def _head_norm_gate(o, g, rnw):
    o = o * lax.rsqrt(jnp.mean(o * o, axis=-1, keepdims=True) + EPS)
    return (g * jax.nn.sigmoid(g)) * (o * rnw)


def _dot_t_lhs(a, b):
    return lax.dot_general(a, b, (((0,), (0,)), ((), ())), preferred_element_type=F32)


def _dot_t_rhs(a, b):
    return lax.dot_general(a, b, (((1,), (1,)), ((), ())), preferred_element_type=F32)


MIX_P_ROWS = 512


def _ret_prompt_compute(q_ref, k_ref, v_ref, g_ref, cos_ref, sin_ref, din_ref, dq_ref, dk_ref, dc_ref,
                        rnw_ref, o_ref, s_ref):
    for c in range(MIX_P_ROWS // RET_CHUNK):
        rows = slice(c * RET_CHUNK, (c + 1) * RET_CHUNK)
        cos, sin = cos_ref[rows, :], sin_ref[rows, :]
        for h in range(RET_HEADS):
            cols = slice(h * RET_HEAD_DIM, (h + 1) * RET_HEAD_DIM)
            dq = jnp.concatenate([dq_ref[h], dq_ref[h]], axis=-1)
            dk = jnp.concatenate([dk_ref[h], dk_ref[h]], axis=-1)
            q = _rotate(q_ref[rows, cols], cos, sin)
            k = _rotate(k_ref[rows, cols], cos, sin) * (RET_HEAD_DIM ** -0.5)
            vb = v_ref[rows, cols].astype(BF16)
            s = s_ref[h]
            scores = _dot_t_rhs(q.astype(BF16), k.astype(BF16)) * din_ref[h]
            o = jnp.dot(scores.astype(BF16), vb, preferred_element_type=F32)
            o = o + jnp.dot((q * dq).astype(BF16), s.astype(BF16), preferred_element_type=F32)
            s_ref[h] = s * dc_ref[h] + _dot_t_lhs((k * dk).astype(BF16), vb)
            o_ref[rows, cols] = _head_norm_gate(o, g_ref[rows, cols], rnw_ref[:, cols]).astype(o_ref.dtype)


def _ret_sample_head(h, dec_ref, z_ref, cos, sin, st_ref, rnw_ref, o_ref, st_out_ref):
    cols = slice(h * RET_HEAD_DIM, (h + 1) * RET_HEAD_DIM)
    part = lambda p: z_ref[:, p * D_RET + h * RET_HEAD_DIM:p * D_RET + (h + 1) * RET_HEAD_DIM]
    q = _rotate(part(0), cos, sin)
    k = _rotate(part(1), cos, sin) * (RET_HEAD_DIM ** -0.5)
    v = part(2)
    g = part(3)
    slab = lambda a, t: a[t * S_TILE_B:(t + 1) * S_TILE_B, :]

    intra = []
    for t in range(DEC_SEQ):
        acc = None
        for j in range(t + 1):
            w = jnp.sum(slab(q, t) * slab(k, j), axis=-1, keepdims=True) * dec_ref[h, t * DEC_SEQ + j]
            acc = w * slab(v, j) if acc is None else acc + w * slab(v, j)
        intra.append(acc)
    o = jnp.concatenate(intra, axis=0)

    dq_rows = jnp.concatenate(
        [jnp.full((S_TILE_B, 1), 1.0, F32) * dec_ref[h, 16 + t] for t in range(DEC_SEQ)], axis=0)
    dk_rows = jnp.concatenate(
        [jnp.full((S_TILE_B, 1), 1.0, F32) * dec_ref[h, 20 + t] for t in range(DEC_SEQ)], axis=0)
    qd = (q * dq_rows).astype(BF16)
    kd = k * dk_rows
    vb = v.astype(BF16)
    dchunk = dec_ref[h, 24]
    row_b = lax.broadcasted_iota(jnp.int32, (S_TILE_ROWS, 1), 0) % S_TILE_B
    for b in range(S_TILE_B):
        mine = row_b == b
        s = st_ref[b, h]
        o = o + jnp.where(mine, jnp.dot(qd, s.astype(BF16), preferred_element_type=F32), 0.0)
        st_out_ref[b, h] = s * dchunk + _dot_t_lhs(jnp.where(mine, kd, 0.0).astype(BF16), vb)
    o_ref[:, cols] = _head_norm_gate(o, g, rnw_ref[:, cols]).astype(o_ref.dtype)


def _ret_sample_kernel(dec_ref, z_ref, cos_ref, sin_ref, st_ref, rnw_ref, o_ref, st_out_ref):
    cos, sin = cos_ref[...], sin_ref[...]
    for h in range(RET_HEADS):
        _ret_sample_head(h, dec_ref, z_ref, cos, sin, st_ref, rnw_ref, o_ref, st_out_ref)


def retention_sample(dec, z, cos_s, sin_s, state_ret, rnw, mix, layer, state_prev):
    extra = [mix] + _carried(state_prev)
    state_spec = pl.BlockSpec((None, S_TILE_B, RET_HEADS, RET_HEAD_DIM, RET_HEAD_DIM),
                              lambda i: (layer, i, 0, 0, 0))
    return pl.pallas_call(
        _skip_refs(_ret_sample_kernel, 6, len(extra)),
        out_shape=(jax.ShapeDtypeStruct(mix.shape, mix.dtype),
                   jax.ShapeDtypeStruct(state_ret.shape, F32)),
        grid=(N_S_TILES,),
        in_specs=[
            pl.BlockSpec(memory_space=pltpu.SMEM),
            pl.BlockSpec((S_TILE_ROWS, D_IN), lambda i: (S_BLK0 + i, 0)),
            pl.BlockSpec((S_TILE_ROWS, RET_HEAD_DIM // 2), lambda i: (0, 0)),
            pl.BlockSpec((S_TILE_ROWS, RET_HEAD_DIM // 2), lambda i: (0, 0)),
            state_spec,
            pl.BlockSpec((None, 1, D_RET), lambda i: (layer, 0, 0)),
        ] + [_ANY] * len(extra),
        out_specs=(pl.BlockSpec((S_TILE_ROWS, D_RET), lambda i: (S_BLK0 + i, 0)), state_spec),
        input_output_aliases={6 + n: n for n in range(len(extra))},
        compiler_params=_cparams(("arbitrary",)),
        name="retention_sample",
    )(dec, z, cos_s, sin_s, state_ret, rnw, *extra)


POOL_HALO = 16


def _shift_rows(a, s):
    return pltpu.roll(a, s, axis=0)


def _pool_prompt_compute(u_ref, pw_ref, ps_ref, o_ref, halo_ref):
    pos = pl.program_id(1) * MIX_P_ROWS + lax.broadcasted_iota(jnp.int32, (MIX_P_ROWS, 1), 0)
    for g, w in enumerate(POOL_WINDOWS):
        cols = slice(g * POOL_GROUP_DIM, (g + 1) * POOL_GROUP_DIM)
        u = u_ref[:, cols]
        a = jnp.concatenate([halo_ref[:, cols], u], axis=0)
        s = 1
        while s < w:
            a = a + _shift_rows(a, s)
            s *= 2
        cnt = jnp.minimum(pos + 1, w).astype(F32)
        pooled = a[POOL_HALO:, :] / cnt - u
        pm = jnp.dot(pooled.astype(BF16), pw_ref[g].astype(BF16), preferred_element_type=F32)
        o_ref[:, cols] = (pm * ps_ref[:, cols]).astype(o_ref.dtype)
    halo_ref[...] = u_ref[MIX_P_ROWS - POOL_HALO:, :]


def _mix_prompt_kernel(q_ref, k_ref, v_ref, g_ref, u_ref, cos_ref, sin_ref, din_ref, dq_ref, dk_ref,
                       dc_ref, rnw_ref, pw_ref, ps_ref, ret_ref, s_out_ref, pool_ref, buf_ref,
                       s_ref, halo_ref):
    t = pl.program_id(1)

    @pl.when(t == 0)
    def _():
        s_ref[...] = jnp.zeros_like(s_ref)
        halo_ref[...] = jnp.zeros_like(halo_ref)

    _ret_prompt_compute(q_ref, k_ref, v_ref, g_ref, cos_ref, sin_ref, din_ref, dq_ref, dk_ref, dc_ref,
                        rnw_ref, ret_ref, s_ref)
    _pool_prompt_compute(u_ref, pw_ref, ps_ref, pool_ref, halo_ref)

    @pl.when(t == pl.num_programs(1) - 1)
    def _():
        s_out_ref[...] = s_ref[...]
        buf_ref[...] = u_ref[MIX_P_ROWS - POOL_BUF:, :]


def mix_prompt(z, cos, sin, din, dq, dk, dc, rnw, pool_w, pool_scale, layer, state_prev, buf_prev):
    nt = SEQ // MIX_P_ROWS
    extra = _carried(state_prev) + _carried(buf_prev)
    zspec = lambda part: pl.BlockSpec((MIX_P_ROWS, D_RET), lambda b, t: (b * nt + t, part))
    whole = lambda a: pl.BlockSpec(a.shape, lambda b, t: (0,) * a.ndim)
    rope = pl.BlockSpec((MIX_P_ROWS, RET_HEAD_DIM // 2), lambda b, t: (t, 0))
    state_block = (RET_HEADS, RET_HEAD_DIM, RET_HEAD_DIM)
    mix_spec = pl.BlockSpec((MIX_P_ROWS, D_RET), lambda b, t: (b * nt + t, 0))
    n_in = 14
    return pl.pallas_call(
        _skip_refs(_mix_prompt_kernel, n_in, len(extra)),
        out_shape=(jax.ShapeDtypeStruct((M_ROWS, D_RET), BF16),
                   jax.ShapeDtypeStruct((DEPTH, BATCH) + state_block, F32),
                   jax.ShapeDtypeStruct((M_ROWS, D_POOL), BF16),
                   jax.ShapeDtypeStruct((DEPTH, BATCH, POOL_BUF, D_POOL), F32)),
        grid=(BATCH, nt),
        in_specs=[
            zspec(0), zspec(1), zspec(2), zspec(3), zspec(4), rope, rope,
            whole(din), whole(dq), whole(dk), whole(dc),
            pl.BlockSpec((None, 1, D_RET), lambda b, t: (layer, 0, 0)),
            pl.BlockSpec((None, 4, POOL_GROUP_DIM, POOL_GROUP_DIM), lambda b, t: (layer, 0, 0, 0)),
            pl.BlockSpec((None, 1, D_POOL), lambda b, t: (layer, 0, 0)),
        ] + [_ANY] * len(extra),
        out_specs=(
            mix_spec,
            pl.BlockSpec((None, None) + state_block, lambda b, t: (layer, b, 0, 0, 0)),
            mix_spec,
            pl.BlockSpec((None, None, POOL_BUF, D_POOL), lambda b, t: (layer, b, 0, 0)),
        ),
        scratch_shapes=[pltpu.VMEM(state_block, F32), pltpu.VMEM((POOL_HALO, D_POOL), F32)],
        input_output_aliases={n_in: 1, n_in + 1: 3} if extra else {},
        compiler_params=_cparams(("parallel", "arbitrary")),
        name="mix_prompt",
    )(z, z, z, z, z, cos, sin, din, dq, dk, dc, rnw, pool_w, pool_scale, *extra)


def _pool_sample_kernel(u_ref, buf_ref, pw_ref, ps_ref, o_ref, nbuf_ref):

    def ext(r, cols):
        if r < POOL_BUF:
            return buf_ref[:, r, cols]
        return u_ref[(r - POOL_BUF) * S_TILE_B:(r - POOL_BUF + 1) * S_TILE_B, cols]

    for g, w in enumerate(POOL_WINDOWS):
        cols = slice(g * POOL_GROUP_DIM, (g + 1) * POOL_GROUP_DIM)
        pooled = []
        for t in range(DEC_SEQ):
            win = ext(POOL_BUF + t, cols)
            for r in range(POOL_BUF + t - w + 1, POOL_BUF + t):
                win = win + ext(r, cols)
            cnt = float(min(t + PAST_LEN + 1, w))
            pooled.append(win / cnt - ext(POOL_BUF + t, cols))
        pooled = jnp.concatenate(pooled, axis=0)
        pm = jnp.dot(pooled.astype(BF16), pw_ref[g].astype(BF16), preferred_element_type=F32)
        o_ref[:, cols] = (pm * ps_ref[:, cols]).astype(o_ref.dtype)

    full = slice(0, D_POOL)
    for r in range(POOL_BUF):
        nbuf_ref[:, r, :] = ext(r + DEC_SEQ, full)


def pool_sample(z, state_pool, pool_w, pool_scale, mix, layer, buf_prev):
    extra = [mix] + _carried(buf_prev)
    return pl.pallas_call(
        _skip_refs(_pool_sample_kernel, 4, len(extra)),
        out_shape=(jax.ShapeDtypeStruct(mix.shape, mix.dtype),
                   jax.ShapeDtypeStruct(state_pool.shape, F32)),
        grid=(N_S_TILES,),
        in_specs=[
            pl.BlockSpec((S_TILE_ROWS, D_POOL), lambda i: (S_BLK0 + i, 4)),
            pl.BlockSpec((None, S_TILE_B, POOL_BUF, D_POOL), lambda i: (layer, i, 0, 0)),
            pl.BlockSpec((None, 4, POOL_GROUP_DIM, POOL_GROUP_DIM), lambda i: (layer, 0, 0, 0)),
            pl.BlockSpec((None, 1, D_POOL), lambda i: (layer, 0, 0)),
        ] + [_ANY] * len(extra),
        out_specs=(
            pl.BlockSpec((S_TILE_ROWS, D_POOL), lambda i: (S_BLK0 + i, 0)),
            pl.BlockSpec((None, S_TILE_B, POOL_BUF, D_POOL), lambda i: (layer, i, 0, 0)),
        ),
        input_output_aliases={4 + n: n for n in range(len(extra))},
        compiler_params=_cparams(("arbitrary",)),
        name="pool_sample",
    )(z, state_pool, pool_w, pool_scale, *extra)


def _softmax_rows(s):
    m = jnp.max(s, axis=-1, keepdims=True)
    e = jnp.exp(s - m)
    return e / jnp.sum(e, axis=-1, keepdims=True)


def _xq_prompt_attention_kernel(h_ref, w_ref, mk_ref, mv_ref, o_ref, wb_ref, *, prompt_tiles):
    _cast_weight_once(w_ref, wb_ref)

    @pl.when(pl.program_id(0) < prompt_tiles)
    def _():
        half = o_ref.shape[0] // 2
        for rows in (slice(0, half), slice(half, 2 * half)):
            q = jnp.dot(h_ref[rows, :], wb_ref[...], preferred_element_type=F32).astype(BF16)
            for hd in range(MEM_HEADS):
                cols = slice(hd * MEM_HEAD_DIM, (hd + 1) * MEM_HEAD_DIM)
                s = _dot_t_rhs(q[:, cols], mk_ref[:, cols].astype(BF16)) * (MEM_HEAD_DIM ** -0.5)
                p = _softmax_rows(s)
                o_ref[rows, cols] = jnp.dot(p.astype(BF16), mv_ref[:, cols].astype(BF16),
                                            preferred_element_type=F32).astype(o_ref.dtype)

    @pl.when(pl.program_id(0) >= prompt_tiles)
    def _():
        o_ref[...] = jnp.dot(h_ref[...], wb_ref[...], preferred_element_type=F32).astype(o_ref.dtype)


def xq_and_prompt_attention(h, w, mk, mv, layer, *, tm):
    m, k = h.shape
    tiles_per_batch = SEQ // tm
    prompt_tiles = P_ROWS // tm
    kv_spec = pl.BlockSpec((None, N_MEM, k),
                           lambda i: (layer, jnp.minimum(i // tiles_per_batch, BATCH - 1), 0))
    return pl.pallas_call(
        functools.partial(_xq_prompt_attention_kernel, prompt_tiles=prompt_tiles),
        out_shape=jax.ShapeDtypeStruct((m, k), BF16),
        grid=(m // tm,),
        in_specs=[pl.BlockSpec((tm, k), lambda i: (i, 0)), _resident_weight_spec(w, layer),
                  kv_spec, kv_spec],
        out_specs=pl.BlockSpec((tm, k), lambda i: (i, 0)),
        scratch_shapes=[pltpu.VMEM((k, k), BF16)],
        compiler_params=_cparams(("arbitrary",)),
        name="xq_prompt_attention",
    )(h, w, mk, mv)


XATTN_S_ROWS = MEM_HEADS * DEC_SEQ
XATTN_S_KEYS = N_MEM * MEM_HEADS


def _sample_attention(q_ref, k_ref, v_ref, o_ref):
    row_h = lax.broadcasted_iota(jnp.int32, (XATTN_S_ROWS, XATTN_S_KEYS), 0) // DEC_SEQ
    col_h = lax.broadcasted_iota(jnp.int32, (XATTN_S_ROWS, XATTN_S_KEYS), 1) % MEM_HEADS
    k2 = k_ref[...].reshape(XATTN_S_KEYS, MEM_HEAD_DIM).astype(BF16)
    v2 = v_ref[...].reshape(XATTN_S_KEYS, MEM_HEAD_DIM).astype(BF16)
    s = _dot_t_rhs(q_ref[...], k2) * (MEM_HEAD_DIM ** -0.5)
    s = jnp.where(row_h == col_h, s, -jnp.inf)
    p = _softmax_rows(s)
    o_ref[...] = jnp.dot(p.astype(BF16), v2, preferred_element_type=F32).astype(o_ref.dtype)


def _sample_rows_to_heads(a):
    a = a.reshape(N_S_TILES, DEC_SEQ, S_TILE_B, MEM_HEADS, MEM_HEAD_DIM)
    return a.transpose(0, 2, 3, 1, 4).reshape(DEC_BATCH, XATTN_S_ROWS, MEM_HEAD_DIM)


def _heads_to_sample_rows(a):
    a = a.reshape(N_S_TILES, S_TILE_B, MEM_HEADS, DEC_SEQ, MEM_HEAD_DIM)
    return a.transpose(0, 3, 1, 2, 4).reshape(S_ROWS, D_MODEL)


FINAL_ROWS = 512


def _final_norm_kernel(x_ref, nw_ref, y_ref):
    def body(rows):
        y_ref[rows, :] = _rms_rows(x_ref[rows, :], nw_ref[...])
    _for_row_chunks(FINAL_ROWS, _norm_chunk(FINAL_ROWS), body)


def final_norm(x, nw):
    m, k = x.shape
    tile = pl.BlockSpec((FINAL_ROWS, k), lambda i: (i, 0))
    return pl.pallas_call(
        _final_norm_kernel,
        out_shape=jax.ShapeDtypeStruct((m, k), F32),
        grid=(m // FINAL_ROWS,),
        in_specs=[tile, pl.BlockSpec((1, k), lambda i: (0, 0))],
        out_specs=tile,
        compiler_params=_cparams(("parallel",)),
        name="final_norm",
    )(x, nw)


def _rope_tables(pos):
    half = RET_HEAD_DIM // 2
    inv = ROPE_BASE ** (-jnp.arange(half, dtype=F32) / half)
    ang = pos.astype(F32)[:, None] * inv[None, :]
    return jnp.cos(ang), jnp.sin(ang)


def _log_gamma():
    return jnp.log1p(-jnp.exp2(-5.0 - jnp.arange(RET_HEADS, dtype=F32)))


def _decay_tables(chunk):
    lg = _log_gamma()
    idx = jnp.arange(chunk, dtype=F32)
    diff = idx[:, None] - idx[None, :]
    decay_in = jnp.where(diff[None] >= 0.0,
                         jnp.exp(lg[:, None, None] * jnp.maximum(diff, 0.0)[None]), 0.0)
    decay_q = jnp.exp(lg[:, None] * (idx[None, :] + 1.0))
    decay_k = jnp.exp(lg[:, None] * (chunk - 1.0 - idx[None, :]))
    decay_chunk = jnp.exp(lg * chunk)
    return decay_in, decay_q, decay_k, decay_chunk


def _to_sample_rows(a):
    d = a.shape[-1]
    return a.reshape(N_S_TILES, S_TILE_B, DEC_SEQ, d).transpose(0, 2, 1, 3).reshape(S_ROWS, d)


def _from_sample_rows(a):
    d = a.shape[-1]
    return a.reshape(N_S_TILES, DEC_SEQ, S_TILE_B, d).transpose(0, 2, 1, 3).reshape(DEC_BATCH, DEC_SEQ, d)


def kernel(x_prompt, x_sample, mem_prompt, state_ret, state_pool, cache_mem_k, cache_mem_v,
           attn_norm_w, w_in, ret_norm_w, pool_w, pool_scale, w_out, xattn_norm_w, mem_norm_w,
           w_xq, w_mk, w_mv, w_xo, mlp_norm_w, w_up, w_down, final_norm_w):
    cos_p, sin_p = _rope_tables(jnp.arange(SEQ))
    cos_s, sin_s = _rope_tables(jnp.arange(DEC_SEQ) + PAST_LEN)
    cos_s = jnp.repeat(cos_s, S_TILE_B, axis=0)
    sin_s = jnp.repeat(sin_s, S_TILE_B, axis=0)
    din_p, dq_p, dk_p, dc_p = _decay_tables(RET_CHUNK)
    half = RET_HEAD_DIM // 2
    dq_p = jnp.broadcast_to(dq_p[:, :, None], (RET_HEADS, RET_CHUNK, half))
    dk_p = jnp.broadcast_to(dk_p[:, :, None], (RET_HEADS, RET_CHUNK, half))
    dc_p = jnp.broadcast_to(dc_p[:, None, None], (RET_HEADS, 1, RET_HEAD_DIM))
    din_s, dq_s, dk_s, dc_s = _decay_tables(DEC_SEQ)
    dec_s = jnp.concatenate([din_s.reshape(RET_HEADS, DEC_SEQ * DEC_SEQ), dq_s, dk_s,
                             dc_s[:, None], jnp.zeros((RET_HEADS, 7), F32)], axis=1)

    row3 = lambda a: a.reshape(DEPTH, 1, a.shape[-1])
    attn_nw, xattn_nw, mem_nw, mlp_nw = map(row3, (attn_norm_w, xattn_norm_w, mem_norm_w, mlp_norm_w))
    ret_nw, pool_sc = row3(ret_norm_w), row3(pool_scale)
    w_in_b = w_in.astype(BF16)

    x_sources = [x_prompt.reshape(P_ROWS, D_MODEL), _to_sample_rows(x_sample)]
    mem = mem_prompt.reshape(BATCH * N_MEM, D_MODEL)

    ret_p = buf_p = mk_p = mv_p = ret_s = buf_s = None
    for l in range(DEPTH):
        z, w_up_b, w_down_b = in_proj_and_mlp_weight_cast(x_sources, attn_nw, w_in_b, w_up, w_down, l,
                                                          tm=TM_IN_PROJ)
        mix_r, ret_p, mix_p, buf_p = mix_prompt(z, cos_p, sin_p, din_p, dq_p, dk_p, dc_p, ret_nw,
                                                pool_w, pool_sc, l, ret_p, buf_p)
        mix_r, ret_s = retention_sample(dec_s, z, cos_s, sin_s, state_ret, ret_nw, mix_r, l, ret_s)
        mix_p, buf_s = pool_sample(z, state_pool, pool_w, pool_sc, mix_p, l, buf_s)
        x, h = matmul_residual_resident([mix_r, mix_p], w_out, x_sources, xattn_nw, l, tm=TM_RESIDENT)

        mk_p, mv_p = mem_kv_stacked(mem, mem_nw, w_mk, w_mv, l, mk_p, mv_p, tm=1024, tn=512)
        att = xq_and_prompt_attention(h, w_xq, mk_p, mv_p, l, tm=TM_RESIDENT)
        xp, hp = matmul_residual_resident([att], w_xo, [x], mlp_nw, l, tm=TM_RESIDENT,
                                          n_tiles=P_ROWS // TM_RESIDENT)
        qs = _sample_rows_to_heads(att[P_ROWS:])
        xp, att_s = mlp_residual(hp, xp, w_up_b, w_down_b, tm=TM_MLP,
                                 sample_attention=(qs, cache_mem_k, cache_mem_v, l))
        xs, hs = matmul_residual_resident([_heads_to_sample_rows(att_s)], w_xo, [x[P_ROWS:]], mlp_nw, l,
                                          tm=TM_RESIDENT)
        xs = mlp_residual(hs, xs, w_up_b, w_down_b, tm=S_ROWS)
        x_sources = [xp, xs]

    nw = final_norm_w.reshape(1, D_MODEL)
    y_p, y_s = final_norm(xp, nw), final_norm(xs, nw)
    mem_shape = (DEPTH, BATCH, N_MEM, MEM_HEADS, MEM_HEAD_DIM)
    return (y_p.reshape(BATCH, SEQ, D_MODEL), _from_sample_rows(y_s),
            ret_p, buf_p, mk_p.reshape(mem_shape), mv_p.reshape(mem_shape), ret_s, buf_s)
```

```python
import functools

import jax
import jax.numpy as jnp
from jax import lax
from jax.experimental import pallas as pl
from jax.experimental.pallas import tpu as pltpu

D_MODEL = 2048
BATCH = 4
SEQ = 2048
DEPTH = 2
DEC_BATCH = 128
DEC_SEQ = 4
PAST_LEN = 16384
D_RET = 1024
RET_HEADS = 4
RET_HEAD_DIM = 256
D_POOL = 1024
POOL_WINDOWS = (2, 4, 8, 16)
POOL_GROUP_DIM = 256
POOL_BUF = 15
D_IN = 5120
N_MEM = 256
MEM_HEADS = 4
MEM_HEAD_DIM = 512
D_FF = 8192
RET_CHUNK = 128
ROPE_BASE = 10000.0
EPS = 1e-6

F32 = jnp.float32
BF16 = jnp.bfloat16

P_ROWS = BATCH * SEQ
S_ROWS = DEC_BATCH * DEC_SEQ
M_ROWS = P_ROWS + S_ROWS
S_TILE_B = 8
S_TILE_ROWS = S_TILE_B * DEC_SEQ
N_S_TILES = DEC_BATCH // S_TILE_B
S_BLK0 = P_ROWS // S_TILE_ROWS

TM_MLP = 1024
TM_RESIDENT = 512
TM_IN_PROJ = 256
VMEM_LIMIT = 58 * 1024 * 1024


def _cparams(sem):
    return pltpu.CompilerParams(dimension_semantics=sem, vmem_limit_bytes=VMEM_LIMIT)


_ANY = pl.BlockSpec(memory_space=pl.ANY)


def _skip_refs(kernel_fn, start, count):
    def wrapped(*refs):
        return kernel_fn(*refs[:start], *refs[start + count:])
    return wrapped


def _carried(prev):
    return [] if prev is None else [prev]


def _stacked_row_specs(sources, tm):
    specs, ends, start = [], [], 0
    for a in sources:
        n = a.shape[0] // tm
        specs.append(pl.BlockSpec((tm, a.shape[1]),
                                  lambda i, start=start, n=n: (jnp.clip(i - start, 0, n - 1), 0)))
        start += n
        ends.append(start)
    return specs, tuple(ends)


def _read_stacked_rows(refs, ends, rows=slice(None)):
    i = pl.program_id(0)
    x = refs[-1][rows, :]
    for ref, end in zip(reversed(refs[:-1]), reversed(ends[:-1])):
        x = jnp.where(i < end, ref[rows, :], x)
    return x


def _rms_rows(x, nw):
    ms = jnp.mean(x * x, axis=-1, keepdims=True)
    return x * lax.rsqrt(ms + EPS) * nw


def _for_row_chunks(n_rows, chunk, body):
    def step(c, carry):
        body(pl.ds(pl.multiple_of(c * chunk, chunk), chunk))
        return carry
    lax.fori_loop(0, n_rows // chunk, step, 0)


def _norm_chunk(n_rows):
    return 272 if n_rows % 272 == 0 else 128


def _cast_weight_once(w_ref, wb_ref):
    @pl.when(pl.program_id(0) == 0)
    def _():
        def body(rows):
            wb_ref[rows, :] = w_ref[rows, :].astype(BF16)
        _for_row_chunks(w_ref.shape[0], 256, body)


def _mem_kv_kernel(x_ref, nw_ref, wk_ref, wv_ref, k_ref, v_ref, h_ref, *, tm):
    @pl.when(pl.program_id(1) == 0)
    def _():
        def body(rows):
            h_ref[rows, :] = _rms_rows(x_ref[rows, :], nw_ref[...]).astype(BF16)
        _for_row_chunks(tm, _norm_chunk(tm), body)

    k_ref[...] = jnp.dot(h_ref[...], wk_ref[...].astype(BF16), preferred_element_type=F32)
    v_ref[...] = jnp.dot(h_ref[...], wv_ref[...].astype(BF16), preferred_element_type=F32)


def mem_kv_stacked(x, nw, w_k, w_v, layer, prev_k, prev_v, *, tm, tn):
    m, k = x.shape
    n = w_k.shape[-1]
    extra = _carried(prev_k) + _carried(prev_v)
    w_spec = pl.BlockSpec((None, k, tn), lambda i, j: (layer, 0, j))
    o_spec = pl.BlockSpec((None, tm, tn), lambda i, j: (layer, i, j))
    stacked = jax.ShapeDtypeStruct((DEPTH, m, n), F32)
    return pl.pallas_call(
        _skip_refs(functools.partial(_mem_kv_kernel, tm=tm), 4, len(extra)),
        out_shape=(stacked, stacked),
        grid=(m // tm, n // tn),
        in_specs=[
            pl.BlockSpec((tm, k), lambda i, j: (i, 0)),
            pl.BlockSpec((None, 1, k), lambda i, j: (layer, 0, 0)),
            w_spec, w_spec,
        ] + [_ANY] * len(extra),
        out_specs=(o_spec, o_spec),
        scratch_shapes=[pltpu.VMEM((tm, k), BF16)],
        input_output_aliases={4: 0, 5: 1} if extra else {},
        compiler_params=_cparams(("parallel", "arbitrary")),
        name="mem_kv",
    )(x, nw, w_k, w_v, *extra)


def _in_proj_kernel(*refs, src_ends):
    x_refs = refs[:len(src_ends)]
    nw_ref, w_ref, wu_ref, wd_ref, o_ref, wub_ref, wdb_ref = refs[len(src_ends):]
    h = _rms_rows(_read_stacked_rows(x_refs, src_ends), nw_ref[...]).astype(BF16)
    o_ref[...] = jnp.dot(h, w_ref[...], preferred_element_type=F32)
    wub_ref[...] = wu_ref[...].astype(BF16)
    wdb_ref[...] = wd_ref[...].astype(BF16)


def in_proj_and_mlp_weight_cast(x_sources, nw, w, w_up, w_down, layer, *, tm):
    x_specs, src_ends = _stacked_row_specs(x_sources, tm)
    m = src_ends[-1] * tm
    k, n = w.shape[1:]
    ff = w_up.shape[-1]
    slab = 256
    last = ff // slab - 1
    assert src_ends[-1] > last
    return pl.pallas_call(
        functools.partial(_in_proj_kernel, src_ends=src_ends),
        out_shape=(jax.ShapeDtypeStruct((m, n), F32),
                   jax.ShapeDtypeStruct((k, ff), BF16),
                   jax.ShapeDtypeStruct((ff, k), BF16)),
        grid=(src_ends[-1],),
        in_specs=x_specs + [
            pl.BlockSpec((None, 1, k), lambda i: (layer, 0, 0)),
            pl.BlockSpec((None, k, n), lambda i: (layer, 0, 0), pipeline_mode=pl.Buffered(1)),
            pl.BlockSpec((None, k, slab), lambda i: (layer, 0, jnp.minimum(i, last))),
            pl.BlockSpec((None, slab, k), lambda i: (layer, jnp.minimum(i, last), 0)),
        ],
        out_specs=(
            pl.BlockSpec((tm, n), lambda i: (i, 0)),
            pl.BlockSpec((k, slab), lambda i: (0, jnp.minimum(i, last))),
            pl.BlockSpec((slab, k), lambda i: (jnp.minimum(i, last), 0)),
        ),
        compiler_params=_cparams(("arbitrary",)),
        name="in_proj",
    )(*x_sources, nw, w, w_up, w_down)


def _resident_weight_spec(w, layer):
    return pl.BlockSpec((None,) + w.shape[1:], lambda i: (layer, 0, 0), pipeline_mode=pl.Buffered(1))


def _matmul_res_resident_kernel(*refs, n_parts, res_ends):
    a_refs = refs[:n_parts]
    w_ref = refs[n_parts]
    r_refs = refs[n_parts + 1:n_parts + 1 + len(res_ends)]
    nw_ref, o_ref, h_ref, wb_ref = refs[n_parts + 1 + len(res_ends):]
    _cast_weight_once(w_ref, wb_ref)
    half = o_ref.shape[0] // 2
    for rows in (slice(0, half), slice(half, 2 * half)):
        acc = _read_stacked_rows(r_refs, res_ends, rows)
        k0 = 0
        for a_ref in a_refs:
            k1 = k0 + a_ref.shape[1]
            acc = acc + jnp.dot(a_ref[rows, :], wb_ref[k0:k1, :], preferred_element_type=F32)
            k0 = k1
        o_ref[rows, :] = acc
        h_ref[rows, :] = _rms_rows(acc, nw_ref[...]).astype(BF16)


def matmul_residual_resident(a_parts, w, res_sources, nw, layer, *, tm, n_tiles=None):
    res_specs, res_ends = _stacked_row_specs(res_sources, tm)
    m, n = (n_tiles or res_ends[-1]) * tm, w.shape[-1]
    k = sum(a.shape[1] for a in a_parts)
    row_tile = lambda width: pl.BlockSpec((tm, width), lambda i: (i, 0))
    return pl.pallas_call(
        functools.partial(_matmul_res_resident_kernel, n_parts=len(a_parts), res_ends=res_ends),
        out_shape=(jax.ShapeDtypeStruct((m, n), F32), jax.ShapeDtypeStruct((m, n), BF16)),
        grid=(m // tm,),
        in_specs=[row_tile(a.shape[1]) for a in a_parts] + [_resident_weight_spec(w, layer)]
        + res_specs + [pl.BlockSpec((None, 1, n), lambda i: (layer, 0, 0))],
        out_specs=(row_tile(n), row_tile(n)),
        scratch_shapes=[pltpu.VMEM((k, n), BF16)],
        compiler_params=_cparams(("arbitrary",)),
        name="matmul_residual_resident",
    )(*a_parts, w, *res_sources, nw)


MLP_TF = 1024
MLP_TN = 256
MLP_UP_STEPS = D_FF // MLP_TF
MLP_DOWN_STEPS = D_MODEL // MLP_TN
MLP_STEPS = MLP_UP_STEPS + MLP_DOWN_STEPS


def _mlp_kernel(h_ref, x_ref, wu_ref, wd_ref, *refs):
    if len(refs) == 2:
        o_ref, hid_ref = refs
        attention_job = None
    else:
        q_ref, k_ref, v_ref, o_ref, att_ref, hid_ref = refs
        attention_job = (q_ref, k_ref, v_ref, att_ref)
    s = pl.program_id(1)

    @pl.when(s < MLP_UP_STEPS)
    def _():
        hid = jnp.dot(h_ref[...], wu_ref[...], preferred_element_type=F32)
        c0 = pl.multiple_of(s * MLP_TF, MLP_TF)
        hid_ref[:, pl.ds(c0, MLP_TF)] = jnp.square(jnp.maximum(hid, 0.0)).astype(BF16)
        if attention_job is not None:
            _sample_attention(*attention_job)

    @pl.when(s >= MLP_UP_STEPS)
    def _():
        o_ref[...] = x_ref[...] + jnp.dot(hid_ref[...], wd_ref[...], preferred_element_type=F32)
        if attention_job is not None:
            _sample_attention(*attention_job)


def mlp_residual(h, x, w_up, w_down, *, tm, sample_attention=None):
    m, k = x.shape
    up = lambda s: jnp.minimum(s, MLP_UP_STEPS - 1)
    down = lambda s: jnp.maximum(s - MLP_UP_STEPS, 0)
    out_tile = pl.BlockSpec((tm, MLP_TN), lambda i, s: (i, down(s)))
    in_specs = [
        pl.BlockSpec((tm, k), lambda i, s: (i, 0)),
        out_tile,
        pl.BlockSpec((k, MLP_TF), lambda i, s: (0, up(s))),
        pl.BlockSpec((D_FF, MLP_TN), lambda i, s: (0, down(s))),
    ]
    out_shape = [jax.ShapeDtypeStruct((m, k), F32)]
    out_specs = [out_tile]
    operands = [h, x, w_up, w_down]
    if sample_attention is not None:
        qs, cache_k, cache_v, layer = sample_attention
        assert (m // tm) * MLP_STEPS == qs.shape[0]
        batch = lambda i, s: i * MLP_STEPS + s
        q_spec = pl.BlockSpec((None,) + qs.shape[1:], lambda i, s: (batch(i, s), 0, 0))
        kv_spec = pl.BlockSpec((None, None) + cache_k.shape[2:], lambda i, s: (layer, batch(i, s), 0, 0, 0))
        in_specs += [q_spec, kv_spec, kv_spec]
        out_shape += [jax.ShapeDtypeStruct(qs.shape, BF16)]
        out_specs += [q_spec]
        operands += [qs, cache_k, cache_v]
    out = pl.pallas_call(
        _mlp_kernel,
        out_shape=tuple(out_shape),
        grid=(m // tm, MLP_STEPS),
        in_specs=in_specs,
        out_specs=tuple(out_specs),
        scratch_shapes=[pltpu.VMEM((tm, D_FF), BF16)],
        compiler_params=_cparams(("arbitrary", "arbitrary")),
        name="mlp_residual",
    )(*operands)
    return out[0] if sample_attention is None else out


def _rotate(x, cos, sin):
    half = x.shape[-1] // 2
    x1, x2 = x[:, :half], x[:, half:]
    return jnp.concatenate([x1 * cos - x2 * sin, x2 * cos + x1 * sin], axis=-1)


def _head_norm_gate(o, g, rnw):
    o = o * lax.rsqrt(jnp.mean(o * o, axis=-1, keepdims=True) + EPS)
    return (g * jax.nn.sigmoid(g)) * (o * rnw)


def _dot_t_lhs(a, b):
    return lax.dot_general(a, b, (((0,), (0,)), ((), ())), preferred_element_type=F32)


def _dot_t_rhs(a, b):
    return lax.dot_general(a, b, (((1,), (1,)), ((), ())), preferred_element_type=F32)


MIX_P_ROWS = 512


def _ret_prompt_compute(q_ref, k_ref, v_ref, g_ref, cos_ref, sin_ref, din_ref, dq_ref, dk_ref, dc_ref,
                        rnw_ref, o_ref, s_ref):
    for c in range(MIX_P_ROWS // RET_CHUNK):
        rows = slice(c * RET_CHUNK, (c + 1) * RET_CHUNK)
        cos, sin = cos_ref[rows, :], sin_ref[rows, :]
        for h in range(RET_HEADS):
            cols = slice(h * RET_HEAD_DIM, (h + 1) * RET_HEAD_DIM)
            dq = jnp.concatenate([dq_ref[h], dq_ref[h]], axis=-1)
            dk = jnp.concatenate([dk_ref[h], dk_ref[h]], axis=-1)
            q = _rotate(q_ref[rows, cols], cos, sin)
            k = _rotate(k_ref[rows, cols], cos, sin) * (RET_HEAD_DIM ** -0.5)
            vb = v_ref[rows, cols].astype(BF16)
            s = s_ref[h]
            scores = _dot_t_rhs(q.astype(BF16), k.astype(BF16)) * din_ref[h]
            o = jnp.dot(scores.astype(BF16), vb, preferred_element_type=F32)
            o = o + jnp.dot((q * dq).astype(BF16), s.astype(BF16), preferred_element_type=F32)
            s_ref[h] = s * dc_ref[h] + _dot_t_lhs((k * dk).astype(BF16), vb)
            o_ref[rows, cols] = _head_norm_gate(o, g_ref[rows, cols], rnw_ref[:, cols]).astype(o_ref.dtype)


def _ret_sample_head(h, dec_ref, z_ref, cos, sin, st_ref, rnw_ref, o_ref, st_out_ref):
    cols = slice(h * RET_HEAD_DIM, (h + 1) * RET_HEAD_DIM)
    part = lambda p: z_ref[:, p * D_RET + h * RET_HEAD_DIM:p * D_RET + (h + 1) * RET_HEAD_DIM]
    q = _rotate(part(0), cos, sin)
    k = _rotate(part(1), cos, sin) * (RET_HEAD_DIM ** -0.5)
    v = part(2)
    g = part(3)
    slab = lambda a, t: a[t * S_TILE_B:(t + 1) * S_TILE_B, :]

    intra = []
    for t in range(DEC_SEQ):
        acc = None
        for j in range(t + 1):
            w = jnp.sum(slab(q, t) * slab(k, j), axis=-1, keepdims=True) * dec_ref[h, t * DEC_SEQ + j]
            acc = w * slab(v, j) if acc is None else acc + w * slab(v, j)
        intra.append(acc)
    o = jnp.concatenate(intra, axis=0)

    dq_rows = jnp.concatenate(
        [jnp.full((S_TILE_B, 1), 1.0, F32) * dec_ref[h, 16 + t] for t in range(DEC_SEQ)], axis=0)
    dk_rows = jnp.concatenate(
        [jnp.full((S_TILE_B, 1), 1.0, F32) * dec_ref[h, 20 + t] for t in range(DEC_SEQ)], axis=0)
    qd = (q * dq_rows).astype(BF16)
    kd = k * dk_rows
    vb = v.astype(BF16)
    dchunk = dec_ref[h, 24]
    row_b = lax.broadcasted_iota(jnp.int32, (S_TILE_ROWS, 1), 0) % S_TILE_B
    for b in range(S_TILE_B):
        mine = row_b == b
        s = st_ref[b, h]
        o = o + jnp.where(mine, jnp.dot(qd, s.astype(BF16), preferred_element_type=F32), 0.0)
        st_out_ref[b, h] = s * dchunk + _dot_t_lhs(jnp.where(mine, kd, 0.0).astype(BF16), vb)
    o_ref[:, cols] = _head_norm_gate(o, g, rnw_ref[:, cols]).astype(o_ref.dtype)


def _ret_sample_kernel(dec_ref, z_ref, cos_ref, sin_ref, st_ref, rnw_ref, o_ref, st_out_ref):
    cos, sin = cos_ref[...], sin_ref[...]
    for h in range(RET_HEADS):
        _ret_sample_head(h, dec_ref, z_ref, cos, sin, st_ref, rnw_ref, o_ref, st_out_ref)


def retention_sample(dec, z, cos_s, sin_s, state_ret, rnw, mix, layer, state_prev):
    extra = [mix] + _carried(state_prev)
    state_spec = pl.BlockSpec((None, S_TILE_B, RET_HEADS, RET_HEAD_DIM, RET_HEAD_DIM),
                              lambda i: (layer, i, 0, 0, 0))
    return pl.pallas_call(
        _skip_refs(_ret_sample_kernel, 6, len(extra)),
        out_shape=(jax.ShapeDtypeStruct(mix.shape, mix.dtype),
                   jax.ShapeDtypeStruct(state_ret.shape, F32)),
        grid=(N_S_TILES,),
        in_specs=[
            pl.BlockSpec(memory_space=pltpu.SMEM),
            pl.BlockSpec((S_TILE_ROWS, D_IN), lambda i: (S_BLK0 + i, 0)),
            pl.BlockSpec((S_TILE_ROWS, RET_HEAD_DIM // 2), lambda i: (0, 0)),
            pl.BlockSpec((S_TILE_ROWS, RET_HEAD_DIM // 2), lambda i: (0, 0)),
            state_spec,
            pl.BlockSpec((None, 1, D_RET), lambda i: (layer, 0, 0)),
        ] + [_ANY] * len(extra),
        out_specs=(pl.BlockSpec((S_TILE_ROWS, D_RET), lambda i: (S_BLK0 + i, 0)), state_spec),
        input_output_aliases={6 + n: n for n in range(len(extra))},
        compiler_params=_cparams(("arbitrary",)),
        name="retention_sample",
    )(dec, z, cos_s, sin_s, state_ret, rnw, *extra)


POOL_HALO = 16


def _shift_rows(a, s):
    return pltpu.roll(a, s, axis=0)


def _pool_prompt_compute(u_ref, pw_ref, ps_ref, o_ref, halo_ref):
    pos = pl.program_id(1) * MIX_P_ROWS + lax.broadcasted_iota(jnp.int32, (MIX_P_ROWS, 1), 0)
    for g, w in enumerate(POOL_WINDOWS):
        cols = slice(g * POOL_GROUP_DIM, (g + 1) * POOL_GROUP_DIM)
        u = u_ref[:, cols]
        a = jnp.concatenate([halo_ref[:, cols], u], axis=0)
        s = 1
        while s < w:
            a = a + _shift_rows(a, s)
            s *= 2
        cnt = jnp.minimum(pos + 1, w).astype(F32)
        pooled = a[POOL_HALO:, :] / cnt - u
        pm = jnp.dot(pooled.astype(BF16), pw_ref[g].astype(BF16), preferred_element_type=F32)
        o_ref[:, cols] = (pm * ps_ref[:, cols]).astype(o_ref.dtype)
    halo_ref[...] = u_ref[MIX_P_ROWS - POOL_HALO:, :]


MIX_PARTS = 5
MIX_SLOTS = 3


def _mix_tile_copy(z_hbm, zbuf, zsem, step, part):
    slot = step % MIX_SLOTS
    src = z_hbm.at[pl.ds(step * MIX_P_ROWS, MIX_P_ROWS), pl.ds(part * D_RET, D_RET)]
    return pltpu.make_async_copy(src, zbuf.at[slot, part], zsem.at[slot, part])


def _mix_prompt_kernel(z_hbm, cos_ref, sin_ref, din_ref, dq_ref, dk_ref, dc_ref, rnw_ref, pw_ref, ps_ref,
                       ret_ref, s_out_ref, pool_ref, buf_ref, s_ref, halo_ref, zbuf, zsem):
    t = pl.program_id(1)
    step = pl.program_id(0) * pl.num_programs(1) + t
    n_steps = pl.num_programs(0) * pl.num_programs(1)

    def start(tile):
        for part in range(MIX_PARTS):
            _mix_tile_copy(z_hbm, zbuf, zsem, tile, part).start()

    @pl.when(step == 0)
    def _():
        start(0)
        start(1)

    @pl.when(step + 2 < n_steps)
    def _():
        start(step + 2)

    for part in range(MIX_PARTS):
        _mix_tile_copy(z_hbm, zbuf, zsem, step, part).wait()

    @pl.when(t == 0)
    def _():
        s_ref[...] = jnp.zeros_like(s_ref)
        halo_ref[...] = jnp.zeros_like(halo_ref)

    slot = step % MIX_SLOTS
    q_ref, k_ref, v_ref, g_ref, u_ref = (zbuf.at[slot, part] for part in range(MIX_PARTS))
    _ret_prompt_compute(q_ref, k_ref, v_ref, g_ref, cos_ref, sin_ref, din_ref, dq_ref, dk_ref, dc_ref,
                        rnw_ref, ret_ref, s_ref)
    _pool_prompt_compute(u_ref, pw_ref, ps_ref, pool_ref, halo_ref)

    @pl.when(t == pl.num_programs(1) - 1)
    def _():
        s_out_ref[...] = s_ref[...]
        buf_ref[...] = u_ref[MIX_P_ROWS - POOL_BUF:, :]


def mix_prompt(z, cos, sin, din, dq, dk, dc, rnw, pool_w, pool_scale, layer, state_prev, buf_prev):
    nt = SEQ // MIX_P_ROWS
    extra = _carried(state_prev) + _carried(buf_prev)
    whole = lambda a: pl.BlockSpec(a.shape, lambda b, t: (0,) * a.ndim)
    rope = pl.BlockSpec((MIX_P_ROWS, RET_HEAD_DIM // 2), lambda b, t: (t, 0))
    state_block = (RET_HEADS, RET_HEAD_DIM, RET_HEAD_DIM)
    mix_spec = pl.BlockSpec((MIX_P_ROWS, D_RET), lambda b, t: (b * nt + t, 0))
    n_in = 10
    return pl.pallas_call(
        _skip_refs(_mix_prompt_kernel, n_in, len(extra)),
        out_shape=(jax.ShapeDtypeStruct((M_ROWS, D_RET), BF16),
                   jax.ShapeDtypeStruct((DEPTH, BATCH) + state_block, F32),
                   jax.ShapeDtypeStruct((M_ROWS, D_POOL), BF16),
                   jax.ShapeDtypeStruct((DEPTH, BATCH, POOL_BUF, D_POOL), F32)),
        grid=(BATCH, nt),
        in_specs=[
            _ANY, rope, rope,
            whole(din), whole(dq), whole(dk), whole(dc),
            pl.BlockSpec((None, 1, D_RET), lambda b, t: (layer, 0, 0)),
            pl.BlockSpec((None, 4, POOL_GROUP_DIM, POOL_GROUP_DIM), lambda b, t: (layer, 0, 0, 0)),
            pl.BlockSpec((None, 1, D_POOL), lambda b, t: (layer, 0, 0)),
        ] + [_ANY] * len(extra),
        out_specs=(
            mix_spec,
            pl.BlockSpec((None, None) + state_block, lambda b, t: (layer, b, 0, 0, 0)),
            mix_spec,
            pl.BlockSpec((None, None, POOL_BUF, D_POOL), lambda b, t: (layer, b, 0, 0)),
        ),
        scratch_shapes=[pltpu.VMEM(state_block, F32), pltpu.VMEM((POOL_HALO, D_POOL), F32),
                        pltpu.VMEM((MIX_SLOTS, MIX_PARTS, MIX_P_ROWS, D_RET), F32),
                        pltpu.SemaphoreType.DMA((MIX_SLOTS, MIX_PARTS))],
        input_output_aliases={n_in: 1, n_in + 1: 3} if extra else {},
        compiler_params=_cparams(("arbitrary", "arbitrary")),
        name="mix_prompt",
    )(z, cos, sin, din, dq, dk, dc, rnw, pool_w, pool_scale, *extra)


def _pool_sample_kernel(u_ref, buf_ref, pw_ref, ps_ref, o_ref, nbuf_ref):

    def ext(r, cols):
        if r < POOL_BUF:
            return buf_ref[:, r, cols]
        return u_ref[(r - POOL_BUF) * S_TILE_B:(r - POOL_BUF + 1) * S_TILE_B, cols]

    for g, w in enumerate(POOL_WINDOWS):
        cols = slice(g * POOL_GROUP_DIM, (g + 1) * POOL_GROUP_DIM)
        pooled = []
        for t in range(DEC_SEQ):
            win = ext(POOL_BUF + t, cols)
            for r in range(POOL_BUF + t - w + 1, POOL_BUF + t):
                win = win + ext(r, cols)
            cnt = float(min(t + PAST_LEN + 1, w))
            pooled.append(win / cnt - ext(POOL_BUF + t, cols))
        pooled = jnp.concatenate(pooled, axis=0)
        pm = jnp.dot(pooled.astype(BF16), pw_ref[g].astype(BF16), preferred_element_type=F32)
        o_ref[:, cols] = (pm * ps_ref[:, cols]).astype(o_ref.dtype)

    full = slice(0, D_POOL)
    for r in range(POOL_BUF):
        nbuf_ref[:, r, :] = ext(r + DEC_SEQ, full)


def pool_sample(z, state_pool, pool_w, pool_scale, mix, layer, buf_prev):
    extra = [mix] + _carried(buf_prev)
    return pl.pallas_call(
        _skip_refs(_pool_sample_kernel, 4, len(extra)),
        out_shape=(jax.ShapeDtypeStruct(mix.shape, mix.dtype),
                   jax.ShapeDtypeStruct(state_pool.shape, F32)),
        grid=(N_S_TILES,),
        in_specs=[
            pl.BlockSpec((S_TILE_ROWS, D_POOL), lambda i: (S_BLK0 + i, 4)),
            pl.BlockSpec((None, S_TILE_B, POOL_BUF, D_POOL), lambda i: (layer, i, 0, 0)),
            pl.BlockSpec((None, 4, POOL_GROUP_DIM, POOL_GROUP_DIM), lambda i: (layer, 0, 0, 0)),
            pl.BlockSpec((None, 1, D_POOL), lambda i: (layer, 0, 0)),
        ] + [_ANY] * len(extra),
        out_specs=(
            pl.BlockSpec((S_TILE_ROWS, D_POOL), lambda i: (S_BLK0 + i, 0)),
            pl.BlockSpec((None, S_TILE_B, POOL_BUF, D_POOL), lambda i: (layer, i, 0, 0)),
        ),
        input_output_aliases={4 + n: n for n in range(len(extra))},
        compiler_params=_cparams(("arbitrary",)),
        name="pool_sample",
    )(z, state_pool, pool_w, pool_scale, *extra)


def _softmax_rows(s):
    m = jnp.max(s, axis=-1, keepdims=True)
    e = jnp.exp(s - m)
    return e / jnp.sum(e, axis=-1, keepdims=True)


def _xq_prompt_attention_kernel(h_ref, w_ref, mk_ref, mv_ref, o_ref, wb_ref, *, prompt_tiles):
    _cast_weight_once(w_ref, wb_ref)

    @pl.when(pl.program_id(0) < prompt_tiles)
    def _():
        half = o_ref.shape[0] // 2
        for rows in (slice(0, half), slice(half, 2 * half)):
            q = jnp.dot(h_ref[rows, :], wb_ref[...], preferred_element_type=F32).astype(BF16)
            for hd in range(MEM_HEADS):
                cols = slice(hd * MEM_HEAD_DIM, (hd + 1) * MEM_HEAD_DIM)
                s = _dot_t_rhs(q[:, cols], mk_ref[:, cols].astype(BF16)) * (MEM_HEAD_DIM ** -0.5)
                p = _softmax_rows(s)
                o_ref[rows, cols] = jnp.dot(p.astype(BF16), mv_ref[:, cols].astype(BF16),
                                            preferred_element_type=F32).astype(o_ref.dtype)

    @pl.when(pl.program_id(0) >= prompt_tiles)
    def _():
        o_ref[...] = jnp.dot(h_ref[...], wb_ref[...], preferred_element_type=F32).astype(o_ref.dtype)


def xq_and_prompt_attention(h, w, mk, mv, layer, *, tm):
    m, k = h.shape
    tiles_per_batch = SEQ // tm
    prompt_tiles = P_ROWS // tm
    kv_spec = pl.BlockSpec((None, N_MEM, k),
                           lambda i: (layer, jnp.minimum(i // tiles_per_batch, BATCH - 1), 0))
    return pl.pallas_call(
        functools.partial(_xq_prompt_attention_kernel, prompt_tiles=prompt_tiles),
        out_shape=jax.ShapeDtypeStruct((m, k), BF16),
        grid=(m // tm,),
        in_specs=[pl.BlockSpec((tm, k), lambda i: (i, 0)), _resident_weight_spec(w, layer),
                  kv_spec, kv_spec],
        out_specs=pl.BlockSpec((tm, k), lambda i: (i, 0)),
        scratch_shapes=[pltpu.VMEM((k, k), BF16)],
        compiler_params=_cparams(("arbitrary",)),
        name="xq_prompt_attention",
    )(h, w, mk, mv)


XATTN_S_ROWS = MEM_HEADS * DEC_SEQ
XATTN_S_KEYS = N_MEM * MEM_HEADS


def _sample_attention(q_ref, k_ref, v_ref, o_ref):
    row_h = lax.broadcasted_iota(jnp.int32, (XATTN_S_ROWS, XATTN_S_KEYS), 0) // DEC_SEQ
    col_h = lax.broadcasted_iota(jnp.int32, (XATTN_S_ROWS, XATTN_S_KEYS), 1) % MEM_HEADS
    k2 = k_ref[...].reshape(XATTN_S_KEYS, MEM_HEAD_DIM).astype(BF16)
    v2 = v_ref[...].reshape(XATTN_S_KEYS, MEM_HEAD_DIM).astype(BF16)
    s = _dot_t_rhs(q_ref[...], k2) * (MEM_HEAD_DIM ** -0.5)
    s = jnp.where(row_h == col_h, s, -jnp.inf)
    p = _softmax_rows(s)
    o_ref[...] = jnp.dot(p.astype(BF16), v2, preferred_element_type=F32).astype(o_ref.dtype)


def _sample_rows_to_heads(a):
    a = a.reshape(N_S_TILES, DEC_SEQ, S_TILE_B, MEM_HEADS, MEM_HEAD_DIM)
    return a.transpose(0, 2, 3, 1, 4).reshape(DEC_BATCH, XATTN_S_ROWS, MEM_HEAD_DIM)


def _heads_to_sample_rows(a):
    a = a.reshape(N_S_TILES, S_TILE_B, MEM_HEADS, DEC_SEQ, MEM_HEAD_DIM)
    return a.transpose(0, 3, 1, 2, 4).reshape(S_ROWS, D_MODEL)


FINAL_ROWS = 512


def _final_norm_kernel(x_ref, nw_ref, y_ref):
    def body(rows):
        y_ref[rows, :] = _rms_rows(x_ref[rows, :], nw_ref[...])
    _for_row_chunks(FINAL_ROWS, _norm_chunk(FINAL_ROWS), body)


def final_norm(x, nw):
    m, k = x.shape
    tile = pl.BlockSpec((FINAL_ROWS, k), lambda i: (i, 0))
    return pl.pallas_call(
        _final_norm_kernel,
        out_shape=jax.ShapeDtypeStruct((m, k), F32),
        grid=(m // FINAL_ROWS,),
        in_specs=[tile, pl.BlockSpec((1, k), lambda i: (0, 0))],
        out_specs=tile,
        compiler_params=_cparams(("parallel",)),
        name="final_norm",
    )(x, nw)


def _rope_tables(pos):
    half = RET_HEAD_DIM // 2
    inv = ROPE_BASE ** (-jnp.arange(half, dtype=F32) / half)
    ang = pos.astype(F32)[:, None] * inv[None, :]
    return jnp.cos(ang), jnp.sin(ang)


def _log_gamma():
    return jnp.log1p(-jnp.exp2(-5.0 - jnp.arange(RET_HEADS, dtype=F32)))


def _decay_tables(chunk):
    lg = _log_gamma()
    idx = jnp.arange(chunk, dtype=F32)
    diff = idx[:, None] - idx[None, :]
    decay_in = jnp.where(diff[None] >= 0.0,
                         jnp.exp(lg[:, None, None] * jnp.maximum(diff, 0.0)[None]), 0.0)
    decay_q = jnp.exp(lg[:, None] * (idx[None, :] + 1.0))
    decay_k = jnp.exp(lg[:, None] * (chunk - 1.0 - idx[None, :]))
    decay_chunk = jnp.exp(lg * chunk)
    return decay_in, decay_q, decay_k, decay_chunk


def _to_sample_rows(a):
    d = a.shape[-1]
    return a.reshape(N_S_TILES, S_TILE_B, DEC_SEQ, d).transpose(0, 2, 1, 3).reshape(S_ROWS, d)


def _from_sample_rows(a):
    d = a.shape[-1]
    return a.reshape(N_S_TILES, DEC_SEQ, S_TILE_B, d).transpose(0, 2, 1, 3).reshape(DEC_BATCH, DEC_SEQ, d)


def kernel(x_prompt, x_sample, mem_prompt, state_ret, state_pool, cache_mem_k, cache_mem_v,
           attn_norm_w, w_in, ret_norm_w, pool_w, pool_scale, w_out, xattn_norm_w, mem_norm_w,
           w_xq, w_mk, w_mv, w_xo, mlp_norm_w, w_up, w_down, final_norm_w):
    cos_p, sin_p = _rope_tables(jnp.arange(SEQ))
    cos_s, sin_s = _rope_tables(jnp.arange(DEC_SEQ) + PAST_LEN)
    cos_s = jnp.repeat(cos_s, S_TILE_B, axis=0)
    sin_s = jnp.repeat(sin_s, S_TILE_B, axis=0)
    din_p, dq_p, dk_p, dc_p = _decay_tables(RET_CHUNK)
    half = RET_HEAD_DIM // 2
    dq_p = jnp.broadcast_to(dq_p[:, :, None], (RET_HEADS, RET_CHUNK, half))
    dk_p = jnp.broadcast_to(dk_p[:, :, None], (RET_HEADS, RET_CHUNK, half))
    dc_p = jnp.broadcast_to(dc_p[:, None, None], (RET_HEADS, 1, RET_HEAD_DIM))
    din_s, dq_s, dk_s, dc_s = _decay_tables(DEC_SEQ)
    dec_s = jnp.concatenate([din_s.reshape(RET_HEADS, DEC_SEQ * DEC_SEQ), dq_s, dk_s,
                             dc_s[:, None], jnp.zeros((RET_HEADS, 7), F32)], axis=1)

    row3 = lambda a: a.reshape(DEPTH, 1, a.shape[-1])
    attn_nw, xattn_nw, mem_nw, mlp_nw = map(row3, (attn_norm_w, xattn_norm_w, mem_norm_w, mlp_norm_w))
    ret_nw, pool_sc = row3(ret_norm_w), row3(pool_scale)
    w_in_b = w_in.astype(BF16)

    x_sources = [x_prompt.reshape(P_ROWS, D_MODEL), _to_sample_rows(x_sample)]
    mem = mem_prompt.reshape(BATCH * N_MEM, D_MODEL)

    ret_p = buf_p = mk_p = mv_p = ret_s = buf_s = None
    for l in range(DEPTH):
        z, w_up_b, w_down_b = in_proj_and_mlp_weight_cast(x_sources, attn_nw, w_in_b, w_up, w_down, l,
                                                          tm=TM_IN_PROJ)
        mix_r, ret_p, mix_p, buf_p = mix_prompt(z, cos_p, sin_p, din_p, dq_p, dk_p, dc_p, ret_nw,
                                                pool_w, pool_sc, l, ret_p, buf_p)
        mix_r, ret_s = retention_sample(dec_s, z, cos_s, sin_s, state_ret, ret_nw, mix_r, l, ret_s)
        mix_p, buf_s = pool_sample(z, state_pool, pool_w, pool_sc, mix_p, l, buf_s)
        x, h = matmul_residual_resident([mix_r, mix_p], w_out, x_sources, xattn_nw, l, tm=TM_RESIDENT)

        mk_p, mv_p = mem_kv_stacked(mem, mem_nw, w_mk, w_mv, l, mk_p, mv_p, tm=1024, tn=512)
        att = xq_and_prompt_attention(h, w_xq, mk_p, mv_p, l, tm=TM_RESIDENT)
        xp, hp = matmul_residual_resident([att], w_xo, [x], mlp_nw, l, tm=TM_RESIDENT,
                                          n_tiles=P_ROWS // TM_RESIDENT)
        qs = _sample_rows_to_heads(att[P_ROWS:])
        xp, att_s = mlp_residual(hp, xp, w_up_b, w_down_b, tm=TM_MLP,
                                 sample_attention=(qs, cache_mem_k, cache_mem_v, l))
        xs, hs = matmul_residual_resident([_heads_to_sample_rows(att_s)], w_xo, [x[P_ROWS:]], mlp_nw, l,
                                          tm=TM_RESIDENT)
        xs = mlp_residual(hs, xs, w_up_b, w_down_b, tm=S_ROWS)
        x_sources = [xp, xs]

    nw = final_norm_w.reshape(1, D_MODEL)
    y_p, y_s = final_norm(xp, nw), final_norm(xs, nw)
    mem_shape = (DEPTH, BATCH, N_MEM, MEM_HEADS, MEM_HEAD_DIM)
    return (y_p.reshape(BATCH, SEQ, D_MODEL), _from_sample_rows(y_s),
            ret_p, buf_p, mk_p.reshape(mem_shape), mv_p.reshape(mem_shape), ret_s, buf_s)
```

```python
import functools

import jax
import jax.numpy as jnp
from jax import lax
from jax.experimental import pallas as pl
from jax.experimental.pallas import tpu as pltpu

D_MODEL = 2048
BATCH = 4
SEQ = 2048
DEPTH = 2
DEC_BATCH = 128
DEC_SEQ = 4
PAST_LEN = 16384
D_RET = 1024
RET_HEADS = 4
RET_HEAD_DIM = 256
D_POOL = 1024
POOL_WINDOWS = (2, 4, 8, 16)
POOL_GROUP_DIM = 256
POOL_BUF = 15
D_IN = 5120
N_MEM = 256
MEM_HEADS = 4
MEM_HEAD_DIM = 512
D_FF = 8192
RET_CHUNK = 128
ROPE_BASE = 10000.0
EPS = 1e-6

F32 = jnp.float32
BF16 = jnp.bfloat16

P_ROWS = BATCH * SEQ
S_ROWS = DEC_BATCH * DEC_SEQ
M_ROWS = P_ROWS + S_ROWS
S_TILE_B = 8
S_TILE_ROWS = S_TILE_B * DEC_SEQ
N_S_TILES = DEC_BATCH // S_TILE_B
S_BLK0 = P_ROWS // S_TILE_ROWS

TM_MLP = 1024
TM_RESIDENT = 512
TM_IN_PROJ = 256
VMEM_LIMIT = 58 * 1024 * 1024


def _cparams(sem):
    return pltpu.CompilerParams(dimension_semantics=sem, vmem_limit_bytes=VMEM_LIMIT)


_ANY = pl.BlockSpec(memory_space=pl.ANY)


def _skip_refs(kernel_fn, start, count):
    def wrapped(*refs):
        return kernel_fn(*refs[:start], *refs[start + count:])
    return wrapped


def _carried(prev):
    return [] if prev is None else [prev]


def _stacked_row_specs(sources, tm):
    specs, ends, start = [], [], 0
    for a in sources:
        n = a.shape[0] // tm
        specs.append(pl.BlockSpec((tm, a.shape[1]),
                                  lambda i, start=start, n=n: (jnp.clip(i - start, 0, n - 1), 0)))
        start += n
        ends.append(start)
    return specs, tuple(ends)


def _read_stacked_rows(refs, ends, rows=slice(None)):
    i = pl.program_id(0)
    x = refs[-1][rows, :]
    for ref, end in zip(reversed(refs[:-1]), reversed(ends[:-1])):
        x = jnp.where(i < end, ref[rows, :], x)
    return x


def _rms_rows(x, nw):
    ms = jnp.mean(x * x, axis=-1, keepdims=True)
    return x * lax.rsqrt(ms + EPS) * nw


def _for_row_chunks(n_rows, chunk, body):
    def step(c, carry):
        body(pl.ds(pl.multiple_of(c * chunk, chunk), chunk))
        return carry
    lax.fori_loop(0, n_rows // chunk, step, 0)


def _norm_chunk(n_rows):
    return 272 if n_rows % 272 == 0 else 128


def _cast_weight_once(w_ref, wb_ref):
    @pl.when(pl.program_id(0) == 0)
    def _():
        def body(rows):
            wb_ref[rows, :] = w_ref[rows, :].astype(BF16)
        _for_row_chunks(w_ref.shape[0], 256, body)


def _mem_kv_kernel(x_ref, nw_ref, wk_ref, wv_ref, k_ref, v_ref, h_ref, *, tm):
    @pl.when(pl.program_id(1) == 0)
    def _():
        def body(rows):
            h_ref[rows, :] = _rms_rows(x_ref[rows, :], nw_ref[...]).astype(BF16)
        _for_row_chunks(tm, _norm_chunk(tm), body)

    k_ref[...] = jnp.dot(h_ref[...], wk_ref[...].astype(BF16), preferred_element_type=F32)
    v_ref[...] = jnp.dot(h_ref[...], wv_ref[...].astype(BF16), preferred_element_type=F32)


def mem_kv_stacked(x, nw, w_k, w_v, layer, prev_k, prev_v, *, tm, tn):
    m, k = x.shape
    n = w_k.shape[-1]
    extra = _carried(prev_k) + _carried(prev_v)
    w_spec = pl.BlockSpec((None, k, tn), lambda i, j: (layer, 0, j))
    o_spec = pl.BlockSpec((None, tm, tn), lambda i, j: (layer, i, j))
    stacked = jax.ShapeDtypeStruct((DEPTH, m, n), F32)
    return pl.pallas_call(
        _skip_refs(functools.partial(_mem_kv_kernel, tm=tm), 4, len(extra)),
        out_shape=(stacked, stacked),
        grid=(m // tm, n // tn),
        in_specs=[
            pl.BlockSpec((tm, k), lambda i, j: (i, 0)),
            pl.BlockSpec((None, 1, k), lambda i, j: (layer, 0, 0)),
            w_spec, w_spec,
        ] + [_ANY] * len(extra),
        out_specs=(o_spec, o_spec),
        scratch_shapes=[pltpu.VMEM((tm, k), BF16)],
        input_output_aliases={4: 0, 5: 1} if extra else {},
        compiler_params=_cparams(("parallel", "arbitrary")),
        name="mem_kv",
    )(x, nw, w_k, w_v, *extra)


def _in_proj_kernel(*refs, src_ends):
    x_refs = refs[:len(src_ends)]
    nw_ref, w_ref, wu_ref, wd_ref, o_ref, wub_ref, wdb_ref = refs[len(src_ends):]
    h = _rms_rows(_read_stacked_rows(x_refs, src_ends), nw_ref[...]).astype(BF16)
    o_ref[...] = jnp.dot(h, w_ref[...], preferred_element_type=F32)
    wub_ref[...] = wu_ref[...].astype(BF16)
    wdb_ref[...] = wd_ref[...].astype(BF16)


def in_proj_and_mlp_weight_cast(x_sources, nw, w, w_up, w_down, layer, *, tm):
    x_specs, src_ends = _stacked_row_specs(x_sources, tm)
    m = src_ends[-1] * tm
    k, n = w.shape[1:]
    ff = w_up.shape[-1]
    slab = 256
    last = ff // slab - 1
    assert src_ends[-1] > last
    return pl.pallas_call(
        functools.partial(_in_proj_kernel, src_ends=src_ends),
        out_shape=(jax.ShapeDtypeStruct((m, n), F32),
                   jax.ShapeDtypeStruct((k, ff), BF16),
                   jax.ShapeDtypeStruct((ff, k), BF16)),
        grid=(src_ends[-1],),
        in_specs=x_specs + [
            pl.BlockSpec((None, 1, k), lambda i: (layer, 0, 0)),
            pl.BlockSpec((None, k, n), lambda i: (layer, 0, 0), pipeline_mode=pl.Buffered(1)),
            pl.BlockSpec((None, k, slab), lambda i: (layer, 0, jnp.minimum(i, last))),
            pl.BlockSpec((None, slab, k), lambda i: (layer, jnp.minimum(i, last), 0)),
        ],
        out_specs=(
            pl.BlockSpec((tm, n), lambda i: (i, 0)),
            pl.BlockSpec((k, slab), lambda i: (0, jnp.minimum(i, last))),
            pl.BlockSpec((slab, k), lambda i: (jnp.minimum(i, last), 0)),
        ),
        compiler_params=_cparams(("arbitrary",)),
        name="in_proj",
    )(*x_sources, nw, w, w_up, w_down)


def _resident_weight_spec(w, layer):
    return pl.BlockSpec((None,) + w.shape[1:], lambda i: (layer, 0, 0), pipeline_mode=pl.Buffered(1))


def _matmul_res_resident_kernel(*refs, n_parts, res_ends):
    a_refs = refs[:n_parts]
    w_ref = refs[n_parts]
    r_refs = refs[n_parts + 1:n_parts + 1 + len(res_ends)]
    nw_ref, o_ref, h_ref, wb_ref = refs[n_parts + 1 + len(res_ends):]
    _cast_weight_once(w_ref, wb_ref)
    half = o_ref.shape[0] // 2
    for rows in (slice(0, half), slice(half, 2 * half)):
        acc = _read_stacked_rows(r_refs, res_ends, rows)
        k0 = 0
        for a_ref in a_refs:
            k1 = k0 + a_ref.shape[1]
            acc = acc + jnp.dot(a_ref[rows, :], wb_ref[k0:k1, :], preferred_element_type=F32)
            k0 = k1
        o_ref[rows, :] = acc
        h_ref[rows, :] = _rms_rows(acc, nw_ref[...]).astype(BF16)


def matmul_residual_resident(a_parts, w, res_sources, nw, layer, *, tm, n_tiles=None):
    res_specs, res_ends = _stacked_row_specs(res_sources, tm)
    m, n = (n_tiles or res_ends[-1]) * tm, w.shape[-1]
    k = sum(a.shape[1] for a in a_parts)
    row_tile = lambda width: pl.BlockSpec((tm, width), lambda i: (i, 0))
    return pl.pallas_call(
        functools.partial(_matmul_res_resident_kernel, n_parts=len(a_parts), res_ends=res_ends),
        out_shape=(jax.ShapeDtypeStruct((m, n), F32), jax.ShapeDtypeStruct((m, n), BF16)),
        grid=(m // tm,),
        in_specs=[row_tile(a.shape[1]) for a in a_parts] + [_resident_weight_spec(w, layer)]
        + res_specs + [pl.BlockSpec((None, 1, n), lambda i: (layer, 0, 0))],
        out_specs=(row_tile(n), row_tile(n)),
        scratch_shapes=[pltpu.VMEM((k, n), BF16)],
        compiler_params=_cparams(("arbitrary",)),
        name="matmul_residual_resident",
    )(*a_parts, w, *res_sources, nw)


MLP_TF = 1024
MLP_TN = 256
MLP_UP_STEPS = D_FF // MLP_TF
MLP_DOWN_STEPS = D_MODEL // MLP_TN
MLP_STEPS = MLP_UP_STEPS + MLP_DOWN_STEPS


def _mlp_kernel(h_ref, x_ref, wu_ref, wd_ref, *refs):
    if len(refs) == 2:
        o_ref, hid_ref = refs
        attention_job = None
    else:
        q_ref, k_ref, v_ref, o_ref, att_ref, hid_ref = refs
        attention_job = (q_ref, k_ref, v_ref, att_ref)
    s = pl.program_id(1)

    @pl.when(s < MLP_UP_STEPS)
    def _():
        hid = jnp.dot(h_ref[...], wu_ref[...], preferred_element_type=F32)
        c0 = pl.multiple_of(s * MLP_TF, MLP_TF)
        hid_ref[:, pl.ds(c0, MLP_TF)] = jnp.square(jnp.maximum(hid, 0.0)).astype(BF16)
        if attention_job is not None:
            _sample_attention(*attention_job)

    @pl.when(s >= MLP_UP_STEPS)
    def _():
        o_ref[...] = x_ref[...] + jnp.dot(hid_ref[...], wd_ref[...], preferred_element_type=F32)
        if attention_job is not None:
            _sample_attention(*attention_job)


def mlp_residual(h, x, w_up, w_down, *, tm, sample_attention=None):
    m, k = x.shape
    up = lambda s: jnp.minimum(s, MLP_UP_STEPS - 1)
    down = lambda s: jnp.maximum(s - MLP_UP_STEPS, 0)
    out_tile = pl.BlockSpec((tm, MLP_TN), lambda i, s: (i, down(s)))
    in_specs = [
        pl.BlockSpec((tm, k), lambda i, s: (i, 0)),
        out_tile,
        pl.BlockSpec((k, MLP_TF), lambda i, s: (0, up(s))),
        pl.BlockSpec((D_FF, MLP_TN), lambda i, s: (0, down(s))),
    ]
    out_shape = [jax.ShapeDtypeStruct((m, k), F32)]
    out_specs = [out_tile]
    operands = [h, x, w_up, w_down]
    if sample_attention is not None:
        qs, cache_k, cache_v, layer = sample_attention
        assert (m // tm) * MLP_STEPS == qs.shape[0]
        batch = lambda i, s: i * MLP_STEPS + s
        q_spec = pl.BlockSpec((None,) + qs.shape[1:], lambda i, s: (batch(i, s), 0, 0))
        kv_spec = pl.BlockSpec((None, None) + cache_k.shape[2:], lambda i, s: (layer, batch(i, s), 0, 0, 0))
        in_specs += [q_spec, kv_spec, kv_spec]
        out_shape += [jax.ShapeDtypeStruct(qs.shape, BF16)]
        out_specs += [q_spec]
        operands += [qs, cache_k, cache_v]
    out = pl.pallas_call(
        _mlp_kernel,
        out_shape=tuple(out_shape),
        grid=(m // tm, MLP_STEPS),
        in_specs=in_specs,
        out_specs=tuple(out_specs),
        scratch_shapes=[pltpu.VMEM((tm, D_FF), BF16)],
        compiler_params=_cparams(("arbitrary", "arbitrary")),
        name="mlp_residual",
    )(*operands)
    return out[0] if sample_attention is None else out


def _rotate(x, cos, sin):
    half = x.shape[-1] // 2
    x1, x2 = x[:, :half], x[:, half:]
    return jnp.concatenate([x1 * cos - x2 * sin, x2 * cos + x1 * sin], axis=-1)


def _head_norm_gate(o, g, rnw):
    o = o * lax.rsqrt(jnp.mean(o * o, axis=-1, keepdims=True) + EPS)
    return (g * jax.nn.sigmoid(g)) * (o * rnw)


def _dot_t_lhs(a, b):
    return lax.dot_general(a, b, (((0,), (0,)), ((), ())), preferred_element_type=F32)


def _dot_t_rhs(a, b):
    return lax.dot_general(a, b, (((1,), (1,)), ((), ())), preferred_element_type=F32)


MIX_P_ROWS = 512


def _ret_prompt_compute(q_ref, k_ref, v_ref, g_ref, cos_ref, sin_ref, din_ref, dq_ref, dk_ref, dc_ref,
                        rnw_ref, o_ref, s_ref):
    for c in range(MIX_P_ROWS // RET_CHUNK):
        rows = slice(c * RET_CHUNK, (c + 1) * RET_CHUNK)
        cos, sin = cos_ref[rows, :], sin_ref[rows, :]
        for h in range(RET_HEADS):
            cols = slice(h * RET_HEAD_DIM, (h + 1) * RET_HEAD_DIM)
            dq = jnp.concatenate([dq_ref[h], dq_ref[h]], axis=-1)
            dk = jnp.concatenate([dk_ref[h], dk_ref[h]], axis=-1)
            q = _rotate(q_ref[rows, cols], cos, sin)
            k = _rotate(k_ref[rows, cols], cos, sin) * (RET_HEAD_DIM ** -0.5)
            vb = v_ref[rows, cols].astype(BF16)
            s = s_ref[h]
            scores = _dot_t_rhs(q.astype(BF16), k.astype(BF16)) * din_ref[h]
            o = jnp.dot(scores.astype(BF16), vb, preferred_element_type=F32)
            o = o + jnp.dot((q * dq).astype(BF16), s.astype(BF16), preferred_element_type=F32)
            s_ref[h] = s * dc_ref[h] + _dot_t_lhs((k * dk).astype(BF16), vb)
            o_ref[rows, cols] = _head_norm_gate(o, g_ref[rows, cols], rnw_ref[:, cols]).astype(o_ref.dtype)


def _ret_sample_head(h, dec_ref, z_ref, cos, sin, st_ref, rnw_ref, o_ref, st_out_ref):
    cols = slice(h * RET_HEAD_DIM, (h + 1) * RET_HEAD_DIM)
    part = lambda p: z_ref[:, p * D_RET + h * RET_HEAD_DIM:p * D_RET + (h + 1) * RET_HEAD_DIM]
    q = _rotate(part(0), cos, sin)
    k = _rotate(part(1), cos, sin) * (RET_HEAD_DIM ** -0.5)
    v = part(2)
    g = part(3)
    slab = lambda a, t: a[t * S_TILE_B:(t + 1) * S_TILE_B, :]

    intra = []
    for t in range(DEC_SEQ):
        acc = None
        for j in range(t + 1):
            w = jnp.sum(slab(q, t) * slab(k, j), axis=-1, keepdims=True) * dec_ref[h, t * DEC_SEQ + j]
            acc = w * slab(v, j) if acc is None else acc + w * slab(v, j)
        intra.append(acc)
    o = jnp.concatenate(intra, axis=0)

    dq_rows = jnp.concatenate(
        [jnp.full((S_TILE_B, 1), 1.0, F32) * dec_ref[h, 16 + t] for t in range(DEC_SEQ)], axis=0)
    dk_rows = jnp.concatenate(
        [jnp.full((S_TILE_B, 1), 1.0, F32) * dec_ref[h, 20 + t] for t in range(DEC_SEQ)], axis=0)
    qd = (q * dq_rows).astype(BF16)
    kd = k * dk_rows
    vb = v.astype(BF16)
    dchunk = dec_ref[h, 24]
    row_b = lax.broadcasted_iota(jnp.int32, (S_TILE_ROWS, 1), 0) % S_TILE_B
    for b in range(S_TILE_B):
        mine = row_b == b
        s = st_ref[b, h]
        o = o + jnp.where(mine, jnp.dot(qd, s.astype(BF16), preferred_element_type=F32), 0.0)
        st_out_ref[b, h] = s * dchunk + _dot_t_lhs(jnp.where(mine, kd, 0.0).astype(BF16), vb)
    o_ref[:, cols] = _head_norm_gate(o, g, rnw_ref[:, cols]).astype(o_ref.dtype)


RET_S_SLOTS = 3


def _ret_state_copy(st_hbm, stbuf, stsem, layer, tile):
    slot = tile % RET_S_SLOTS
    return pltpu.make_async_copy(st_hbm.at[layer, pl.ds(tile * S_TILE_B, S_TILE_B)],
                                 stbuf.at[slot], stsem.at[slot])


def _ret_sample_kernel(dec_ref, z_ref, cos_ref, sin_ref, st_hbm, rnw_ref, o_ref, st_out_ref,
                       stbuf, stsem, *, layer):
    i = pl.program_id(0)
    n = pl.num_programs(0)

    @pl.when(i == 0)
    def _():
        _ret_state_copy(st_hbm, stbuf, stsem, layer, 0).start()
        _ret_state_copy(st_hbm, stbuf, stsem, layer, 1).start()

    @pl.when(i + 2 < n)
    def _():
        _ret_state_copy(st_hbm, stbuf, stsem, layer, i + 2).start()

    _ret_state_copy(st_hbm, stbuf, stsem, layer, i).wait()
    st_ref = stbuf.at[i % RET_S_SLOTS]
    cos, sin = cos_ref[...], sin_ref[...]
    for h in range(RET_HEADS):
        _ret_sample_head(h, dec_ref, z_ref, cos, sin, st_ref, rnw_ref, o_ref, st_out_ref)


def retention_sample(dec, z, cos_s, sin_s, state_ret, rnw, mix, layer, state_prev):
    extra = [mix] + _carried(state_prev)
    state_spec = pl.BlockSpec((None, S_TILE_B, RET_HEADS, RET_HEAD_DIM, RET_HEAD_DIM),
                              lambda i: (layer, i, 0, 0, 0))
    return pl.pallas_call(
        _skip_refs(functools.partial(_ret_sample_kernel, layer=layer), 6, len(extra)),
        out_shape=(jax.ShapeDtypeStruct(mix.shape, mix.dtype),
                   jax.ShapeDtypeStruct(state_ret.shape, F32)),
        grid=(N_S_TILES,),
        in_specs=[
            pl.BlockSpec(memory_space=pltpu.SMEM),
            pl.BlockSpec((S_TILE_ROWS, D_IN), lambda i: (S_BLK0 + i, 0)),
            pl.BlockSpec((S_TILE_ROWS, RET_HEAD_DIM // 2), lambda i: (0, 0)),
            pl.BlockSpec((S_TILE_ROWS, RET_HEAD_DIM // 2), lambda i: (0, 0)),
            _ANY,
            pl.BlockSpec((None, 1, D_RET), lambda i: (layer, 0, 0)),
        ] + [_ANY] * len(extra),
        out_specs=(pl.BlockSpec((S_TILE_ROWS, D_RET), lambda i: (S_BLK0 + i, 0)), state_spec),
        scratch_shapes=[pltpu.VMEM((RET_S_SLOTS, S_TILE_B, RET_HEADS, RET_HEAD_DIM, RET_HEAD_DIM), F32),
                        pltpu.SemaphoreType.DMA((RET_S_SLOTS,))],
        input_output_aliases={6 + n: n for n in range(len(extra))},
        compiler_params=_cparams(("arbitrary",)),
        name="retention_sample",
    )(dec, z, cos_s, sin_s, state_ret, rnw, *extra)


POOL_HALO = 16


def _shift_rows(a, s):
    return pltpu.roll(a, s, axis=0)


def _pool_prompt_compute(u_ref, pw_ref, ps_ref, o_ref, halo_ref):
    pos = pl.program_id(1) * MIX_P_ROWS + lax.broadcasted_iota(jnp.int32, (MIX_P_ROWS, 1), 0)
    for g, w in enumerate(POOL_WINDOWS):
        cols = slice(g * POOL_GROUP_DIM, (g + 1) * POOL_GROUP_DIM)
        u = u_ref[:, cols]
        a = jnp.concatenate([halo_ref[:, cols], u], axis=0)
        s = 1
        while s < w:
            a = a + _shift_rows(a, s)
            s *= 2
        cnt = jnp.minimum(pos + 1, w).astype(F32)
        pooled = a[POOL_HALO:, :] / cnt - u
        pm = jnp.dot(pooled.astype(BF16), pw_ref[g].astype(BF16), preferred_element_type=F32)
        o_ref[:, cols] = (pm * ps_ref[:, cols]).astype(o_ref.dtype)
    halo_ref[...] = u_ref[MIX_P_ROWS - POOL_HALO:, :]


MIX_PARTS = 5
MIX_SLOTS = 3


def _mix_tile_copy(z_hbm, zbuf, zsem, step, part):
    slot = step % MIX_SLOTS
    src = z_hbm.at[pl.ds(step * MIX_P_ROWS, MIX_P_ROWS), pl.ds(part * D_RET, D_RET)]
    return pltpu.make_async_copy(src, zbuf.at[slot, part], zsem.at[slot, part])


def _mix_prompt_kernel(z_hbm, cos_ref, sin_ref, din_ref, dq_ref, dk_ref, dc_ref, rnw_ref, pw_ref, ps_ref,
                       ret_ref, s_out_ref, pool_ref, buf_ref, s_ref, halo_ref, zbuf, zsem):
    t = pl.program_id(1)
    step = pl.program_id(0) * pl.num_programs(1) + t
    n_steps = pl.num_programs(0) * pl.num_programs(1)

    def start(tile):
        for part in range(MIX_PARTS):
            _mix_tile_copy(z_hbm, zbuf, zsem, tile, part).start()

    @pl.when(step == 0)
    def _():
        start(0)
        start(1)

    @pl.when(step + 2 < n_steps)
    def _():
        start(step + 2)

    for part in range(MIX_PARTS):
        _mix_tile_copy(z_hbm, zbuf, zsem, step, part).wait()

    @pl.when(t == 0)
    def _():
        s_ref[...] = jnp.zeros_like(s_ref)
        halo_ref[...] = jnp.zeros_like(halo_ref)

    slot = step % MIX_SLOTS
    q_ref, k_ref, v_ref, g_ref, u_ref = (zbuf.at[slot, part] for part in range(MIX_PARTS))
    _ret_prompt_compute(q_ref, k_ref, v_ref, g_ref, cos_ref, sin_ref, din_ref, dq_ref, dk_ref, dc_ref,
                        rnw_ref, ret_ref, s_ref)
    _pool_prompt_compute(u_ref, pw_ref, ps_ref, pool_ref, halo_ref)

    @pl.when(t == pl.num_programs(1) - 1)
    def _():
        s_out_ref[...] = s_ref[...]
        buf_ref[...] = u_ref[MIX_P_ROWS - POOL_BUF:, :]


def mix_prompt(z, cos, sin, din, dq, dk, dc, rnw, pool_w, pool_scale, layer, state_prev, buf_prev):
    nt = SEQ // MIX_P_ROWS
    extra = _carried(state_prev) + _carried(buf_prev)
    whole = lambda a: pl.BlockSpec(a.shape, lambda b, t: (0,) * a.ndim)
    rope = pl.BlockSpec((MIX_P_ROWS, RET_HEAD_DIM // 2), lambda b, t: (t, 0))
    state_block = (RET_HEADS, RET_HEAD_DIM, RET_HEAD_DIM)
    mix_spec = pl.BlockSpec((MIX_P_ROWS, D_RET), lambda b, t: (b * nt + t, 0))
    n_in = 10
    return pl.pallas_call(
        _skip_refs(_mix_prompt_kernel, n_in, len(extra)),
        out_shape=(jax.ShapeDtypeStruct((M_ROWS, D_RET), BF16),
                   jax.ShapeDtypeStruct((DEPTH, BATCH) + state_block, F32),
                   jax.ShapeDtypeStruct((M_ROWS, D_POOL), BF16),
                   jax.ShapeDtypeStruct((DEPTH, BATCH, POOL_BUF, D_POOL), F32)),
        grid=(BATCH, nt),
        in_specs=[
            _ANY, rope, rope,
            whole(din), whole(dq), whole(dk), whole(dc),
            pl.BlockSpec((None, 1, D_RET), lambda b, t: (layer, 0, 0)),
            pl.BlockSpec((None, 4, POOL_GROUP_DIM, POOL_GROUP_DIM), lambda b, t: (layer, 0, 0, 0)),
            pl.BlockSpec((None, 1, D_POOL), lambda b, t: (layer, 0, 0)),
        ] + [_ANY] * len(extra),
        out_specs=(
            mix_spec,
            pl.BlockSpec((None, None) + state_block, lambda b, t: (layer, b, 0, 0, 0)),
            mix_spec,
            pl.BlockSpec((None, None, POOL_BUF, D_POOL), lambda b, t: (layer, b, 0, 0)),
        ),
        scratch_shapes=[pltpu.VMEM(state_block, F32), pltpu.VMEM((POOL_HALO, D_POOL), F32),
                        pltpu.VMEM((MIX_SLOTS, MIX_PARTS, MIX_P_ROWS, D_RET), F32),
                        pltpu.SemaphoreType.DMA((MIX_SLOTS, MIX_PARTS))],
        input_output_aliases={n_in: 1, n_in + 1: 3} if extra else {},
        compiler_params=_cparams(("arbitrary", "arbitrary")),
        name="mix_prompt",
    )(z, cos, sin, din, dq, dk, dc, rnw, pool_w, pool_scale, *extra)


def _pool_sample_kernel(u_ref, buf_ref, pw_ref, ps_ref, o_ref, nbuf_ref):

    def ext(r, cols):
        if r < POOL_BUF:
            return buf_ref[:, r, cols]
        return u_ref[(r - POOL_BUF) * S_TILE_B:(r - POOL_BUF + 1) * S_TILE_B, cols]

    for g, w in enumerate(POOL_WINDOWS):
        cols = slice(g * POOL_GROUP_DIM, (g + 1) * POOL_GROUP_DIM)
        pooled = []
        for t in range(DEC_SEQ):
            win = ext(POOL_BUF + t, cols)
            for r in range(POOL_BUF + t - w + 1, POOL_BUF + t):
                win = win + ext(r, cols)
            cnt = float(min(t + PAST_LEN + 1, w))
            pooled.append(win / cnt - ext(POOL_BUF + t, cols))
        pooled = jnp.concatenate(pooled, axis=0)
        pm = jnp.dot(pooled.astype(BF16), pw_ref[g].astype(BF16), preferred_element_type=F32)
        o_ref[:, cols] = (pm * ps_ref[:, cols]).astype(o_ref.dtype)

    full = slice(0, D_POOL)
    for r in range(POOL_BUF):
        nbuf_ref[:, r, :] = ext(r + DEC_SEQ, full)


def pool_sample(z, state_pool, pool_w, pool_scale, mix, layer, buf_prev):
    extra = [mix] + _carried(buf_prev)
    return pl.pallas_call(
        _skip_refs(_pool_sample_kernel, 4, len(extra)),
        out_shape=(jax.ShapeDtypeStruct(mix.shape, mix.dtype),
                   jax.ShapeDtypeStruct(state_pool.shape, F32)),
        grid=(N_S_TILES,),
        in_specs=[
            pl.BlockSpec((S_TILE_ROWS, D_POOL), lambda i: (S_BLK0 + i, 4)),
            pl.BlockSpec((None, S_TILE_B, POOL_BUF, D_POOL), lambda i: (layer, i, 0, 0)),
            pl.BlockSpec((None, 4, POOL_GROUP_DIM, POOL_GROUP_DIM), lambda i: (layer, 0, 0, 0)),
            pl.BlockSpec((None, 1, D_POOL), lambda i: (layer, 0, 0)),
        ] + [_ANY] * len(extra),
        out_specs=(
            pl.BlockSpec((S_TILE_ROWS, D_POOL), lambda i: (S_BLK0 + i, 0)),
            pl.BlockSpec((None, S_TILE_B, POOL_BUF, D_POOL), lambda i: (layer, i, 0, 0)),
        ),
        input_output_aliases={4 + n: n for n in range(len(extra))},
        compiler_params=_cparams(("arbitrary",)),
        name="pool_sample",
    )(z, state_pool, pool_w, pool_scale, *extra)


def _softmax_rows(s):
    m = jnp.max(s, axis=-1, keepdims=True)
    e = jnp.exp(s - m)
    return e / jnp.sum(e, axis=-1, keepdims=True)


def _xq_prompt_attention_kernel(h_ref, w_ref, mk_ref, mv_ref, o_ref, wb_ref, *, prompt_tiles):
    _cast_weight_once(w_ref, wb_ref)

    @pl.when(pl.program_id(0) < prompt_tiles)
    def _():
        half = o_ref.shape[0] // 2
        for rows in (slice(0, half), slice(half, 2 * half)):
            q = jnp.dot(h_ref[rows, :], wb_ref[...], preferred_element_type=F32).astype(BF16)
            for hd in range(MEM_HEADS):
                cols = slice(hd * MEM_HEAD_DIM, (hd + 1) * MEM_HEAD_DIM)
                s = _dot_t_rhs(q[:, cols], mk_ref[:, cols].astype(BF16)) * (MEM_HEAD_DIM ** -0.5)
                p = _softmax_rows(s)
                o_ref[rows, cols] = jnp.dot(p.astype(BF16), mv_ref[:, cols].astype(BF16),
                                            preferred_element_type=F32).astype(o_ref.dtype)

    @pl.when(pl.program_id(0) >= prompt_tiles)
    def _():
        o_ref[...] = jnp.dot(h_ref[...], wb_ref[...], preferred_element_type=F32).astype(o_ref.dtype)


def xq_and_prompt_attention(h, w, mk, mv, layer, *, tm):
    m, k = h.shape
    tiles_per_batch = SEQ // tm
    prompt_tiles = P_ROWS // tm
    kv_spec = pl.BlockSpec((None, N_MEM, k),
                           lambda i: (layer, jnp.minimum(i // tiles_per_batch, BATCH - 1), 0))
    return pl.pallas_call(
        functools.partial(_xq_prompt_attention_kernel, prompt_tiles=prompt_tiles),
        out_shape=jax.ShapeDtypeStruct((m, k), BF16),
        grid=(m // tm,),
        in_specs=[pl.BlockSpec((tm, k), lambda i: (i, 0)), _resident_weight_spec(w, layer),
                  kv_spec, kv_spec],
        out_specs=pl.BlockSpec((tm, k), lambda i: (i, 0)),
        scratch_shapes=[pltpu.VMEM((k, k), BF16)],
        compiler_params=_cparams(("arbitrary",)),
        name="xq_prompt_attention",
    )(h, w, mk, mv)


XATTN_S_ROWS = MEM_HEADS * DEC_SEQ
XATTN_S_KEYS = N_MEM * MEM_HEADS


def _sample_attention(q_ref, k_ref, v_ref, o_ref):
    row_h = lax.broadcasted_iota(jnp.int32, (XATTN_S_ROWS, XATTN_S_KEYS), 0) // DEC_SEQ
    col_h = lax.broadcasted_iota(jnp.int32, (XATTN_S_ROWS, XATTN_S_KEYS), 1) % MEM_HEADS
    k2 = k_ref[...].reshape(XATTN_S_KEYS, MEM_HEAD_DIM).astype(BF16)
    v2 = v_ref[...].reshape(XATTN_S_KEYS, MEM_HEAD_DIM).astype(BF16)
    s = _dot_t_rhs(q_ref[...], k2) * (MEM_HEAD_DIM ** -0.5)
    s = jnp.where(row_h == col_h, s, -jnp.inf)
    p = _softmax_rows(s)
    o_ref[...] = jnp.dot(p.astype(BF16), v2, preferred_element_type=F32).astype(o_ref.dtype)


def _sample_rows_to_heads(a):
    a = a.reshape(N_S_TILES, DEC_SEQ, S_TILE_B, MEM_HEADS, MEM_HEAD_DIM)
    return a.transpose(0, 2, 3, 1, 4).reshape(DEC_BATCH, XATTN_S_ROWS, MEM_HEAD_DIM)


def _heads_to_sample_rows(a):
    a = a.reshape(N_S_TILES, S_TILE_B, MEM_HEADS, DEC_SEQ, MEM_HEAD_DIM)
    return a.transpose(0, 3, 1, 2, 4).reshape(S_ROWS, D_MODEL)


FINAL_ROWS = 512


def _final_norm_kernel(x_ref, nw_ref, y_ref):
    def body(rows):
        y_ref[rows, :] = _rms_rows(x_ref[rows, :], nw_ref[...])
    _for_row_chunks(FINAL_ROWS, _norm_chunk(FINAL_ROWS), body)


def final_norm(x, nw):
    m, k = x.shape
    tile = pl.BlockSpec((FINAL_ROWS, k), lambda i: (i, 0))
    return pl.pallas_call(
        _final_norm_kernel,
        out_shape=jax.ShapeDtypeStruct((m, k), F32),
        grid=(m // FINAL_ROWS,),
        in_specs=[tile, pl.BlockSpec((1, k), lambda i: (0, 0))],
        out_specs=tile,
        compiler_params=_cparams(("parallel",)),
        name="final_norm",
    )(x, nw)


def _rope_tables(pos):
    half = RET_HEAD_DIM // 2
    inv = ROPE_BASE ** (-jnp.arange(half, dtype=F32) / half)
    ang = pos.astype(F32)[:, None] * inv[None, :]
    return jnp.cos(ang), jnp.sin(ang)


def _log_gamma():
    return jnp.log1p(-jnp.exp2(-5.0 - jnp.arange(RET_HEADS, dtype=F32)))


def _decay_tables(chunk):
    lg = _log_gamma()
    idx = jnp.arange(chunk, dtype=F32)
    diff = idx[:, None] - idx[None, :]
    decay_in = jnp.where(diff[None] >= 0.0,
                         jnp.exp(lg[:, None, None] * jnp.maximum(diff, 0.0)[None]), 0.0)
    decay_q = jnp.exp(lg[:, None] * (idx[None, :] + 1.0))
    decay_k = jnp.exp(lg[:, None] * (chunk - 1.0 - idx[None, :]))
    decay_chunk = jnp.exp(lg * chunk)
    return decay_in, decay_q, decay_k, decay_chunk


def _to_sample_rows(a):
    d = a.shape[-1]
    return a.reshape(N_S_TILES, S_TILE_B, DEC_SEQ, d).transpose(0, 2, 1, 3).reshape(S_ROWS, d)


def _from_sample_rows(a):
    d = a.shape[-1]
    return a.reshape(N_S_TILES, DEC_SEQ, S_TILE_B, d).transpose(0, 2, 1, 3).reshape(DEC_BATCH, DEC_SEQ, d)


def kernel(x_prompt, x_sample, mem_prompt, state_ret, state_pool, cache_mem_k, cache_mem_v,
           attn_norm_w, w_in, ret_norm_w, pool_w, pool_scale, w_out, xattn_norm_w, mem_norm_w,
           w_xq, w_mk, w_mv, w_xo, mlp_norm_w, w_up, w_down, final_norm_w):
    cos_p, sin_p = _rope_tables(jnp.arange(SEQ))
    cos_s, sin_s = _rope_tables(jnp.arange(DEC_SEQ) + PAST_LEN)
    cos_s = jnp.repeat(cos_s, S_TILE_B, axis=0)
    sin_s = jnp.repeat(sin_s, S_TILE_B, axis=0)
    din_p, dq_p, dk_p, dc_p = _decay_tables(RET_CHUNK)
    half = RET_HEAD_DIM // 2
    dq_p = jnp.broadcast_to(dq_p[:, :, None], (RET_HEADS, RET_CHUNK, half))
    dk_p = jnp.broadcast_to(dk_p[:, :, None], (RET_HEADS, RET_CHUNK, half))
    dc_p = jnp.broadcast_to(dc_p[:, None, None], (RET_HEADS, 1, RET_HEAD_DIM))
    din_s, dq_s, dk_s, dc_s = _decay_tables(DEC_SEQ)
    dec_s = jnp.concatenate([din_s.reshape(RET_HEADS, DEC_SEQ * DEC_SEQ), dq_s, dk_s,
                             dc_s[:, None], jnp.zeros((RET_HEADS, 7), F32)], axis=1)

    row3 = lambda a: a.reshape(DEPTH, 1, a.shape[-1])
    attn_nw, xattn_nw, mem_nw, mlp_nw = map(row3, (attn_norm_w, xattn_norm_w, mem_norm_w, mlp_norm_w))
    ret_nw, pool_sc = row3(ret_norm_w), row3(pool_scale)
    w_in_b = w_in.astype(BF16)

    x_sources = [x_prompt.reshape(P_ROWS, D_MODEL), _to_sample_rows(x_sample)]
    mem = mem_prompt.reshape(BATCH * N_MEM, D_MODEL)

    ret_p = buf_p = mk_p = mv_p = ret_s = buf_s = None
    for l in range(DEPTH):
        z, w_up_b, w_down_b = in_proj_and_mlp_weight_cast(x_sources, attn_nw, w_in_b, w_up, w_down, l,
                                                          tm=TM_IN_PROJ)
        mix_r, ret_p, mix_p, buf_p = mix_prompt(z, cos_p, sin_p, din_p, dq_p, dk_p, dc_p, ret_nw,
                                                pool_w, pool_sc, l, ret_p, buf_p)
        mix_r, ret_s = retention_sample(dec_s, z, cos_s, sin_s, state_ret, ret_nw, mix_r, l, ret_s)
        mix_p, buf_s = pool_sample(z, state_pool, pool_w, pool_sc, mix_p, l, buf_s)
        x, h = matmul_residual_resident([mix_r, mix_p], w_out, x_sources, xattn_nw, l, tm=TM_RESIDENT)

        mk_p, mv_p = mem_kv_stacked(mem, mem_nw, w_mk, w_mv, l, mk_p, mv_p, tm=1024, tn=512)
        att = xq_and_prompt_attention(h, w_xq, mk_p, mv_p, l, tm=TM_RESIDENT)
        xp, hp = matmul_residual_resident([att], w_xo, [x], mlp_nw, l, tm=TM_RESIDENT,
                                          n_tiles=P_ROWS // TM_RESIDENT)
        qs = _sample_rows_to_heads(att[P_ROWS:])
        xp, att_s = mlp_residual(hp, xp, w_up_b, w_down_b, tm=TM_MLP,
                                 sample_attention=(qs, cache_mem_k, cache_mem_v, l))
        xs, hs = matmul_residual_resident([_heads_to_sample_rows(att_s)], w_xo, [x[P_ROWS:]], mlp_nw, l,
                                          tm=TM_RESIDENT)
        xs = mlp_residual(hs, xs, w_up_b, w_down_b, tm=S_ROWS)
        x_sources = [xp, xs]

    nw = final_norm_w.reshape(1, D_MODEL)
    y_p, y_s = final_norm(xp, nw), final_norm(xs, nw)
    mem_shape = (DEPTH, BATCH, N_MEM, MEM_HEADS, MEM_HEAD_DIM)
    return (y_p.reshape(BATCH, SEQ, D_MODEL), _from_sample_rows(y_s),
            ret_p, buf_p, mk_p.reshape(mem_shape), mv_p.reshape(mem_shape), ret_s, buf_s)
```
